```python
import jax, jax.numpy as jnp
from jax import lax
import numpy as np

D_MODEL = 1024
BATCH = 4
SEQ = 8192
DEPTH = 4

GRID_W = 64
CTX_LEN = 256
ROPE_THETA = 10000.0
NORM_EPS = 1e-6
HEAD_DIM = 64

A_HEADS = 8
A_KV_HEADS = 2
A_WINDOW = 128
A_BLOCK = 128
B_HEADS = 8
B_Q_LORA = 384
B_KV_LORA = 256
B_NOPE = 64
B_ROPE = 32
B_V = 64
B_BLOCK = 128
C_HEADS = 16
C_KH = 8
C_KW = 16

A_WIDTH = A_HEADS * HEAD_DIM
A_KV_WIDTH = A_KV_HEADS * HEAD_DIM
B_WIDTH = B_HEADS * B_V
AB_MIX = A_WIDTH + B_WIDTH
AB_SPLITS = (
    A_WIDTH,
    A_WIDTH + A_KV_WIDTH,
    A_WIDTH + 2 * A_KV_WIDTH,
    2 * A_WIDTH + 2 * A_KV_WIDTH,
    2 * A_WIDTH + 2 * A_KV_WIDTH + B_Q_LORA,
    2 * A_WIDTH + 2 * A_KV_WIDTH + B_Q_LORA + B_KV_LORA,
    2 * A_WIDTH + 2 * A_KV_WIDTH + B_Q_LORA + B_KV_LORA + B_ROPE,
)
AB_IN = 2 * A_WIDTH + 2 * A_KV_WIDTH + B_Q_LORA + B_KV_LORA + B_ROPE + B_WIDTH
C_WIDTH = C_HEADS * HEAD_DIM
C_IN = 4 * C_WIDTH
N_EVEN = (DEPTH + 1) // 2
N_ODD = DEPTH // 2

kernel_name = "hybrid_window_mla_natten_prefix_dit"


def rms_norm(x, g):
    xf = x.astype(jnp.float32)
    y = xf * lax.rsqrt(jnp.mean(xf * xf, axis=-1, keepdims=True) + NORM_EPS)
    return (y * g.astype(jnp.float32)).astype(x.dtype)


def axial_rope_angles(n_tokens, rot_dim):
    t = jnp.arange(n_tokens, dtype=jnp.int32)
    row = (t // GRID_W).astype(jnp.float32)
    col = (t % GRID_W).astype(jnp.float32)
    pairs_per_axis = rot_dim // 4
    inv = ROPE_THETA ** (-jnp.arange(pairs_per_axis, dtype=jnp.float32) / pairs_per_axis)
    ang = jnp.concatenate([row[:, None] * inv, col[:, None] * inv], axis=-1)
    return jnp.cos(ang), jnp.sin(ang)


def apply_rope(x, cos, sin):
    d = x.shape[-1]
    xr = x.reshape(x.shape[:-1] + (d // 2, 2)).astype(jnp.float32)
    x1, x2 = xr[..., 0], xr[..., 1]
    out = jnp.stack([x1 * cos - x2 * sin, x1 * sin + x2 * cos], axis=-1)
    return out.reshape(x.shape).astype(x.dtype)


def gqa_sink_dense(q, k, v, sink):
    bn, n, _, d = q.shape
    grp = A_HEADS // A_KV_HEADS
    qg = q.reshape(bn, n, A_KV_HEADS, grp, d)
    s = jnp.einsum('bqhgd,bkhd->bhgqk', qg, k).astype(jnp.float32) * (d ** -0.5)
    s_sink = jnp.broadcast_to(sink.astype(jnp.float32).reshape(A_KV_HEADS, grp)[None, :, :, None, None], s.shape[:-1] + (1,))
    p = jax.nn.softmax(jnp.concatenate([s, s_sink], axis=-1), axis=-1)[..., :-1].astype(v.dtype)
    return jnp.einsum('bhgqk,bkhd->bqhgd', p, v).reshape(bn, n, A_HEADS * d)


def window_gqa_latent(q, k, v, kc, vc, sink):
    bn, s_len, _, d = q.shape
    nb = s_len // A_BLOCK
    grp = A_HEADS // A_KV_HEADS
    scale = d ** -0.5
    pad = [(0, 0), (A_BLOCK, A_BLOCK), (0, 0), (0, 0)]

    def band(t):
        tb = jnp.pad(t, pad).reshape(bn, nb + 2, A_BLOCK, A_KV_HEADS, d)
        w = jnp.concatenate([tb[:, :-2], tb[:, 1:-1], tb[:, 2:]], axis=2)
        return jnp.moveaxis(w, 1, 0)

    kw, vw = band(k), band(v)
    qb = jnp.moveaxis(q.reshape(bn, nb, A_BLOCK, A_KV_HEADS, grp, d), 1, 0)
    qi = jnp.arange(A_BLOCK)[:, None]
    kj = jnp.arange(3 * A_BLOCK)[None, :] - A_BLOCK
    band_ok = jnp.abs(kj - qi) <= A_WINDOW
    sink_g = sink.astype(jnp.float32).reshape(A_KV_HEADS, grp)[None, :, :, None, None]

    def block(args):
        n, qn, kn, vn = args
        kpos = n * A_BLOCK + kj
        mask = band_ok & (kpos >= 0) & (kpos < s_len)
        s_loc = jnp.einsum('bqhgd,bkhd->bhgqk', qn, kn).astype(jnp.float32) * scale
        s_loc = jnp.where(mask, s_loc, -jnp.inf)
        s_ctx = jnp.einsum('bqhgd,blhd->bhgql', qn, kc).astype(jnp.float32) * scale
        s_sink = jnp.broadcast_to(sink_g, s_loc.shape[:-1] + (1,))
        p = jax.nn.softmax(jnp.concatenate([s_loc, s_ctx, s_sink], axis=-1), axis=-1)
        p_loc = p[..., :3 * A_BLOCK].astype(vn.dtype)
        p_ctx = p[..., 3 * A_BLOCK:-1].astype(vc.dtype)
        return (jnp.einsum('bhgqk,bkhd->bqhgd', p_loc, vn)
                + jnp.einsum('bhgql,blhd->bqhgd', p_ctx, vc))

    o = lax.map(block, (jnp.arange(nb), qb, kw, vw))
    return jnp.moveaxis(o, 0, 1).reshape(bn, s_len, A_HEADS * d)


def mla_dense(qn, qr, kn, kr, v):
    scale = (B_NOPE + B_ROPE) ** -0.5
    s = (jnp.einsum('bqhd,bkhd->bhqk', qn, kn) + jnp.einsum('bqhr,bkr->bhqk', qr, kr)).astype(jnp.float32) * scale
    p = jax.nn.softmax(s, axis=-1).astype(v.dtype)
    return jnp.einsum('bhqk,bkhd->bqhd', p, v)


def mla_latent(qn, qr, kn, kr, v, kn_c, kr_c, v_c):
    bn, s_len = qn.shape[0], qn.shape[1]
    nb = s_len // B_BLOCK
    k_n = jnp.concatenate([kn_c, kn], axis=1)
    k_r = jnp.concatenate([kr_c, kr], axis=1)
    v_all = jnp.concatenate([v_c, v], axis=1)
    qn_b = jnp.moveaxis(qn.reshape(bn, nb, B_BLOCK, B_HEADS, B_NOPE), 1, 0)
    qr_b = jnp.moveaxis(qr.reshape(bn, nb, B_BLOCK, B_HEADS, B_ROPE), 1, 0)
    o = lax.map(lambda a: mla_dense(a[0], a[1], k_n, k_r, v_all), (qn_b, qr_b))
    return jnp.moveaxis(o, 0, 1).reshape(bn, s_len, B_HEADS * B_V)


def ab_project(u, in_w, qn_g, w_uq, kvn_g, w_ukv):
    bn, n, _ = u.shape
    p = u @ in_w
    qa, ka, va, za, cq, ckv, kr, zb = jnp.split(p, AB_SPLITS, axis=-1)
    qa = qa.reshape(bn, n, A_HEADS, HEAD_DIM)
    ka = ka.reshape(bn, n, A_KV_HEADS, HEAD_DIM)
    va = va.reshape(bn, n, A_KV_HEADS, HEAD_DIM)
    qb = (rms_norm(cq, qn_g) @ w_uq).reshape(bn, n, B_HEADS, B_NOPE + B_ROPE)
    kvb = (rms_norm(ckv, kvn_g) @ w_ukv).reshape(bn, n, B_HEADS, B_NOPE + B_V)
    return (qa, ka, va, za, qb[..., :B_NOPE], qb[..., B_NOPE:],
            kvb[..., :B_NOPE], kr, kvb[..., B_NOPE:], zb)


def ab_mixer(u, uc, in_w, out_w, sink, qn_g, w_uq, kvn_g, w_ukv, rope_a, rope_b, need_ctx):
    qa, ka, va, za, qbn, qbr, kbn, kbr, vb, zb = ab_project(u, in_w, qn_g, w_uq, kvn_g, w_ukv)
    qa_c, ka_c, va_c, za_c, qbn_c, qbr_c, kbn_c, kbr_c, vb_c, zb_c = ab_project(uc, in_w, qn_g, w_uq, kvn_g, w_ukv)
    cos_a, sin_a = rope_a
    cos_b, sin_b = rope_b
    qa = apply_rope(qa, cos_a[:, None, :], sin_a[:, None, :])
    ka = apply_rope(ka, cos_a[:, None, :], sin_a[:, None, :])
    qbr = apply_rope(qbr, cos_b[:, None, :], sin_b[:, None, :])
    kbr = apply_rope(kbr, cos_b, sin_b)
    oa = window_gqa_latent(qa, ka, va, ka_c, va_c, sink)
    ob = mla_latent(qbn, qbr, kbn, kbr, vb, kbn_c, kbr_c, vb_c)
    y = jnp.concatenate([oa * jax.nn.silu(za), ob * jax.nn.silu(zb)], axis=-1) @ out_w
    if not need_ctx:
        return y, None
    bn, n = uc.shape[0], uc.shape[1]
    oa_c = gqa_sink_dense(qa_c, ka_c, va_c, sink)
    ob_c = mla_dense(qbn_c, qbr_c, kbn_c, kbr_c, vb_c).reshape(bn, n, B_WIDTH)
    yc = jnp.concatenate([oa_c * jax.nn.silu(za_c), ob_c * jax.nn.silu(zb_c)], axis=-1) @ out_w
    return y, yc


def mha_dense(q, k, v):
    bn, n, h, d = q.shape
    s = jnp.einsum('bqhd,bkhd->bhqk', q, k).astype(jnp.float32) * (d ** -0.5)
    p = jax.nn.softmax(s, axis=-1).astype(v.dtype)
    return jnp.einsum('bhqk,bkhd->bqhd', p, v).reshape(bn, n, h * d)


def neighbourhood_latent(q, k, v, kc, vc, rpb):
    bn, s_len, h, d = q.shape
    rows = s_len // GRID_W
    kh = min(C_KH, rows)
    scale = d ** -0.5
    qg = jnp.moveaxis(q.reshape(bn, rows, GRID_W, h, d), 1, 0)
    kg = k.reshape(bn, rows, GRID_W, h, d)
    vg = v.reshape(bn, rows, GRID_W, h, d)
    col = jnp.arange(GRID_W)
    cs = jnp.clip(col - C_KW // 2, 0, GRID_W - C_KW)
    col_ok = (col[None, :] >= cs[:, None]) & (col[None, :] < cs[:, None] + C_KW)
    col_idx = jnp.clip(col[None, :] - col[:, None] + (C_KW - 1), 0, 2 * C_KW - 2)
    rpb_col = rpb[:, :, col_idx]
    mask = jnp.broadcast_to(col_ok[:, None, :], (GRID_W, kh, GRID_W)).reshape(GRID_W, kh * GRID_W)

    def row_block(args):
        r, qr = args
        rs = jnp.clip(r - kh // 2, 0, rows - kh)
        kr = lax.dynamic_slice_in_dim(kg, rs, kh, axis=1).reshape(bn, kh * GRID_W, h, d)
        vr = lax.dynamic_slice_in_dim(vg, rs, kh, axis=1).reshape(bn, kh * GRID_W, h, d)
        row_idx = rs + jnp.arange(kh) - r + (C_KH - 1)
        bias = jnp.take(rpb_col, row_idx, axis=1)
        bias = jnp.transpose(bias, (0, 2, 1, 3)).reshape(h, GRID_W, kh * GRID_W)
        s_loc = jnp.einsum('bqhd,bkhd->bhqk', qr, kr).astype(jnp.float32) * scale + bias.astype(jnp.float32)
        s_loc = jnp.where(mask, s_loc, -jnp.inf)
        s_ctx = jnp.einsum('bqhd,blhd->bhql', qr, kc).astype(jnp.float32) * scale
        p = jax.nn.softmax(jnp.concatenate([s_loc, s_ctx], axis=-1), axis=-1)
        p_loc = p[..., :kh * GRID_W].astype(vr.dtype)
        p_ctx = p[..., kh * GRID_W:].astype(vc.dtype)
        return jnp.einsum('bhqk,bkhd->bqhd', p_loc, vr) + jnp.einsum('bhql,blhd->bqhd', p_ctx, vc)

    o = lax.map(row_block, (jnp.arange(rows), qg))
    return jnp.moveaxis(o, 0, 1).reshape(bn, s_len, h * d)


def c_mixer(u, uc, in_w, out_w, rpb, need_ctx):
    def proj(t):
        bn, n, _ = t.shape
        q, k, v, z = jnp.split(t @ in_w, 4, axis=-1)
        shp = (bn, n, C_HEADS, HEAD_DIM)
        return q.reshape(shp), k.reshape(shp), v.reshape(shp), z
    q, k, v, z = proj(u)
    q_c, k_c, v_c, z_c = proj(uc)
    o = neighbourhood_latent(q, k, v, k_c, v_c, rpb)
    y = (o * jax.nn.silu(z)) @ out_w
    if not need_ctx:
        return y, None
    yc = (mha_dense(q_c, k_c, v_c) * jax.nn.silu(z_c)) @ out_w
    return y, yc


def setup_inputs(seed: int = 0) -> dict:
    key = jax.random.key(seed)
    ks = jax.random.split(key, 20)
    f32 = jnp.float32

    def w(k, shape, fan_in, gain=1.0):
        return jax.random.normal(k, shape, f32) * (gain * fan_in ** -0.5)

    def gain(k, shape):
        return 1.0 + 0.05 * jax.random.normal(k, shape, f32)

    return {
        "x": jax.random.normal(ks[0], (BATCH, SEQ, D_MODEL), f32),
        "c": jax.random.normal(ks[1], (BATCH, D_MODEL), f32),
        "ctx": jax.random.normal(ks[2], (BATCH, CTX_LEN, D_MODEL), f32),
        "c_ctx": jax.random.normal(ks[3], (D_MODEL,), f32),
        "ada_w": w(ks[4], (DEPTH, D_MODEL, 3 * D_MODEL), D_MODEL, 0.5),
        "ada_b": 0.02 * jax.random.normal(ks[5], (DEPTH, 3 * D_MODEL), f32),
        "norm_g": gain(ks[6], (DEPTH, D_MODEL)),
        "ab_in_w": w(ks[7], (N_EVEN, D_MODEL, AB_IN), D_MODEL),
        "ab_out_w": w(ks[8], (N_EVEN, AB_MIX, D_MODEL), AB_MIX),
        "a_sink": jax.random.normal(ks[9], (N_EVEN, A_HEADS), f32),
        "b_q_norm_g": gain(ks[10], (N_EVEN, B_Q_LORA)),
        "b_w_uq": w(ks[11], (N_EVEN, B_Q_LORA, B_HEADS * (B_NOPE + B_ROPE)), B_Q_LORA),
        "b_kv_norm_g": gain(ks[12], (N_EVEN, B_KV_LORA)),
        "b_w_ukv": w(ks[13], (N_EVEN, B_KV_LORA, B_HEADS * (B_NOPE + B_V)), B_KV_LORA),
        "c_in_w": w(ks[14], (N_ODD, D_MODEL, C_IN), D_MODEL),
        "c_out_w": w(ks[15], (N_ODD, C_WIDTH, D_MODEL), C_WIDTH),
        "c_rpb": 0.5 * jax.random.normal(ks[16], (N_ODD, C_HEADS, 2 * C_KH - 1, 2 * C_KW - 1), f32),
        "final_g": gain(ks[17], (D_MODEL,)),
    }


def reference(x, c, ctx, c_ctx, ada_w, ada_b, norm_g, ab_in_w, ab_out_w, a_sink,
              b_q_norm_g, b_w_uq, b_kv_norm_g, b_w_ukv, c_in_w, c_out_w, c_rpb, final_g):
    n_lat = x.shape[1]
    rope_a = axial_rope_angles(n_lat, HEAD_DIM)
    rope_b = axial_rope_angles(n_lat, B_ROPE)
    sc = jax.nn.silu(c)
    sc_ctx = jax.nn.silu(c_ctx)
    h, hc = x, ctx
    for layer in range(DEPTH):
        last = layer == DEPTH - 1
        mod = sc @ ada_w[layer] + ada_b[layer]
        shift, scale, gate = jnp.split(mod[:, None, :], 3, axis=-1)
        mod_c = sc_ctx @ ada_w[layer] + ada_b[layer]
        shift_c, scale_c, gate_c = jnp.split(mod_c, 3, axis=-1)
        u = rms_norm(h, norm_g[layer]) * (1 + scale) + shift
        uc = rms_norm(hc, norm_g[layer]) * (1 + scale_c) + shift_c
        if layer % 2 == 0:
            i = layer // 2
            y, yc = ab_mixer(u, uc, ab_in_w[i], ab_out_w[i], a_sink[i], b_q_norm_g[i], b_w_uq[i],
                             b_kv_norm_g[i], b_w_ukv[i], rope_a, rope_b, not last)
        else:
            i = layer // 2
            y, yc = c_mixer(u, uc, c_in_w[i], c_out_w[i], c_rpb[i], not last)
        h = h + gate * y
        if not last:
            hc = hc + gate_c * yc
    return rms_norm(h, final_g)
```

```python
import functools
import math

import numpy as np
import jax
import jax.numpy as jnp
from jax import lax
from jax.experimental import pallas as pl
from jax.experimental.pallas import tpu as pltpu

D_MODEL = 1024
HEAD_DIM = 64
GRID_W = 64
ROPE_THETA = 10000.0
NORM_EPS = 1e-6
A_HEADS = 8
A_KV_HEADS = 2
A_WINDOW = 128
B_HEADS = 8
B_Q_LORA = 384
B_KV_LORA = 256
B_NOPE = 64
B_ROPE = 32
B_V = 64
C_HEADS = 16
C_KH = 8
C_KW = 16

LANES = 128
TILE = 256
MLA_CHUNK = 512
NEG = -1e30
VMEM_LIMIT = 48 * 1024 * 1024

_QA, _KA, _VA, _ZA, _ZB, _CQ, _CKV, _KR, _AB_COLS = 0, 512, 640, 768, 1280, 1792, 2176, 2432, 2560
_A_SLOTS = (0, 4, 1, 5, 2, 6, 3, 7)


def _cparams(sem):
    return pltpu.CompilerParams(dimension_semantics=sem, vmem_limit_bytes=VMEM_LIMIT)


def _silu(x):
    return x * (1.0 / (1.0 + jnp.exp(-x)))


def _dot(a, b):
    return jnp.dot(a, b, preferred_element_type=jnp.float32)


def _dot_t(a, b):
    return lax.dot_general(a, b, (((1,), (1,)), ((), ())), preferred_element_type=jnp.float32)


def _mods_body(c_ref, w_ref, b_ref, o_ref):
    sc = _silu(c_ref[...])
    o_ref[...] = jnp.dot(sc, w_ref[...], preferred_element_type=jnp.float32,
                         precision=lax.Precision.HIGHEST) + b_ref[...]


def _mods(cc, ada_w, ada_b):
    depth, d, d3 = ada_w.shape
    rows = cc.shape[0]
    tn = 512
    return pl.pallas_call(
        _mods_body,
        grid=(depth, d3 // tn),
        in_specs=[
            pl.BlockSpec((rows, d), lambda l, j: (0, 0)),
            pl.BlockSpec((None, d, tn), lambda l, j: (l, 0, j)),
            pl.BlockSpec((None, 1, tn), lambda l, j: (l, 0, j)),
        ],
        out_specs=pl.BlockSpec((None, rows, tn), lambda l, j: (l, 0, j)),
        out_shape=jax.ShapeDtypeStruct((depth, rows, d3), jnp.float32),
        compiler_params=_cparams(("arbitrary", "arbitrary")),
        name="mods",
    )(cc, ada_w, ada_b.reshape(depth, 1, d3))


def _mod_spec(n_batch):
    return pl.BlockSpec((None, 1, 3 * D_MODEL), lambda b, i: (jnp.where(i == 0, n_batch, b), 0, 0))


def _norm_mod(h, g, mod):
    y = h * lax.rsqrt(jnp.mean(h * h, axis=-1, keepdims=True) + NORM_EPS) * g
    return y * (1.0 + mod[:, D_MODEL:2 * D_MODEL]) + mod[:, :D_MODEL]


def _rms(x, g):
    return x * lax.rsqrt(jnp.mean(x * x, axis=-1, keepdims=True) + NORM_EPS) * g


def _rope(x, c, s1, s2, half):
    return x * c + pltpu.roll(x, LANES - half, 1) * s1 + pltpu.roll(x, half, 1) * s2


def _proj_even_body(h_ref, mod_ref, g_ref, w_ref, qg_ref, wuq_ref, kvg_ref, wukv_ref,
                    ca_ref, sa1_ref, sa2_ref, cb_ref, sb1_ref, sb2_ref,
                    qa_ref, ka_ref, va_ref, gate_ref, qb_ref, kb_ref, vb_ref):
    u = _norm_mod(h_ref[...], g_ref[...], mod_ref[...]).astype(jnp.bfloat16)
    ca, sa1, sa2 = ca_ref[...], sa1_ref[...], sa2_ref[...]
    cb, sb1, sb2 = cb_ref[...], sb1_ref[...], sb2_ref[...]

    qk = _dot(u, w_ref[:, _QA:_VA])
    a_scale = HEAD_DIM ** -0.5
    for j in range(4):
        blk = _rope(qk[:, j * LANES:(j + 1) * LANES], ca, sa1, sa2, HEAD_DIM // 2)
        qa_ref[:, j * LANES:(j + 1) * LANES] = (blk * a_scale).astype(jnp.bfloat16)
    ka_ref[...] = _rope(qk[:, _KA:_VA], ca, sa1, sa2, HEAD_DIM // 2).astype(jnp.bfloat16)
    va_ref[...] = _dot(u, w_ref[:, _VA:_ZA]).astype(jnp.bfloat16)
    gate_ref[...] = _silu(_dot(u, w_ref[:, _ZA:_CQ])).astype(jnp.bfloat16)

    lat = _dot(u, w_ref[:, _CQ:_AB_COLS])
    cq = _rms(lat[:, :B_Q_LORA], qg_ref[...]).astype(jnp.bfloat16)
    ckv = _rms(lat[:, B_Q_LORA:B_Q_LORA + B_KV_LORA], kvg_ref[...]).astype(jnp.bfloat16)
    kr = _rope(lat[:, B_Q_LORA + B_KV_LORA:], cb, sb1, sb2, B_ROPE // 2)

    qb = _dot(cq, wuq_ref[...])
    b_scale = (B_NOPE + B_ROPE) ** -0.5
    kvb = _dot(ckv, wukv_ref[...])
    for hd in range(B_HEADS):
        sl = slice(hd * LANES, (hd + 1) * LANES)
        qb_ref[:, sl] = (_rope(qb[:, sl], cb, sb1, sb2, B_ROPE // 2) * b_scale).astype(jnp.bfloat16)
        kb_ref[:, sl] = (kvb[:, sl] + kr).astype(jnp.bfloat16)
    vb_ref[...] = kvb[:, B_HEADS * LANES:].astype(jnp.bfloat16)


def _proj_even(h, mods_l, g, w_in, qg, wuq, kvg, wukv, tabs_a, tabs_b):
    n_batch, t_len, d = h.shape
    nt = t_len // TILE
    row = lambda w: pl.BlockSpec((None, TILE, w), lambda b, i: (b, i, 0))
    full = lambda a: pl.BlockSpec(a.shape, lambda b, i: (0,) * a.ndim)
    tab = pl.BlockSpec((TILE, LANES), lambda b, i: (i, 0))
    bf = jnp.bfloat16
    widths = (512, 128, 128, 1024, 1024, 1024, 1024)
    return pl.pallas_call(
        _proj_even_body,
        grid=(n_batch, nt),
        in_specs=[row(d), _mod_spec(n_batch), full(g), full(w_in), full(qg), full(wuq), full(kvg), full(wukv)]
                 + [tab] * 6,
        out_specs=[row(w) for w in widths],
        out_shape=[jax.ShapeDtypeStruct((n_batch, t_len, w), bf) for w in widths],
        compiler_params=_cparams(("parallel", "arbitrary")),
        name="proj_even",
    )(h, mods_l, g, w_in, qg, wuq, kvg, wukv, *tabs_a, *tabs_b)


def _proj_odd_body(h_ref, mod_ref, g_ref, w_ref, q_ref, k_ref, v_ref, gate_ref):
    u = _norm_mod(h_ref[...], g_ref[...], mod_ref[...]).astype(jnp.bfloat16)
    d = D_MODEL
    q_ref[...] = (_dot(u, w_ref[:, :d]) * (HEAD_DIM ** -0.5)).astype(jnp.bfloat16)
    k_ref[...] = _dot(u, w_ref[:, d:2 * d]).astype(jnp.bfloat16)
    v_ref[...] = _dot(u, w_ref[:, 2 * d:3 * d]).astype(jnp.bfloat16)
    gate_ref[...] = _silu(_dot(u, w_ref[:, 3 * d:])).astype(jnp.bfloat16)


def _proj_odd(h, mods_l, g, w_in):
    n_batch, t_len, d = h.shape
    nt = t_len // TILE
    row = pl.BlockSpec((None, TILE, d), lambda b, i: (b, i, 0))
    full = lambda a: pl.BlockSpec(a.shape, lambda b, i: (0,) * a.ndim)
    return pl.pallas_call(
        _proj_odd_body,
        grid=(n_batch, nt),
        in_specs=[row, _mod_spec(n_batch), full(g), full(w_in)],
        out_specs=[row] * 4,
        out_shape=[jax.ShapeDtypeStruct((n_batch, t_len, d), jnp.bfloat16)] * 4,
        compiler_params=_cparams(("parallel", "arbitrary")),
        name="proj_odd",
    )(h, mods_l, g, w_in)


def _lane_half_masks(rows, dtype):
    lane = lax.broadcasted_iota(jnp.int32, (rows, LANES), 1)
    lo = lane < HEAD_DIM
    return lo, jnp.logical_not(lo)


def _win_body(sink_ref, q_ref, kc_ref, kp_ref, km_ref, kn_ref, vc_ref, vp_ref, vm_ref, vn_ref, o_ref,
              *, ctx_len, seq_len):
    i = pl.program_id(1)
    half = TILE // 2
    qpos = i * TILE - ctx_len + lax.broadcasted_iota(jnp.int32, (TILE, 2 * TILE), 0)
    kcol = lax.broadcasted_iota(jnp.int32, (TILE, 2 * TILE), 1)
    kpos = i * TILE - ctx_len - half + kcol
    ok = (jnp.abs(kpos - qpos) <= A_WINDOW) & (kpos >= 0) & (kpos < seq_len) & (i > 0)

    k_loc = jnp.concatenate([kp_ref[...], km_ref[...], kn_ref[...]], axis=0)
    v_loc = jnp.concatenate([vp_ref[...], vm_ref[...], vn_ref[...]], axis=0)
    k_ctx, v_ctx = kc_ref[...], vc_ref[...]
    lo_q, hi_q = _lane_half_masks(TILE, None)
    lo_l, hi_l = _lane_half_masks(2 * TILE, None)
    lo_c, hi_c = _lane_half_masks(ctx_len, None)
    zero = jnp.zeros((), jnp.bfloat16)
    v_loc_h = (jnp.where(lo_l, v_loc, zero), jnp.where(hi_l, v_loc, zero))
    v_ctx_h = (jnp.where(lo_c, v_ctx, zero), jnp.where(hi_c, v_ctx, zero))

    for p in range(A_HEADS // 2):
        q2 = q_ref[:, p * LANES:(p + 1) * LANES]
        out = jnp.zeros((TILE, LANES), jnp.float32)
        for hh, qmask in enumerate((lo_q, hi_q)):
            qm = jnp.where(qmask, q2, zero)
            sink = sink_ref[2 * p + hh]
            s_loc = jnp.where(ok, _dot_t(qm, k_loc), NEG)
            s_ctx = _dot_t(qm, k_ctx)
            m = jnp.maximum(jnp.maximum(jnp.max(s_loc, axis=1, keepdims=True),
                                        jnp.max(s_ctx, axis=1, keepdims=True)), sink)
            p_loc = jnp.exp(s_loc - m)
            p_ctx = jnp.exp(s_ctx - m)
            denom = (jnp.sum(p_loc, axis=1, keepdims=True) + jnp.sum(p_ctx, axis=1, keepdims=True)
                     + jnp.exp(sink - m))
            pv = _dot(p_loc.astype(jnp.bfloat16), v_loc_h[hh]) + _dot(p_ctx.astype(jnp.bfloat16), v_ctx_h[hh])
            out = out + pv * (1.0 / denom)
        o_ref[:, p * LANES:(p + 1) * LANES] = out.astype(jnp.bfloat16)


def _win_attn(qa, ka, va, sink, ctx_len):
    n_batch, t_len, _ = qa.shape
    nt = t_len // TILE
    assert ctx_len == TILE
    n128 = t_len // 128
    kv_ctx = pl.BlockSpec((None, ctx_len, LANES), lambda b, i: (b, 0, 0))
    kv_prev = pl.BlockSpec((None, 128, LANES), lambda b, i: (b, jnp.maximum(2 * i - 1, 0), 0))
    kv_mid = pl.BlockSpec((None, TILE, LANES), lambda b, i: (b, i, 0))
    kv_next = pl.BlockSpec((None, 128, LANES), lambda b, i: (b, jnp.minimum(2 * i + 2, n128 - 1), 0))
    kv_specs = [kv_ctx, kv_prev, kv_mid, kv_next]
    body = functools.partial(_win_body, ctx_len=ctx_len, seq_len=t_len - ctx_len)
    return pl.pallas_call(
        body,
        grid=(n_batch, nt),
        in_specs=[pl.BlockSpec(memory_space=pltpu.SMEM),
                  pl.BlockSpec((None, TILE, 512), lambda b, i: (b, i, 0))] + kv_specs + kv_specs,
        out_specs=pl.BlockSpec((None, TILE, 512), lambda b, i: (b, i, 0)),
        out_shape=jax.ShapeDtypeStruct((n_batch, t_len, 512), jnp.bfloat16),
        compiler_params=_cparams(("parallel", "arbitrary")),
        name="win_attn",
    )(sink, qa, ka, ka, ka, ka, va, va, va, va)


def _mla_body(q_ref, k_ref, v_ref, o_ref, *, ctx_len, n_chunks):
    i = pl.program_id(2)
    q = (q_ref[:, :LANES], q_ref[:, LANES:])

    def step(carry, start, size):
        new = []
        acc = carry[-1]
        for hh in range(2):
            m_prev, l_prev = carry[2 * hh], carry[2 * hh + 1]
            k = k_ref[pl.ds(start, size), hh * LANES:(hh + 1) * LANES]
            v = v_ref[pl.ds(start, size), hh * LANES:(hh + 1) * LANES]
            s = _dot_t(q[hh], k)
            m_new = jnp.maximum(m_prev, jnp.max(s, axis=1, keepdims=True))
            alpha = jnp.exp(m_prev - m_new)
            p = jnp.exp(s - m_new)
            l_new = alpha * l_prev + jnp.sum(p, axis=1, keepdims=True)
            pv = _dot(p.astype(jnp.bfloat16), v)
            half = lax.broadcasted_iota(jnp.int32, (TILE, LANES), 1) // HEAD_DIM == hh
            acc = jnp.where(half, acc * alpha, acc) + pv
            new += [m_new, l_new]
        return tuple(new) + (acc,)

    neg = jnp.full((TILE, 1), NEG, jnp.float32)
    zero = jnp.zeros((TILE, 1), jnp.float32)
    carry = (neg, zero, neg, zero, jnp.zeros((TILE, LANES), jnp.float32))
    carry = step(carry, 0, ctx_len)

    def body(j, c):
        return step(c, pl.multiple_of(ctx_len + j * MLA_CHUNK, MLA_CHUNK // 2), MLA_CHUNK)

    carry = lax.fori_loop(0, jnp.where(i == 0, 0, n_chunks), body, carry)
    lane_lo = lax.broadcasted_iota(jnp.int32, (TILE, LANES), 1) < HEAD_DIM
    inv = jnp.where(lane_lo, 1.0 / carry[1], 1.0 / carry[3])
    o_ref[...] = (carry[4] * inv).astype(jnp.bfloat16)


def _mla_attn(qb, kb, vb, ctx_len):
    n_batch, t_len, _ = qb.shape
    nt = t_len // TILE
    seq = t_len - ctx_len
    assert ctx_len == TILE and seq % MLA_CHUNK == 0
    body = functools.partial(_mla_body, ctx_len=ctx_len, n_chunks=seq // MLA_CHUNK)
    kv = pl.BlockSpec((None, t_len, 2 * LANES), lambda b, p, i: (b, 0, p))
    return pl.pallas_call(
        body,
        grid=(n_batch, B_HEADS // 2, nt),
        in_specs=[pl.BlockSpec((None, TILE, 2 * LANES), lambda b, p, i: (b, i, p)), kv, kv],
        out_specs=pl.BlockSpec((None, TILE, LANES), lambda b, p, i: (b, i, p)),
        out_shape=jax.ShapeDtypeStruct((n_batch, t_len, B_HEADS * B_V), jnp.bfloat16),
        compiler_params=_cparams(("parallel", "parallel", "arbitrary")),
        name="mla_attn",
    )(qb, kb, vb)


def _nbr_body(q_ref, kc_ref, k0_ref, k1_ref, k2_ref, vc_ref, v0_ref, v1_ref, v2_ref, bias_ref, o_ref):
    k_loc = jnp.concatenate([k0_ref[...], k1_ref[...], k2_ref[...]], axis=0)
    v_loc = jnp.concatenate([v0_ref[...], v1_ref[...], v2_ref[...]], axis=0)
    k_ctx, v_ctx = kc_ref[...], vc_ref[...]
    q2 = q_ref[...]
    zero = jnp.zeros((), jnp.bfloat16)
    lo_q, hi_q = _lane_half_masks(TILE, None)
    lo_l, hi_l = _lane_half_masks(3 * TILE, None)
    lo_c, hi_c = _lane_half_masks(k_ctx.shape[0], None)
    v_loc_h = (jnp.where(lo_l, v_loc, zero), jnp.where(hi_l, v_loc, zero))
    v_ctx_h = (jnp.where(lo_c, v_ctx, zero), jnp.where(hi_c, v_ctx, zero))
    out = jnp.zeros((TILE, LANES), jnp.float32)
    for hh, qmask in enumerate((lo_q, hi_q)):
        qm = jnp.where(qmask, q2, zero)
        s_loc = _dot_t(qm, k_loc) + bias_ref[hh]
        s_ctx = _dot_t(qm, k_ctx)
        m = jnp.maximum(jnp.max(s_loc, axis=1, keepdims=True), jnp.max(s_ctx, axis=1, keepdims=True))
        p_loc = jnp.exp(s_loc - m)
        p_ctx = jnp.exp(s_ctx - m)
        denom = jnp.sum(p_loc, axis=1, keepdims=True) + jnp.sum(p_ctx, axis=1, keepdims=True)
        pv = _dot(p_loc.astype(jnp.bfloat16), v_loc_h[hh]) + _dot(p_ctx.astype(jnp.bfloat16), v_ctx_h[hh])
        out = out + pv * (1.0 / denom)
    o_ref[...] = out.astype(jnp.bfloat16)


def _nbr_attn(q, k, v, bias, ctx_len):
    n_batch, t_len, d = q.shape
    nt = t_len // TILE
    assert ctx_len == TILE and nt >= 4
    blk = lambda f: pl.BlockSpec((None, TILE, LANES), lambda b, p, i: (b, f(i), p))
    kv_specs = [blk(lambda i: 0), blk(lambda i: jnp.clip(i - 1, 1, nt - 1)), blk(lambda i: i),
                blk(lambda i: jnp.clip(i + 1, 1, nt - 1))]

    def case(i):
        return jnp.where(i == 0, 3, jnp.where(i == 1, 0, jnp.where(i == nt - 1, 2, 1)))

    bias_spec = pl.BlockSpec((None, 2, TILE, 3 * TILE), lambda b, p, i: (case(i), p, 0, 0))
    return pl.pallas_call(
        _nbr_body,
        grid=(n_batch, C_HEADS // 2, nt),
        in_specs=[blk(lambda i: i)] + kv_specs + kv_specs + [bias_spec],
        out_specs=blk(lambda i: i),
        out_shape=jax.ShapeDtypeStruct((n_batch, t_len, d), jnp.bfloat16),
        compiler_params=_cparams(("parallel", "parallel", "arbitrary")),
        name="nbr_attn",
    )(q, k, k, k, k, v, v, v, v, bias)


def _nbr_bias(rpb, rows):
    rpt = TILE // GRID_W
    jr = np.arange(rpt)[:, None, None, None]
    qc = np.arange(GRID_W)[None, :, None, None]
    t = np.arange(3 * rpt)[None, None, :, None]
    kc = np.arange(GRID_W)[None, None, None, :]
    row_idx = np.clip(t - rpt - jr + (C_KH - 1), 0, 2 * C_KH - 2)
    col_idx = np.clip(kc - qc + (C_KW - 1), 0, 2 * C_KW - 2)
    cs = np.clip(qc - C_KW // 2, 0, GRID_W - C_KW)
    col_ok = (kc >= cs) & (kc < cs + C_KW)
    n_tiles = rows // rpt
    masks = []
    for g in (0, 1, n_tiles - 1):
        r = g * rpt + jr
        rs = np.clip(r - C_KH // 2, 0, rows - C_KH)
        k_abs = (g - 1) * rpt + t
        masks.append(np.broadcast_to((k_abs >= rs) & (k_abs < rs + C_KH) & col_ok, (rpt, GRID_W, 3 * rpt, GRID_W)))
    masks.append(np.zeros((rpt, GRID_W, 3 * rpt, GRID_W), bool))
    mask = jnp.asarray(np.stack(masks).reshape(4, 1, TILE, 3 * TILE))
    ri = np.broadcast_to(row_idx, (rpt, GRID_W, 3 * rpt, GRID_W)).reshape(TILE, 3 * TILE)
    ci = np.broadcast_to(col_idx, (rpt, GRID_W, 3 * rpt, GRID_W)).reshape(TILE, 3 * TILE)
    gathered = rpb[:, ri, ci]
    return jnp.where(mask, gathered[None], NEG)


def _out_body(h_ref, mod_ref, o1_ref, o2_ref, gate_ref, w_ref, *rest, final):
    half = D_MODEL // 2
    g = gate_ref[...].astype(jnp.float32)
    og1 = (o1_ref[...].astype(jnp.float32) * g[:, :half]).astype(jnp.bfloat16)
    og2 = (o2_ref[...].astype(jnp.float32) * g[:, half:]).astype(jnp.bfloat16)
    y = _dot(og1, w_ref[:half, :]) + _dot(og2, w_ref[half:, :])
    h_new = h_ref[...] + mod_ref[:, 2 * D_MODEL:] * y
    if final:
        fg_ref, out_ref = rest
        out_ref[...] = _rms(h_new, fg_ref[...])
    else:
        rest[0][...] = h_new


def _out_proj(h, mods_l, o1, o2, gate, w, o2_block, final_g=None):
    n_batch, t_len, d = h.shape
    nt = t_len // TILE
    final = final_g is not None
    off = 1 if final else 0
    row = lambda wd, blk=0: pl.BlockSpec((None, TILE, wd), lambda b, i: (b, i + off, blk))
    mod = pl.BlockSpec((None, 1, 3 * D_MODEL), lambda b, i: (jnp.where(i + off == 0, n_batch, b), 0, 0))
    full = lambda a: pl.BlockSpec(a.shape, lambda b, i: (0,) * a.ndim)
    in_specs = [row(d), mod, row(d // 2), row(d // 2, o2_block), row(d), full(w)]
    args = [h, mods_l, o1, o2, gate, w]
    if final:
        in_specs.append(full(final_g))
        args.append(final_g)
        out_spec = pl.BlockSpec((None, TILE, d), lambda b, i: (b, i, 0))
        out_shape = jax.ShapeDtypeStruct((n_batch, t_len - TILE, d), jnp.float32)
        aliases = {}
    else:
        out_spec = row(d)
        out_shape = jax.ShapeDtypeStruct(h.shape, jnp.float32)
        aliases = {0: 0}
    return pl.pallas_call(
        functools.partial(_out_body, final=final),
        grid=(n_batch, nt - off),
        in_specs=in_specs,
        out_specs=out_spec,
        out_shape=out_shape,
        input_output_aliases=aliases,
        compiler_params=_cparams(("parallel", "arbitrary")),
        name="out_proj_final" if final else "out_proj",
    )(*args)


def _deinterleave(n):
    return np.concatenate([np.arange(0, n, 2), np.arange(1, n, 2)])


def _take_cols(w, cols):
    cols = np.asarray(cols)
    picked = jnp.take(w, jnp.asarray(np.maximum(cols, 0)), axis=1)
    return jnp.where(jnp.asarray(cols >= 0)[None, :], picked, 0.0)


def _even_layout():
    perm = _deinterleave(HEAD_DIM)
    qa0, ka0 = 0, A_HEADS * HEAD_DIM
    va0 = ka0 + A_KV_HEADS * HEAD_DIM
    za0 = va0 + A_KV_HEADS * HEAD_DIM
    cq0 = za0 + A_HEADS * HEAD_DIM
    ckv0 = cq0 + B_Q_LORA
    kr0 = ckv0 + B_KV_LORA
    zb0 = kr0 + B_ROPE
    cols = []
    cols += [qa0 + hd * HEAD_DIM + perm for hd in _A_SLOTS]
    cols += [ka0 + kv * HEAD_DIM + perm for kv in range(A_KV_HEADS)]
    cols += [va0 + np.arange(A_KV_HEADS * HEAD_DIM)]
    cols += [za0 + hd * HEAD_DIM + np.arange(HEAD_DIM) for hd in _A_SLOTS]
    cols += [zb0 + np.arange(B_HEADS * B_V)]
    cols += [cq0 + np.arange(B_Q_LORA), ckv0 + np.arange(B_KV_LORA)]
    cols += [np.full(B_NOPE, -1), kr0 + _deinterleave(B_ROPE), np.full(LANES - B_NOPE - B_ROPE, -1)]
    in_cols = np.concatenate(cols)
    assert in_cols.shape[0] == _AB_COLS

    uq = []
    for hd in range(B_HEADS):
        base = hd * (B_NOPE + B_ROPE)
        uq += [base + np.arange(B_NOPE), base + B_NOPE + _deinterleave(B_ROPE),
               np.full(LANES - B_NOPE - B_ROPE, -1)]
    ukv_k, ukv_v = [], []
    pad = np.full(LANES - B_NOPE, -1)
    for hd in range(B_HEADS):
        base = hd * (B_NOPE + B_V)
        ukv_k += [base + np.arange(B_NOPE), pad]
        vcols = base + B_NOPE + np.arange(B_V)
        ukv_v += [vcols, pad] if hd % 2 == 0 else [pad, vcols]
    out_rows = np.concatenate([hd * HEAD_DIM + np.arange(HEAD_DIM) for hd in _A_SLOTS]
                              + [A_HEADS * HEAD_DIM + np.arange(B_HEADS * B_V)])
    return in_cols, np.concatenate(uq), np.concatenate(ukv_k + ukv_v), out_rows


def _rope_tables(seq_len, ctx_len):
    t = jnp.arange(seq_len, dtype=jnp.int32)
    row = (t // GRID_W).astype(jnp.float32)
    col = (t % GRID_W).astype(jnp.float32)

    def angles(rot_dim):
        ppa = rot_dim // 4
        inv = ROPE_THETA ** (-jnp.arange(ppa, dtype=jnp.float32) / ppa)
        ang = jnp.concatenate([row[:, None] * inv, col[:, None] * inv], axis=-1)
        return jnp.cos(ang), jnp.sin(ang)

    def with_ctx(c, s1, s2):
        pad = lambda a, v: jnp.concatenate([jnp.full((ctx_len, LANES), v, jnp.float32), a], axis=0)
        return pad(c, 1.0), pad(s1, 0.0), pad(s2, 0.0)

    cos_a, sin_a = angles(HEAD_DIM)
    z32 = jnp.zeros_like(sin_a)
    tabs_a = with_ctx(jnp.tile(cos_a, (1, 4)), jnp.tile(jnp.concatenate([-sin_a, z32], 1), (1, 2)),
                      jnp.tile(jnp.concatenate([z32, sin_a], 1), (1, 2)))
    cos_b, sin_b = angles(B_ROPE)
    one64 = jnp.ones((seq_len, B_NOPE), jnp.float32)
    z64 = jnp.zeros((seq_len, B_NOPE), jnp.float32)
    z16 = jnp.zeros_like(sin_b)
    one32 = jnp.ones((seq_len, LANES - B_NOPE - B_ROPE), jnp.float32)
    z32b = jnp.zeros((seq_len, LANES - B_NOPE - B_ROPE), jnp.float32)
    tabs_b = with_ctx(jnp.concatenate([one64, cos_b, cos_b, one32], 1),
                      jnp.concatenate([z64, -sin_b, z16, z32b], 1),
                      jnp.concatenate([z64, z16, sin_b, z32b], 1))
    return tabs_a, tabs_b


def kernel(x, c, ctx, c_ctx, ada_w, ada_b, norm_g, ab_in_w, ab_out_w, a_sink, b_q_norm_g, b_w_uq,
           b_kv_norm_g, b_w_ukv, c_in_w, c_out_w, c_rpb, final_g):
    n_batch, seq_len, d = x.shape
    ctx_len = ctx.shape[1]
    depth = ada_w.shape[0]
    assert d == D_MODEL and ctx_len == TILE and seq_len % MLA_CHUNK == 0 and seq_len % GRID_W == 0
    assert depth % 2 == 0, "the final norm is fused into the last (odd) layer's output projection"
    bf = jnp.bfloat16

    mod_rows = -(-(n_batch + 1) // 8) * 8
    cc = jnp.concatenate([c, c_ctx[None, :], jnp.zeros((mod_rows - n_batch - 1, d), jnp.float32)], axis=0)
    mods = _mods(cc, ada_w, ada_b).reshape(depth, mod_rows, 1, 3 * d)

    in_cols, uq_cols, ukv_cols, out_rows = _even_layout()
    tabs_a, tabs_b = _rope_tables(seq_len, ctx_len)
    h = jnp.concatenate([ctx, x], axis=1)
    row2 = lambda v: v.reshape(1, -1)

    out = None
    for layer in range(depth):
        i = layer // 2
        last = layer == depth - 1
        g = row2(norm_g[layer])
        if layer % 2 == 0:
            w_in = _take_cols(ab_in_w[i], in_cols).astype(bf)
            wuq = _take_cols(b_w_uq[i], uq_cols).astype(bf)
            wukv = _take_cols(b_w_ukv[i], ukv_cols).astype(bf)
            w_out = jnp.take(ab_out_w[i], jnp.asarray(out_rows), axis=0).astype(bf)
            sink = jnp.take(a_sink[i], jnp.asarray(_A_SLOTS))
            qa, ka, va, gate, qb, kb, vb = _proj_even(h, mods[layer], g, w_in, row2(b_q_norm_g[i]), wuq,
                                                      row2(b_kv_norm_g[i]), wukv, tabs_a, tabs_b)
            o1 = _win_attn(qa, ka, va, sink, ctx_len)
            o2 = _mla_attn(qb, kb, vb, ctx_len)
            o2_block = 0
        else:
            q, k, v, gate = _proj_odd(h, mods[layer], g, c_in_w[i].astype(bf))
            bias = _nbr_bias(c_rpb[i], seq_len // GRID_W)
            o1 = o2 = _nbr_attn(q, k, v, bias, ctx_len)
            o2_block = 1
            w_out = c_out_w[i].astype(bf)
        if last:
            out = _out_proj(h, mods[layer], o1, o2, gate, w_out, o2_block, final_g=row2(final_g))
        else:
            h = _out_proj(h, mods[layer], o1, o2, gate, w_out, o2_block)
    return out
```

```python
import functools
import math

import numpy as np
import jax
import jax.numpy as jnp
from jax import lax
from jax.experimental import pallas as pl
from jax.experimental.pallas import tpu as pltpu

D_MODEL = 1024
HEAD_DIM = 64
GRID_W = 64
ROPE_THETA = 10000.0
NORM_EPS = 1e-6
A_HEADS = 8
A_KV_HEADS = 2
A_WINDOW = 128
B_HEADS = 8
B_Q_LORA = 384
B_KV_LORA = 256
B_NOPE = 64
B_ROPE = 32
B_V = 64
C_HEADS = 16
C_KH = 8
C_KW = 16

LANES = 128
TILE = 256
MLA_CHUNK = 1024
LOG2E = math.log2(math.e)
NEG = -1e30
VMEM_LIMIT = 48 * 1024 * 1024

_QA, _KA, _VA, _ZA, _ZB, _CQ, _CKV, _KR, _AB_COLS = 0, 512, 640, 768, 1280, 1792, 2176, 2432, 2560
_A_SLOTS = (0, 4, 1, 5, 2, 6, 3, 7)


def _cparams(sem):
    return pltpu.CompilerParams(dimension_semantics=sem, vmem_limit_bytes=VMEM_LIMIT)


def _silu(x):
    return x * (1.0 / (1.0 + jnp.exp(-x)))


def _dot(a, b):
    return jnp.dot(a, b, preferred_element_type=jnp.float32)


def _dot_t(a, b):
    return lax.dot_general(a, b, (((1,), (1,)), ((), ())), preferred_element_type=jnp.float32)


def _mods_body(c_ref, w_ref, b_ref, o_ref):
    sc = _silu(c_ref[...])
    o_ref[...] = jnp.dot(sc, w_ref[...], preferred_element_type=jnp.float32,
                         precision=lax.Precision.HIGHEST) + b_ref[...]


def _mods(cc, ada_w, ada_b):
    depth, d, d3 = ada_w.shape
    rows = cc.shape[0]
    tn = 512
    return pl.pallas_call(
        _mods_body,
        grid=(depth, d3 // tn),
        in_specs=[
            pl.BlockSpec((rows, d), lambda l, j: (0, 0)),
            pl.BlockSpec((None, d, tn), lambda l, j: (l, 0, j)),
            pl.BlockSpec((None, 1, tn), lambda l, j: (l, 0, j)),
        ],
        out_specs=pl.BlockSpec((None, rows, tn), lambda l, j: (l, 0, j)),
        out_shape=jax.ShapeDtypeStruct((depth, rows, d3), jnp.float32),
        compiler_params=_cparams(("arbitrary", "arbitrary")),
        name="mods",
    )(cc, ada_w, ada_b.reshape(depth, 1, d3))


def _mod_spec(n_batch):
    return pl.BlockSpec((None, 1, 3 * D_MODEL), lambda b, i: (jnp.where(i == 0, n_batch, b), 0, 0))


def _norm_mod(h, g, mod):
    y = h * lax.rsqrt(jnp.mean(h * h, axis=-1, keepdims=True) + NORM_EPS) * g
    return y * (1.0 + mod[:, D_MODEL:2 * D_MODEL]) + mod[:, :D_MODEL]


def _rms(x, g):
    return x * lax.rsqrt(jnp.mean(x * x, axis=-1, keepdims=True) + NORM_EPS) * g


def _rope(x, c, s1, s2, half):
    return x * c + pltpu.roll(x, LANES - half, 1) * s1 + pltpu.roll(x, half, 1) * s2


def _proj_even_body(h_ref, mod_ref, g_ref, w_ref, qg_ref, wuq_ref, kvg_ref, wukv_ref, wvt_ref,
                    ca_ref, sa1_ref, sa2_ref, cb_ref, sb1_ref, sb2_ref,
                    qa_ref, ka_ref, va_ref, gate_ref, qb_ref, kb_ref, vbt_ref):
    u = _norm_mod(h_ref[...], g_ref[...], mod_ref[...]).astype(jnp.bfloat16)
    ca, sa1, sa2 = ca_ref[...], sa1_ref[...], sa2_ref[...]
    cb, sb1, sb2 = cb_ref[...], sb1_ref[...], sb2_ref[...]

    qk = _dot(u, w_ref[:, _QA:_VA])
    a_scale = HEAD_DIM ** -0.5
    for j in range(4):
        blk = _rope(qk[:, j * LANES:(j + 1) * LANES], ca, sa1, sa2, HEAD_DIM // 2)
        qa_ref[:, j * LANES:(j + 1) * LANES] = (blk * a_scale).astype(jnp.bfloat16)
    ka_ref[...] = _rope(qk[:, _KA:_VA], ca, sa1, sa2, HEAD_DIM // 2).astype(jnp.bfloat16)
    va_ref[...] = _dot(u, w_ref[:, _VA:_ZA]).astype(jnp.bfloat16)
    gate_ref[...] = _silu(_dot(u, w_ref[:, _ZA:_CQ])).astype(jnp.bfloat16)

    lat = _dot(u, w_ref[:, _CQ:_AB_COLS])
    cq = _rms(lat[:, :B_Q_LORA], qg_ref[...]).astype(jnp.bfloat16)
    ckv = _rms(lat[:, B_Q_LORA:B_Q_LORA + B_KV_LORA], kvg_ref[...]).astype(jnp.bfloat16)
    kr = _rope(lat[:, B_Q_LORA + B_KV_LORA:], cb, sb1, sb2, B_ROPE // 2)

    qb = _dot(cq, wuq_ref[...])
    b_scale = (B_NOPE + B_ROPE) ** -0.5 * LOG2E
    kb = _dot(ckv, wukv_ref[...])
    for hd in range(B_HEADS):
        sl = slice(hd * LANES, (hd + 1) * LANES)
        qb_ref[:, sl] = (_rope(qb[:, sl], cb, sb1, sb2, B_ROPE // 2) * b_scale).astype(jnp.bfloat16)
        kb_ref[:, sl] = (kb[:, sl] + kr).astype(jnp.bfloat16)
    vbt_ref[...] = _dot_t(wvt_ref[...], ckv).astype(jnp.bfloat16)


def _proj_even(h, mods_l, g, w_in, qg, wuq, kvg, wukv, wvt, tabs_a, tabs_b):
    n_batch, t_len, d = h.shape
    nt = t_len // TILE
    row = lambda w: pl.BlockSpec((None, TILE, w), lambda b, i: (b, i, 0))
    full = lambda a: pl.BlockSpec(a.shape, lambda b, i: (0,) * a.ndim)
    tab = pl.BlockSpec((TILE, LANES), lambda b, i: (i, 0))
    bf = jnp.bfloat16
    widths = (512, 128, 128, 1024, 1024, 1024)
    n_vt = B_HEADS * B_V
    return pl.pallas_call(
        _proj_even_body,
        grid=(n_batch, nt),
        in_specs=[row(d), _mod_spec(n_batch), full(g), full(w_in), full(qg), full(wuq), full(kvg), full(wukv),
                  full(wvt)] + [tab] * 6,
        out_specs=[row(w) for w in widths] + [pl.BlockSpec((None, None, n_vt, TILE), lambda b, i: (b, i, 0, 0))],
        out_shape=[jax.ShapeDtypeStruct((n_batch, t_len, w), bf) for w in widths]
                  + [jax.ShapeDtypeStruct((n_batch, nt, n_vt, TILE), bf)],
        compiler_params=_cparams(("parallel", "arbitrary")),
        name="proj_even",
    )(h, mods_l, g, w_in, qg, wuq, kvg, wukv, wvt, *tabs_a, *tabs_b)


def _proj_odd_body(h_ref, mod_ref, g_ref, w_ref, q_ref, k_ref, v_ref, gate_ref):
    u = _norm_mod(h_ref[...], g_ref[...], mod_ref[...]).astype(jnp.bfloat16)
    d = D_MODEL
    q_ref[...] = (_dot(u, w_ref[:, :d]) * (HEAD_DIM ** -0.5)).astype(jnp.bfloat16)
    k_ref[...] = _dot(u, w_ref[:, d:2 * d]).astype(jnp.bfloat16)
    v_ref[...] = _dot(u, w_ref[:, 2 * d:3 * d]).astype(jnp.bfloat16)
    gate_ref[...] = _silu(_dot(u, w_ref[:, 3 * d:])).astype(jnp.bfloat16)


def _proj_odd(h, mods_l, g, w_in):
    n_batch, t_len, d = h.shape
    nt = t_len // TILE
    row = pl.BlockSpec((None, TILE, d), lambda b, i: (b, i, 0))
    full = lambda a: pl.BlockSpec(a.shape, lambda b, i: (0,) * a.ndim)
    return pl.pallas_call(
        _proj_odd_body,
        grid=(n_batch, nt),
        in_specs=[row, _mod_spec(n_batch), full(g), full(w_in)],
        out_specs=[row] * 4,
        out_shape=[jax.ShapeDtypeStruct((n_batch, t_len, d), jnp.bfloat16)] * 4,
        compiler_params=_cparams(("parallel", "arbitrary")),
        name="proj_odd",
    )(h, mods_l, g, w_in)


def _lane_half_masks(rows, dtype):
    lane = lax.broadcasted_iota(jnp.int32, (rows, LANES), 1)
    lo = lane < HEAD_DIM
    return lo, jnp.logical_not(lo)


def _win_body(sink_ref, q_ref, kc_ref, kp_ref, km_ref, kn_ref, vc_ref, vp_ref, vm_ref, vn_ref, o_ref,
              *, ctx_len, seq_len):
    i = pl.program_id(1)
    half = TILE // 2
    qpos = i * TILE - ctx_len + lax.broadcasted_iota(jnp.int32, (TILE, 2 * TILE), 0)
    kcol = lax.broadcasted_iota(jnp.int32, (TILE, 2 * TILE), 1)
    kpos = i * TILE - ctx_len - half + kcol
    ok = (jnp.abs(kpos - qpos) <= A_WINDOW) & (kpos >= 0) & (kpos < seq_len) & (i > 0)

    k_loc = jnp.concatenate([kp_ref[...], km_ref[...], kn_ref[...]], axis=0)
    v_loc = jnp.concatenate([vp_ref[...], vm_ref[...], vn_ref[...]], axis=0)
    k_ctx, v_ctx = kc_ref[...], vc_ref[...]
    lo_q, hi_q = _lane_half_masks(TILE, None)
    lo_l, hi_l = _lane_half_masks(2 * TILE, None)
    lo_c, hi_c = _lane_half_masks(ctx_len, None)
    zero = jnp.zeros((), jnp.bfloat16)
    v_loc_h = (jnp.where(lo_l, v_loc, zero), jnp.where(hi_l, v_loc, zero))
    v_ctx_h = (jnp.where(lo_c, v_ctx, zero), jnp.where(hi_c, v_ctx, zero))

    for p in range(A_HEADS // 2):
        q2 = q_ref[:, p * LANES:(p + 1) * LANES]
        out = jnp.zeros((TILE, LANES), jnp.float32)
        for hh, qmask in enumerate((lo_q, hi_q)):
            qm = jnp.where(qmask, q2, zero)
            sink = sink_ref[2 * p + hh]
            s_loc = jnp.where(ok, _dot_t(qm, k_loc), NEG)
            s_ctx = _dot_t(qm, k_ctx)
            m = jnp.maximum(jnp.maximum(jnp.max(s_loc, axis=1, keepdims=True),
                                        jnp.max(s_ctx, axis=1, keepdims=True)), sink)
            p_loc = jnp.exp(s_loc - m)
            p_ctx = jnp.exp(s_ctx - m)
            denom = (jnp.sum(p_loc, axis=1, keepdims=True) + jnp.sum(p_ctx, axis=1, keepdims=True)
                     + jnp.exp(sink - m))
            pv = _dot(p_loc.astype(jnp.bfloat16), v_loc_h[hh]) + _dot(p_ctx.astype(jnp.bfloat16), v_ctx_h[hh])
            out = out + pv * (1.0 / denom)
        o_ref[:, p * LANES:(p + 1) * LANES] = out.astype(jnp.bfloat16)


def _win_attn(qa, ka, va, sink, ctx_len):
    n_batch, t_len, _ = qa.shape
    nt = t_len // TILE
    assert ctx_len == TILE
    n128 = t_len // 128
    kv_ctx = pl.BlockSpec((None, ctx_len, LANES), lambda b, i: (b, 0, 0))
    kv_prev = pl.BlockSpec((None, 128, LANES), lambda b, i: (b, jnp.maximum(2 * i - 1, 0), 0))
    kv_mid = pl.BlockSpec((None, TILE, LANES), lambda b, i: (b, i, 0))
    kv_next = pl.BlockSpec((None, 128, LANES), lambda b, i: (b, jnp.minimum(2 * i + 2, n128 - 1), 0))
    kv_specs = [kv_ctx, kv_prev, kv_mid, kv_next]
    body = functools.partial(_win_body, ctx_len=ctx_len, seq_len=t_len - ctx_len)
    return pl.pallas_call(
        body,
        grid=(n_batch, nt),
        in_specs=[pl.BlockSpec(memory_space=pltpu.SMEM),
                  pl.BlockSpec((None, TILE, 512), lambda b, i: (b, i, 0))] + kv_specs + kv_specs,
        out_specs=pl.BlockSpec((None, TILE, 512), lambda b, i: (b, i, 0)),
        out_shape=jax.ShapeDtypeStruct((n_batch, t_len, 512), jnp.bfloat16),
        compiler_params=_cparams(("parallel", "arbitrary")),
        name="win_attn",
    )(sink, qa, ka, ka, ka, ka, va, va, va, va)


def _mla_body(q_ref, k_ref, vt_ref, o_ref, *, ctx_len, n_chunks):
    i = pl.program_id(2)
    q = (q_ref[:, :LANES], q_ref[:, LANES:])
    sub = MLA_CHUNK // TILE

    def scores(start, size):
        return tuple(_dot_t(k_ref[pl.ds(start, size), hh * LANES:(hh + 1) * LANES], q[hh]) for hh in range(2))

    def update(state, s, tile0, n_tiles):
        new = []
        for hh in range(2):
            m_prev, l_prev, acc = state[hh]
            m_new = jnp.maximum(m_prev, jnp.max(s[hh], axis=0, keepdims=True))
            alpha = jnp.exp2(m_prev - m_new)
            p = jnp.exp2(s[hh] - m_new)
            l_new = alpha * l_prev + jnp.sum(p, axis=0, keepdims=True)
            pb = p.astype(jnp.bfloat16)
            pv = None
            for c in range(n_tiles):
                vt = vt_ref[tile0 + c, hh * B_V:(hh + 1) * B_V, :]
                part = _dot(vt, pb[c * TILE:(c + 1) * TILE, :])
                pv = part if pv is None else pv + part
            new.append((m_new, l_new, acc * alpha + pv))
        return tuple(new)

    def finish(state):
        o_t = jnp.concatenate([acc * (1.0 / l) for _, l, acc in state], axis=0)
        o_ref[...] = o_t.T.astype(jnp.bfloat16)

    init = tuple((jnp.full((1, TILE), NEG, jnp.float32), jnp.zeros((1, TILE), jnp.float32),
                  jnp.zeros((B_V, TILE), jnp.float32)) for _ in range(2))

    @pl.when(i == 0)
    def _():
        finish(update(init, scores(0, ctx_len), 0, 1))

    @pl.when(i > 0)
    def _():
        state = update(init, scores(0, ctx_len), 0, 1)
        s = scores(ctx_len, MLA_CHUNK)

        def body(j, carry):
            state, s = carry
            start = pl.multiple_of(ctx_len + (j + 1) * MLA_CHUNK, TILE)
            s_next = scores(start, MLA_CHUNK)
            return update(state, s, 1 + j * sub, sub), s_next

        state, s = lax.fori_loop(0, n_chunks - 1, body, (state, s))
        finish(update(state, s, 1 + (n_chunks - 1) * sub, sub))


def _mla_attn(qb, kb, vbt, ctx_len):
    n_batch, t_len, _ = qb.shape
    nt = t_len // TILE
    seq = t_len - ctx_len
    assert ctx_len == TILE and seq % MLA_CHUNK == 0
    body = functools.partial(_mla_body, ctx_len=ctx_len, n_chunks=seq // MLA_CHUNK)
    return pl.pallas_call(
        body,
        grid=(n_batch, B_HEADS // 2, nt),
        in_specs=[pl.BlockSpec((None, TILE, 2 * LANES), lambda b, p, i: (b, i, p)),
                  pl.BlockSpec((None, t_len, 2 * LANES), lambda b, p, i: (b, 0, p)),
                  pl.BlockSpec((None, nt, 2 * B_V, TILE), lambda b, p, i: (b, 0, p, 0))],
        out_specs=pl.BlockSpec((None, TILE, LANES), lambda b, p, i: (b, i, p)),
        out_shape=jax.ShapeDtypeStruct((n_batch, t_len, B_HEADS * B_V), jnp.bfloat16),
        compiler_params=_cparams(("parallel", "parallel", "arbitrary")),
        name="mla_attn",
    )(qb, kb, vbt)


def _nbr_row_valid(case, jr, t):
    rpt = TILE // GRID_W
    lo = (rpt, jr + rpt - C_KH // 2, 2 * rpt - C_KH)[case]
    return lo <= t < lo + C_KH


def _nbr_fill_bias(tz_ref, bias_ref):
    rpt = TILE // GRID_W
    lane_lo = lax.broadcasted_iota(jnp.int32, (GRID_W, LANES), 1) < GRID_W
    neg = jnp.full((GRID_W, LANES), NEG, jnp.float32)
    for hh in range(2):
        bias_ref[3, hh] = jnp.full((TILE, 3 * TILE), NEG, jnp.float32)
        for case in range(3):
            for jr in range(rpt):
                for tp in range(3 * rpt // 2):
                    ok0 = _nbr_row_valid(case, jr, 2 * tp)
                    ok1 = _nbr_row_valid(case, jr, 2 * tp + 1)
                    if ok0 or ok1:
                        blk = tz_ref[hh, 2 * tp - rpt - jr + C_KH - 1]
                        if not ok1:
                            blk = jnp.where(lane_lo, blk, NEG)
                        elif not ok0:
                            blk = jnp.where(lane_lo, NEG, blk)
                    else:
                        blk = neg
                    bias_ref[case, hh, jr * GRID_W:(jr + 1) * GRID_W, tp * LANES:(tp + 1) * LANES] = blk


def _nbr_body(q_ref, kc_ref, k0_ref, k1_ref, k2_ref, vc_ref, v0_ref, v1_ref, v2_ref, tz_ref, o_ref, bias_ref):
    i = pl.program_id(2)
    nt = pl.num_programs(2)

    @pl.when(i == 0)
    def _():
        _nbr_fill_bias(tz_ref, bias_ref)

    case = jnp.where(i == 0, 3, jnp.where(i == 1, 0, jnp.where(i == nt - 1, 2, 1)))
    k_loc = jnp.concatenate([k0_ref[...], k1_ref[...], k2_ref[...]], axis=0)
    v_loc = jnp.concatenate([v0_ref[...], v1_ref[...], v2_ref[...]], axis=0)
    k_ctx, v_ctx = kc_ref[...], vc_ref[...]
    q2 = q_ref[...]
    zero = jnp.zeros((), jnp.bfloat16)
    lo_q, hi_q = _lane_half_masks(TILE, None)
    lo_l, hi_l = _lane_half_masks(3 * TILE, None)
    lo_c, hi_c = _lane_half_masks(k_ctx.shape[0], None)
    v_loc_h = (jnp.where(lo_l, v_loc, zero), jnp.where(hi_l, v_loc, zero))
    v_ctx_h = (jnp.where(lo_c, v_ctx, zero), jnp.where(hi_c, v_ctx, zero))
    out = jnp.zeros((TILE, LANES), jnp.float32)
    for hh, qmask in enumerate((lo_q, hi_q)):
        qm = jnp.where(qmask, q2, zero)
        s_loc = _dot_t(qm, k_loc) + bias_ref[case, hh]
        s_ctx = _dot_t(qm, k_ctx)
        m = jnp.maximum(jnp.max(s_loc, axis=1, keepdims=True), jnp.max(s_ctx, axis=1, keepdims=True))
        p_loc = jnp.exp(s_loc - m)
        p_ctx = jnp.exp(s_ctx - m)
        denom = jnp.sum(p_loc, axis=1, keepdims=True) + jnp.sum(p_ctx, axis=1, keepdims=True)
        pv = _dot(p_loc.astype(jnp.bfloat16), v_loc_h[hh]) + _dot(p_ctx.astype(jnp.bfloat16), v_ctx_h[hh])
        out = out + pv * (1.0 / denom)
    o_ref[...] = out.astype(jnp.bfloat16)


def _nbr_attn(q, k, v, tz, ctx_len):
    n_batch, t_len, d = q.shape
    nt = t_len // TILE
    assert ctx_len == TILE and TILE % GRID_W == 0 and C_KH == 2 * (TILE // GRID_W)
    assert nt >= 4, "needs distinct first / interior / last latent tiles"
    blk = lambda f: pl.BlockSpec((None, TILE, LANES), lambda b, p, i: (b, f(i), p))
    kv_specs = [blk(lambda i: 0), blk(lambda i: jnp.clip(i - 1, 1, nt - 1)), blk(lambda i: i),
                blk(lambda i: jnp.clip(i + 1, 1, nt - 1))]
    tz_spec = pl.BlockSpec((2,) + tz.shape[1:], lambda b, p, i: (p, 0, 0, 0))
    return pl.pallas_call(
        _nbr_body,
        grid=(n_batch, C_HEADS // 2, nt),
        in_specs=[blk(lambda i: i)] + kv_specs + kv_specs + [tz_spec],
        out_specs=blk(lambda i: i),
        out_shape=jax.ShapeDtypeStruct((n_batch, t_len, d), jnp.bfloat16),
        scratch_shapes=[pltpu.VMEM((4, 2, TILE, 3 * TILE), jnp.float32)],
        compiler_params=_cparams(("arbitrary", "arbitrary", "arbitrary")),
        name="nbr_attn",
    )(q, k, k, k, k, v, v, v, v, tz)


def _nbr_toeplitz(rpb):
    n_off = 2 * C_KW - 1
    pad = GRID_W - C_KW
    ext = jnp.pad(rpb, ((0, 0), (0, 0), (pad, pad)), mode="edge")
    tz = jnp.stack([ext[:, :, GRID_W - 1 - qc:2 * GRID_W - 1 - qc] for qc in range(GRID_W)], axis=2)
    assert ext.shape[-1] == n_off + 2 * pad == 2 * GRID_W - 1
    qc = np.arange(GRID_W)[:, None]
    kc = np.arange(GRID_W)[None, :]
    cs = np.clip(qc - C_KW // 2, 0, GRID_W - C_KW)
    col_ok = jnp.asarray((kc >= cs) & (kc < cs + C_KW))
    tz = jnp.where(col_ok, tz, NEG)
    return jnp.concatenate([tz[:, :-1], tz[:, 1:]], axis=-1)


def _out_body(h_ref, mod_ref, o1_ref, o2_ref, gate_ref, w_ref, *rest, final):
    half = D_MODEL // 2
    g = gate_ref[...].astype(jnp.float32)
    og1 = (o1_ref[...].astype(jnp.float32) * g[:, :half]).astype(jnp.bfloat16)
    og2 = (o2_ref[...].astype(jnp.float32) * g[:, half:]).astype(jnp.bfloat16)
    y = _dot(og1, w_ref[:half, :]) + _dot(og2, w_ref[half:, :])
    h_new = h_ref[...] + mod_ref[:, 2 * D_MODEL:] * y
    if final:
        fg_ref, out_ref = rest
        out_ref[...] = _rms(h_new, fg_ref[...])
    else:
        rest[0][...] = h_new


def _out_proj(h, mods_l, o1, o2, gate, w, o2_block, final_g=None):
    n_batch, t_len, d = h.shape
    nt = t_len // TILE
    final = final_g is not None
    off = 1 if final else 0
    row = lambda wd, blk=0: pl.BlockSpec((None, TILE, wd), lambda b, i: (b, i + off, blk))
    mod = pl.BlockSpec((None, 1, 3 * D_MODEL), lambda b, i: (jnp.where(i + off == 0, n_batch, b), 0, 0))
    full = lambda a: pl.BlockSpec(a.shape, lambda b, i: (0,) * a.ndim)
    in_specs = [row(d), mod, row(d // 2), row(d // 2, o2_block), row(d), full(w)]
    args = [h, mods_l, o1, o2, gate, w]
    if final:
        in_specs.append(full(final_g))
        args.append(final_g)
        out_spec = pl.BlockSpec((None, TILE, d), lambda b, i: (b, i, 0))
        out_shape = jax.ShapeDtypeStruct((n_batch, t_len - TILE, d), jnp.float32)
        aliases = {}
    else:
        out_spec = row(d)
        out_shape = jax.ShapeDtypeStruct(h.shape, jnp.float32)
        aliases = {0: 0}
    return pl.pallas_call(
        functools.partial(_out_body, final=final),
        grid=(n_batch, nt - off),
        in_specs=in_specs,
        out_specs=out_spec,
        out_shape=out_shape,
        input_output_aliases=aliases,
        compiler_params=_cparams(("parallel", "arbitrary")),
        name="out_proj_final" if final else "out_proj",
    )(*args)


def _deinterleave(n):
    return np.concatenate([np.arange(0, n, 2), np.arange(1, n, 2)])


def _take_cols(w, cols):
    cols = np.asarray(cols)
    picked = jnp.take(w, jnp.asarray(np.maximum(cols, 0)), axis=1)
    return jnp.where(jnp.asarray(cols >= 0)[None, :], picked, 0.0)


def _even_layout():
    perm = _deinterleave(HEAD_DIM)
    qa0, ka0 = 0, A_HEADS * HEAD_DIM
    va0 = ka0 + A_KV_HEADS * HEAD_DIM
    za0 = va0 + A_KV_HEADS * HEAD_DIM
    cq0 = za0 + A_HEADS * HEAD_DIM
    ckv0 = cq0 + B_Q_LORA
    kr0 = ckv0 + B_KV_LORA
    zb0 = kr0 + B_ROPE
    cols = []
    cols += [qa0 + hd * HEAD_DIM + perm for hd in _A_SLOTS]
    cols += [ka0 + kv * HEAD_DIM + perm for kv in range(A_KV_HEADS)]
    cols += [va0 + np.arange(A_KV_HEADS * HEAD_DIM)]
    cols += [za0 + hd * HEAD_DIM + np.arange(HEAD_DIM) for hd in _A_SLOTS]
    cols += [zb0 + np.arange(B_HEADS * B_V)]
    cols += [cq0 + np.arange(B_Q_LORA), ckv0 + np.arange(B_KV_LORA)]
    cols += [np.full(B_NOPE, -1), kr0 + _deinterleave(B_ROPE), np.full(LANES - B_NOPE - B_ROPE, -1)]
    in_cols = np.concatenate(cols)
    assert in_cols.shape[0] == _AB_COLS

    uq = []
    for hd in range(B_HEADS):
        base = hd * (B_NOPE + B_ROPE)
        uq += [base + np.arange(B_NOPE), base + B_NOPE + _deinterleave(B_ROPE),
               np.full(LANES - B_NOPE - B_ROPE, -1)]
    ukv_k, ukv_v = [], []
    pad = np.full(LANES - B_NOPE, -1)
    for hd in range(B_HEADS):
        base = hd * (B_NOPE + B_V)
        ukv_k += [base + np.arange(B_NOPE), pad]
        ukv_v += [base + B_NOPE + np.arange(B_V)]
    out_rows = np.concatenate([hd * HEAD_DIM + np.arange(HEAD_DIM) for hd in _A_SLOTS]
                              + [A_HEADS * HEAD_DIM + np.arange(B_HEADS * B_V)])
    return in_cols, np.concatenate(uq), np.concatenate(ukv_k), np.concatenate(ukv_v), out_rows


def _rope_tables(seq_len, ctx_len):
    t = jnp.arange(seq_len, dtype=jnp.int32)
    row = (t // GRID_W).astype(jnp.float32)
    col = (t % GRID_W).astype(jnp.float32)

    def angles(rot_dim):
        ppa = rot_dim // 4
        inv = ROPE_THETA ** (-jnp.arange(ppa, dtype=jnp.float32) / ppa)
        ang = jnp.concatenate([row[:, None] * inv, col[:, None] * inv], axis=-1)
        return jnp.cos(ang), jnp.sin(ang)

    def with_ctx(c, s1, s2):
        pad = lambda a, v: jnp.concatenate([jnp.full((ctx_len, LANES), v, jnp.float32), a], axis=0)
        return pad(c, 1.0), pad(s1, 0.0), pad(s2, 0.0)

    cos_a, sin_a = angles(HEAD_DIM)
    z32 = jnp.zeros_like(sin_a)
    tabs_a = with_ctx(jnp.tile(cos_a, (1, 4)), jnp.tile(jnp.concatenate([-sin_a, z32], 1), (1, 2)),
                      jnp.tile(jnp.concatenate([z32, sin_a], 1), (1, 2)))
    cos_b, sin_b = angles(B_ROPE)
    one64 = jnp.ones((seq_len, B_NOPE), jnp.float32)
    z64 = jnp.zeros((seq_len, B_NOPE), jnp.float32)
    z16 = jnp.zeros_like(sin_b)
    one32 = jnp.ones((seq_len, LANES - B_NOPE - B_ROPE), jnp.float32)
    z32b = jnp.zeros((seq_len, LANES - B_NOPE - B_ROPE), jnp.float32)
    tabs_b = with_ctx(jnp.concatenate([one64, cos_b, cos_b, one32], 1),
                      jnp.concatenate([z64, -sin_b, z16, z32b], 1),
                      jnp.concatenate([z64, z16, sin_b, z32b], 1))
    return tabs_a, tabs_b


def kernel(x, c, ctx, c_ctx, ada_w, ada_b, norm_g, ab_in_w, ab_out_w, a_sink, b_q_norm_g, b_w_uq,
           b_kv_norm_g, b_w_ukv, c_in_w, c_out_w, c_rpb, final_g):
    n_batch, seq_len, d = x.shape
    ctx_len = ctx.shape[1]
    depth = ada_w.shape[0]
    assert d == D_MODEL and ctx_len == TILE and seq_len % MLA_CHUNK == 0 and seq_len % GRID_W == 0
    assert depth % 2 == 0, "the final norm is fused into the last (odd) layer's output projection"
    bf = jnp.bfloat16

    mod_rows = -(-(n_batch + 1) // 8) * 8
    cc = jnp.concatenate([c, c_ctx[None, :], jnp.zeros((mod_rows - n_batch - 1, d), jnp.float32)], axis=0)
    mods = _mods(cc, ada_w, ada_b).reshape(depth, mod_rows, 1, 3 * d)

    in_cols, uq_cols, uk_cols, uv_cols, out_rows = _even_layout()
    tabs_a, tabs_b = _rope_tables(seq_len, ctx_len)
    h = jnp.concatenate([ctx, x], axis=1)
    row2 = lambda v: v.reshape(1, -1)

    out = None
    for layer in range(depth):
        i = layer // 2
        last = layer == depth - 1
        g = row2(norm_g[layer])
        if layer % 2 == 0:
            w_in = _take_cols(ab_in_w[i], in_cols).astype(bf)
            wuq = _take_cols(b_w_uq[i], uq_cols).astype(bf)
            wuk = _take_cols(b_w_ukv[i], uk_cols).astype(bf)
            wvt = _take_cols(b_w_ukv[i], uv_cols).T.astype(bf)
            w_out = jnp.take(ab_out_w[i], jnp.asarray(out_rows), axis=0).astype(bf)
            sink = jnp.take(a_sink[i], jnp.asarray(_A_SLOTS))
            qa, ka, va, gate, qb, kb, vbt = _proj_even(h, mods[layer], g, w_in, row2(b_q_norm_g[i]), wuq,
                                                       row2(b_kv_norm_g[i]), wuk, wvt, tabs_a, tabs_b)
            o1 = _win_attn(qa, ka, va, sink, ctx_len)
            o2 = _mla_attn(qb, kb, vbt, ctx_len)
            o2_block = 0
        else:
            q, k, v, gate = _proj_odd(h, mods[layer], g, c_in_w[i].astype(bf))
            o1 = o2 = _nbr_attn(q, k, v, _nbr_toeplitz(c_rpb[i]), ctx_len)
            o2_block = 1
            w_out = c_out_w[i].astype(bf)
        if last:
            out = _out_proj(h, mods[layer], o1, o2, gate, w_out, o2_block, final_g=row2(final_g))
        else:
            h = _out_proj(h, mods[layer], o1, o2, gate, w_out, o2_block)
    return out
```

```python
import functools
import math

import numpy as np
import jax
import jax.numpy as jnp
from jax import lax
from jax.experimental import pallas as pl
from jax.experimental.pallas import tpu as pltpu

D_MODEL = 1024
HEAD_DIM = 64
GRID_W = 64
ROPE_THETA = 10000.0
NORM_EPS = 1e-6
A_HEADS = 8
A_KV_HEADS = 2
A_WINDOW = 128
B_HEADS = 8
B_Q_LORA = 384
B_KV_LORA = 256
B_NOPE = 64
B_ROPE = 32
B_V = 64
C_HEADS = 16
C_KH = 8
C_KW = 16

LANES = 128
TILE = 256
MLA_CHUNK = 1024
NBR_PAIRS = 2
LOG2E = math.log2(math.e)
NEG = -1e30
VMEM_LIMIT = 48 * 1024 * 1024

_QA, _KA, _VA, _ZA, _ZB, _CQ, _CKV, _KR, _AB_COLS = 0, 512, 640, 768, 1280, 1792, 2176, 2432, 2560
_A_SLOTS = (0, 4, 1, 5, 2, 6, 3, 7)


def _cparams(sem):
    return pltpu.CompilerParams(dimension_semantics=sem, vmem_limit_bytes=VMEM_LIMIT)


def _silu(x):
    return x * (1.0 / (1.0 + jnp.exp(-x)))


def _dot(a, b):
    return jnp.dot(a, b, preferred_element_type=jnp.float32)


def _dot_t(a, b):
    return lax.dot_general(a, b, (((1,), (1,)), ((), ())), preferred_element_type=jnp.float32)


def _mods_body(c_ref, w_ref, b_ref, o_ref):
    sc = _silu(c_ref[...])
    o_ref[...] = jnp.dot(sc, w_ref[...], preferred_element_type=jnp.float32,
                         precision=lax.Precision.HIGHEST) + b_ref[...]


def _mods(cc, ada_w, ada_b):
    depth, d, d3 = ada_w.shape
    rows = cc.shape[0]
    tn = 512
    return pl.pallas_call(
        _mods_body,
        grid=(depth, d3 // tn),
        in_specs=[
            pl.BlockSpec((rows, d), lambda l, j: (0, 0)),
            pl.BlockSpec((None, d, tn), lambda l, j: (l, 0, j)),
            pl.BlockSpec((None, 1, tn), lambda l, j: (l, 0, j)),
        ],
        out_specs=pl.BlockSpec((None, rows, tn), lambda l, j: (l, 0, j)),
        out_shape=jax.ShapeDtypeStruct((depth, rows, d3), jnp.float32),
        compiler_params=_cparams(("arbitrary", "arbitrary")),
        name="mods",
    )(cc, ada_w, ada_b.reshape(depth, 1, d3))


def _mod_spec(n_batch):
    return pl.BlockSpec((None, 1, 3 * D_MODEL), lambda b, i: (jnp.where(i == 0, n_batch, b), 0, 0))


def _norm_mod(h, g, mod):
    y = h * lax.rsqrt(jnp.mean(h * h, axis=-1, keepdims=True) + NORM_EPS) * g
    return y * (1.0 + mod[:, D_MODEL:2 * D_MODEL]) + mod[:, :D_MODEL]


def _rms(x, g):
    return x * lax.rsqrt(jnp.mean(x * x, axis=-1, keepdims=True) + NORM_EPS) * g


def _rope(x, c, s1, s2, half):
    return x * c + pltpu.roll(x, LANES - half, 1) * s1 + pltpu.roll(x, half, 1) * s2


def _proj_even_body(h_ref, mod_ref, g_ref, w_ref, qg_ref, wuq_ref, kvg_ref, wukv_ref, wvt_ref,
                    ca_ref, sa1_ref, sa2_ref, cb_ref, sb1_ref, sb2_ref,
                    qa_ref, ka_ref, va_ref, gate_ref, qb_ref, kb_ref, vbt_ref):
    u = _norm_mod(h_ref[...], g_ref[...], mod_ref[...]).astype(jnp.bfloat16)
    ca, sa1, sa2 = ca_ref[...], sa1_ref[...], sa2_ref[...]
    cb, sb1, sb2 = cb_ref[...], sb1_ref[...], sb2_ref[...]

    qk = _dot(u, w_ref[:, _QA:_VA])
    a_scale = HEAD_DIM ** -0.5
    for j in range(4):
        blk = _rope(qk[:, j * LANES:(j + 1) * LANES], ca, sa1, sa2, HEAD_DIM // 2)
        qa_ref[:, j * LANES:(j + 1) * LANES] = (blk * a_scale).astype(jnp.bfloat16)
    ka_ref[...] = _rope(qk[:, _KA:_VA], ca, sa1, sa2, HEAD_DIM // 2).astype(jnp.bfloat16)
    va_ref[...] = _dot(u, w_ref[:, _VA:_ZA]).astype(jnp.bfloat16)
    gate_ref[...] = _silu(_dot(u, w_ref[:, _ZA:_CQ])).astype(jnp.bfloat16)

    lat = _dot(u, w_ref[:, _CQ:_AB_COLS])
    cq = _rms(lat[:, :B_Q_LORA], qg_ref[...]).astype(jnp.bfloat16)
    ckv = _rms(lat[:, B_Q_LORA:B_Q_LORA + B_KV_LORA], kvg_ref[...]).astype(jnp.bfloat16)
    kr = _rope(lat[:, B_Q_LORA + B_KV_LORA:], cb, sb1, sb2, B_ROPE // 2)

    qb = _dot(cq, wuq_ref[...])
    b_scale = (B_NOPE + B_ROPE) ** -0.5 * LOG2E
    kb = _dot(ckv, wukv_ref[...])
    for hd in range(B_HEADS):
        sl = slice(hd * LANES, (hd + 1) * LANES)
        qb_ref[:, sl] = (_rope(qb[:, sl], cb, sb1, sb2, B_ROPE // 2) * b_scale).astype(jnp.bfloat16)
        kb_ref[:, sl] = (kb[:, sl] + kr).astype(jnp.bfloat16)
    vbt_ref[...] = _dot_t(wvt_ref[...], ckv).astype(jnp.bfloat16)


def _proj_even(h, mods_l, g, w_in, qg, wuq, kvg, wukv, wvt, tabs_a, tabs_b):
    n_batch, t_len, d = h.shape
    nt = t_len // TILE
    row = lambda w: pl.BlockSpec((None, TILE, w), lambda b, i: (b, i, 0))
    full = lambda a: pl.BlockSpec(a.shape, lambda b, i: (0,) * a.ndim)
    tab = pl.BlockSpec((TILE, LANES), lambda b, i: (i, 0))
    bf = jnp.bfloat16
    widths = (512, 128, 128, 1024, 1024, 1024)
    n_vt = B_HEADS * B_V
    return pl.pallas_call(
        _proj_even_body,
        grid=(n_batch, nt),
        in_specs=[row(d), _mod_spec(n_batch), full(g), full(w_in), full(qg), full(wuq), full(kvg), full(wukv),
                  full(wvt)] + [tab] * 6,
        out_specs=[row(w) for w in widths] + [pl.BlockSpec((None, None, n_vt, TILE), lambda b, i: (b, i, 0, 0))],
        out_shape=[jax.ShapeDtypeStruct((n_batch, t_len, w), bf) for w in widths]
                  + [jax.ShapeDtypeStruct((n_batch, nt, n_vt, TILE), bf)],
        compiler_params=_cparams(("parallel", "arbitrary")),
        name="proj_even",
    )(h, mods_l, g, w_in, qg, wuq, kvg, wukv, wvt, *tabs_a, *tabs_b)


def _proj_odd_body(h_ref, mod_ref, g_ref, w_ref, wvt_ref, q_ref, k_ref, vt_ref, gate_ref):
    u = _norm_mod(h_ref[...], g_ref[...], mod_ref[...]).astype(jnp.bfloat16)
    d = D_MODEL
    q_ref[...] = (_dot(u, w_ref[:, :d]) * (HEAD_DIM ** -0.5 * LOG2E)).astype(jnp.bfloat16)
    k_ref[...] = _dot(u, w_ref[:, d:2 * d]).astype(jnp.bfloat16)
    vt_ref[...] = _dot_t(wvt_ref[...], u).astype(jnp.bfloat16)
    gate_ref[...] = _silu(_dot(u, w_ref[:, 2 * d:])).astype(jnp.bfloat16)


def _proj_odd(h, mods_l, g, w_qkz, wvt):
    n_batch, t_len, d = h.shape
    nt = t_len // TILE
    row = pl.BlockSpec((None, TILE, d), lambda b, i: (b, i, 0))
    row_t = pl.BlockSpec((None, None, d, TILE), lambda b, i: (b, i, 0, 0))
    full = lambda a: pl.BlockSpec(a.shape, lambda b, i: (0,) * a.ndim)
    out = jax.ShapeDtypeStruct((n_batch, t_len, d), jnp.bfloat16)
    return pl.pallas_call(
        _proj_odd_body,
        grid=(n_batch, nt),
        in_specs=[row, _mod_spec(n_batch), full(g), full(w_qkz), full(wvt)],
        out_specs=[row, row, row_t, row],
        out_shape=[out, out, jax.ShapeDtypeStruct((n_batch, nt, d, TILE), jnp.bfloat16), out],
        compiler_params=_cparams(("parallel", "arbitrary")),
        name="proj_odd",
    )(h, mods_l, g, w_qkz, wvt)


def _lane_half_masks(rows, dtype):
    lane = lax.broadcasted_iota(jnp.int32, (rows, LANES), 1)
    lo = lane < HEAD_DIM
    return lo, jnp.logical_not(lo)


def _win_body(sink_ref, q_ref, kc_ref, kp_ref, km_ref, kn_ref, vc_ref, vp_ref, vm_ref, vn_ref, o_ref,
              *, ctx_len, seq_len):
    i = pl.program_id(1)
    half = TILE // 2
    qpos = i * TILE - ctx_len + lax.broadcasted_iota(jnp.int32, (TILE, 2 * TILE), 0)
    kcol = lax.broadcasted_iota(jnp.int32, (TILE, 2 * TILE), 1)
    kpos = i * TILE - ctx_len - half + kcol
    ok = (jnp.abs(kpos - qpos) <= A_WINDOW) & (kpos >= 0) & (kpos < seq_len) & (i > 0)

    k_loc = jnp.concatenate([kp_ref[...], km_ref[...], kn_ref[...]], axis=0)
    v_loc = jnp.concatenate([vp_ref[...], vm_ref[...], vn_ref[...]], axis=0)
    k_ctx, v_ctx = kc_ref[...], vc_ref[...]
    lo_q, hi_q = _lane_half_masks(TILE, None)
    lo_l, hi_l = _lane_half_masks(2 * TILE, None)
    lo_c, hi_c = _lane_half_masks(ctx_len, None)
    zero = jnp.zeros((), jnp.bfloat16)
    v_loc_h = (jnp.where(lo_l, v_loc, zero), jnp.where(hi_l, v_loc, zero))
    v_ctx_h = (jnp.where(lo_c, v_ctx, zero), jnp.where(hi_c, v_ctx, zero))

    for p in range(A_HEADS // 2):
        q2 = q_ref[:, p * LANES:(p + 1) * LANES]
        out = jnp.zeros((TILE, LANES), jnp.float32)
        for hh, qmask in enumerate((lo_q, hi_q)):
            qm = jnp.where(qmask, q2, zero)
            sink = sink_ref[2 * p + hh]
            s_loc = jnp.where(ok, _dot_t(qm, k_loc), NEG)
            s_ctx = _dot_t(qm, k_ctx)
            m = jnp.maximum(jnp.maximum(jnp.max(s_loc, axis=1, keepdims=True),
                                        jnp.max(s_ctx, axis=1, keepdims=True)), sink)
            p_loc = jnp.exp(s_loc - m)
            p_ctx = jnp.exp(s_ctx - m)
            denom = (jnp.sum(p_loc, axis=1, keepdims=True) + jnp.sum(p_ctx, axis=1, keepdims=True)
                     + jnp.exp(sink - m))
            pv = _dot(p_loc.astype(jnp.bfloat16), v_loc_h[hh]) + _dot(p_ctx.astype(jnp.bfloat16), v_ctx_h[hh])
            out = out + pv * (1.0 / denom)
        o_ref[:, p * LANES:(p + 1) * LANES] = out.astype(jnp.bfloat16)


def _win_attn(qa, ka, va, sink, ctx_len):
    n_batch, t_len, _ = qa.shape
    nt = t_len // TILE
    assert ctx_len == TILE
    n128 = t_len // 128
    kv_ctx = pl.BlockSpec((None, ctx_len, LANES), lambda b, i: (b, 0, 0))
    kv_prev = pl.BlockSpec((None, 128, LANES), lambda b, i: (b, jnp.maximum(2 * i - 1, 0), 0))
    kv_mid = pl.BlockSpec((None, TILE, LANES), lambda b, i: (b, i, 0))
    kv_next = pl.BlockSpec((None, 128, LANES), lambda b, i: (b, jnp.minimum(2 * i + 2, n128 - 1), 0))
    kv_specs = [kv_ctx, kv_prev, kv_mid, kv_next]
    body = functools.partial(_win_body, ctx_len=ctx_len, seq_len=t_len - ctx_len)
    return pl.pallas_call(
        body,
        grid=(n_batch, nt),
        in_specs=[pl.BlockSpec(memory_space=pltpu.SMEM),
                  pl.BlockSpec((None, TILE, 512), lambda b, i: (b, i, 0))] + kv_specs + kv_specs,
        out_specs=pl.BlockSpec((None, TILE, 512), lambda b, i: (b, i, 0)),
        out_shape=jax.ShapeDtypeStruct((n_batch, t_len, 512), jnp.bfloat16),
        compiler_params=_cparams(("parallel", "arbitrary")),
        name="win_attn",
    )(sink, qa, ka, ka, ka, ka, va, va, va, va)


def _mla_body(q_ref, k_ref, vt_ref, o_ref, sa_ref, sb_ref, p_ref, *, ctx_len, n_chunks):
    i = pl.program_id(2)
    q = (q_ref[:, :LANES], q_ref[:, LANES:])
    sub = MLA_CHUNK // TILE
    blk = LANES

    def scores(chunk, s_ref):
        start, size = (0, ctx_len) if chunk is None else (pl.multiple_of(ctx_len + chunk * MLA_CHUNK, TILE), MLA_CHUNK)
        cm = []
        for hh in range(2):
            s = _dot_t(k_ref[pl.ds(start, size), hh * LANES:(hh + 1) * LANES], q[hh])
            s_ref[hh, :size, :] = s
            cm.append(jnp.max(s, axis=0, keepdims=True))
        return tuple(cm)

    def update(state, cm, s_ref, chunk):
        tile0, n_tiles = (0, 1) if chunk is None else (1 + chunk * sub, sub)
        size = n_tiles * TILE
        new = []
        for hh in range(2):
            m_prev, acc = state[hh]
            m_new = jnp.maximum(m_prev, cm[hh])
            alpha = jnp.exp2(m_prev - m_new)
            for c in range(size // blk):
                p = jnp.exp2(s_ref[hh, c * blk:(c + 1) * blk, :] - m_new)
                p_ref[hh, c * blk:(c + 1) * blk, :] = p.astype(jnp.bfloat16)
            vt = jnp.concatenate([vt_ref[tile0 + c, hh * B_V:(hh + 1) * B_V, :] for c in range(n_tiles)], axis=1)
            vt = jnp.concatenate([vt, jnp.ones((ones_rows, size), jnp.bfloat16)], axis=0)
            new.append((m_new, acc * alpha + _dot(vt, p_ref[hh, :size, :])))
        return tuple(new)

    def finish(state):
        o_t = jnp.concatenate([acc[:B_V] * (1.0 / acc[B_V:B_V + 1]) for _, acc in state], axis=0)
        o_ref[...] = o_t.T.astype(jnp.bfloat16)

    ones_rows = 16
    init = tuple((jnp.full((1, TILE), NEG, jnp.float32), jnp.zeros((B_V + ones_rows, TILE), jnp.float32))
                 for _ in range(2))

    @pl.when(i == 0)
    def _():
        finish(update(init, scores(None, sa_ref), sa_ref, None))

    @pl.when(i > 0)
    def _():
        state = update(init, scores(None, sa_ref), sa_ref, None)
        cm_a = scores(0, sa_ref)

        def body(t, carry):
            state, cm_a = carry
            cm_b = scores(2 * t + 1, sb_ref)
            state = update(state, cm_a, sa_ref, 2 * t)
            cm_a = scores(2 * t + 2, sa_ref)
            state = update(state, cm_b, sb_ref, 2 * t + 1)
            return state, cm_a

        state, cm_a = lax.fori_loop(0, n_chunks // 2 - 1, body, (state, cm_a))
        cm_b = scores(n_chunks - 1, sb_ref)
        state = update(state, cm_a, sa_ref, n_chunks - 2)
        finish(update(state, cm_b, sb_ref, n_chunks - 1))


def _mla_attn(qb, kb, vbt, ctx_len):
    n_batch, t_len, _ = qb.shape
    nt = t_len // TILE
    seq = t_len - ctx_len
    assert ctx_len == TILE and seq % (2 * MLA_CHUNK) == 0, "the chunk loop handles two chunks per trip"
    body = functools.partial(_mla_body, ctx_len=ctx_len, n_chunks=seq // MLA_CHUNK)
    s_buf = pltpu.VMEM((2, MLA_CHUNK, TILE), jnp.float32)
    return pl.pallas_call(
        body,
        grid=(n_batch, B_HEADS // 2, nt),
        in_specs=[pl.BlockSpec((None, TILE, 2 * LANES), lambda b, p, i: (b, i, p)),
                  pl.BlockSpec((None, t_len, 2 * LANES), lambda b, p, i: (b, 0, p)),
                  pl.BlockSpec((None, nt, 2 * B_V, TILE), lambda b, p, i: (b, 0, p, 0))],
        out_specs=pl.BlockSpec((None, TILE, LANES), lambda b, p, i: (b, i, p)),
        out_shape=jax.ShapeDtypeStruct((n_batch, t_len, B_HEADS * B_V), jnp.bfloat16),
        scratch_shapes=[s_buf, s_buf, pltpu.VMEM((2, MLA_CHUNK, TILE), jnp.bfloat16)],
        compiler_params=_cparams(("parallel", "parallel", "arbitrary")),
        name="mla_attn",
    )(qb, kb, vbt)


def _nbr_row_valid(case, jr, t):
    rpt = TILE // GRID_W
    lo = (rpt, jr + rpt - C_KH // 2, 2 * rpt - C_KH)[case]
    return lo <= t < lo + C_KH


def _nbr_fill_bias(tz_ref, bias_ref):
    rpt = TILE // GRID_W
    lane_lo = lax.broadcasted_iota(jnp.int32, (GRID_W, LANES), 1) < GRID_W
    neg = jnp.full((GRID_W, LANES), NEG, jnp.float32)
    for hh in range(2 * NBR_PAIRS):
        bias_ref[3, hh] = jnp.full((3 * TILE, TILE), NEG, jnp.float32)
        for case in range(3):
            for t in range(3 * rpt):
                for jp in range(rpt // 2):
                    ok0 = _nbr_row_valid(case, 2 * jp, t)
                    ok1 = _nbr_row_valid(case, 2 * jp + 1, t)
                    if ok0 or ok1:
                        blk = tz_ref[hh, t - rpt - 2 * jp + C_KH - 2]
                        if not ok1:
                            blk = jnp.where(lane_lo, blk, NEG)
                        elif not ok0:
                            blk = jnp.where(lane_lo, NEG, blk)
                    else:
                        blk = neg
                    bias_ref[case, hh, t * GRID_W:(t + 1) * GRID_W, jp * LANES:(jp + 1) * LANES] = blk


def _nbr_body(q_ref, kc_ref, k0_ref, k1_ref, k2_ref, vc_ref, v0_ref, v1_ref, v2_ref, tz_ref, o_ref,
              bias_ref, s_ref, p_ref):
    i = pl.program_id(2)
    nt = pl.num_programs(2)

    @pl.when(i == 0)
    def _():
        _nbr_fill_bias(tz_ref, bias_ref)

    case = jnp.where(i == 0, 3, jnp.where(i == 1, 0, jnp.where(i == nt - 1, 2, 1)))
    n_ctx = kc_ref.shape[0]
    n_keys = n_ctx + 3 * TILE
    ones = jnp.ones((16, n_keys), jnp.bfloat16)
    zero = jnp.zeros((), jnp.bfloat16)
    blk = LANES
    masks = _lane_half_masks(TILE, None)
    m = []
    for pair in range(NBR_PAIRS):
        lanes = slice(pair * LANES, (pair + 1) * LANES)
        k_ctx = kc_ref[:, lanes]
        k_loc = jnp.concatenate([k0_ref[:, lanes], k1_ref[:, lanes], k2_ref[:, lanes]], axis=0)
        q2 = q_ref[:, lanes]
        for hh in range(2):
            hd = 2 * pair + hh
            qm = jnp.where(masks[hh], q2, zero)
            s_ctx = _dot_t(k_ctx, qm)
            s_ref[hd, :n_ctx, :] = s_ctx
            s_loc = _dot_t(k_loc, qm) + bias_ref[case, hd]
            s_ref[hd, n_ctx:, :] = s_loc
            m.append(jnp.maximum(jnp.max(s_ctx, axis=0, keepdims=True), jnp.max(s_loc, axis=0, keepdims=True)))
    for pair in range(NBR_PAIRS):
        lanes = slice(pair * LANES, (pair + 1) * LANES)
        vt = jnp.concatenate([r[lanes, :] for r in (vc_ref, v0_ref, v1_ref, v2_ref)], axis=1)
        outs = []
        for hh in range(2):
            hd = 2 * pair + hh
            for c in range(n_keys // blk):
                p = jnp.exp2(s_ref[hd, c * blk:(c + 1) * blk, :] - m[hd])
                p_ref[hd, c * blk:(c + 1) * blk, :] = p.astype(jnp.bfloat16)
            vt_h = jnp.concatenate([vt[hh * HEAD_DIM:(hh + 1) * HEAD_DIM], ones], axis=0)
            acc = _dot(vt_h, p_ref[hd])
            outs.append(acc[:HEAD_DIM] * (1.0 / acc[HEAD_DIM:HEAD_DIM + 1]))
        o_ref[:, lanes] = jnp.concatenate(outs, axis=0).T.astype(jnp.bfloat16)


def _nbr_attn(q, k, vt, tz, ctx_len):
    n_batch, t_len, d = q.shape
    nt = t_len // TILE
    assert ctx_len == TILE and TILE % GRID_W == 0 and C_KH == 2 * (TILE // GRID_W)
    assert nt >= 4, "needs distinct first / interior / last latent tiles"
    tiles = [lambda i: 0, lambda i: jnp.clip(i - 1, 1, nt - 1), lambda i: i, lambda i: jnp.clip(i + 1, 1, nt - 1)]
    width = NBR_PAIRS * LANES
    heads = 2 * NBR_PAIRS
    blk = lambda f: pl.BlockSpec((None, TILE, width), lambda b, p, i: (b, f(i), p))
    blk_t = lambda f: pl.BlockSpec((None, None, width, TILE), lambda b, p, i: (b, f(i), p, 0))
    tz_spec = pl.BlockSpec((heads,) + tz.shape[1:], lambda b, p, i: (p, 0, 0, 0))
    n_keys = ctx_len + 3 * TILE
    return pl.pallas_call(
        _nbr_body,
        grid=(n_batch, C_HEADS // heads, nt),
        in_specs=[blk(lambda i: i)] + [blk(f) for f in tiles] + [blk_t(f) for f in tiles] + [tz_spec],
        out_specs=blk(lambda i: i),
        out_shape=jax.ShapeDtypeStruct((n_batch, t_len, d), jnp.bfloat16),
        scratch_shapes=[pltpu.VMEM((4, heads, 3 * TILE, TILE), jnp.float32),
                        pltpu.VMEM((heads, n_keys, TILE), jnp.float32),
                        pltpu.VMEM((heads, n_keys, TILE), jnp.bfloat16)],
        compiler_params=_cparams(("arbitrary", "arbitrary", "arbitrary")),
        name="nbr_attn",
    )(q, k, k, k, k, vt, vt, vt, vt, tz)


def _nbr_toeplitz(rpb):
    n_off = 2 * C_KW - 1
    pad = GRID_W - C_KW
    ext = jnp.pad(rpb, ((0, 0), (0, 0), (pad, pad)), mode="edge")
    assert ext.shape[-1] == n_off + 2 * pad == 2 * GRID_W - 1
    tz = jnp.stack([ext[:, :, GRID_W - 1 - qc:2 * GRID_W - 1 - qc] for qc in range(GRID_W)], axis=-1)
    qc = np.arange(GRID_W)[None, :]
    kc = np.arange(GRID_W)[:, None]
    cs = np.clip(qc - C_KW // 2, 0, GRID_W - C_KW)
    col_ok = jnp.asarray((kc >= cs) & (kc < cs + C_KW))
    tz = jnp.where(col_ok, tz * LOG2E, NEG)
    return jnp.concatenate([tz[:, 1:], tz[:, :-1]], axis=-1)


def _out_body(h_ref, mod_ref, o1_ref, o2_ref, gate_ref, w_ref, *rest, final):
    half = D_MODEL // 2
    g = gate_ref[...].astype(jnp.float32)
    og1 = (o1_ref[...].astype(jnp.float32) * g[:, :half]).astype(jnp.bfloat16)
    og2 = (o2_ref[...].astype(jnp.float32) * g[:, half:]).astype(jnp.bfloat16)
    y = _dot(og1, w_ref[:half, :]) + _dot(og2, w_ref[half:, :])
    h_new = h_ref[...] + mod_ref[:, 2 * D_MODEL:] * y
    if final:
        fg_ref, out_ref = rest
        out_ref[...] = _rms(h_new, fg_ref[...])
    else:
        rest[0][...] = h_new


def _out_proj(h, mods_l, o1, o2, gate, w, o2_block, final_g=None):
    n_batch, t_len, d = h.shape
    nt = t_len // TILE
    final = final_g is not None
    off = 1 if final else 0
    row = lambda wd, blk=0: pl.BlockSpec((None, TILE, wd), lambda b, i: (b, i + off, blk))
    mod = pl.BlockSpec((None, 1, 3 * D_MODEL), lambda b, i: (jnp.where(i + off == 0, n_batch, b), 0, 0))
    full = lambda a: pl.BlockSpec(a.shape, lambda b, i: (0,) * a.ndim)
    in_specs = [row(d), mod, row(d // 2), row(d // 2, o2_block), row(d), full(w)]
    args = [h, mods_l, o1, o2, gate, w]
    if final:
        in_specs.append(full(final_g))
        args.append(final_g)
        out_spec = pl.BlockSpec((None, TILE, d), lambda b, i: (b, i, 0))
        out_shape = jax.ShapeDtypeStruct((n_batch, t_len - TILE, d), jnp.float32)
        aliases = {}
    else:
        out_spec = row(d)
        out_shape = jax.ShapeDtypeStruct(h.shape, jnp.float32)
        aliases = {0: 0}
    return pl.pallas_call(
        functools.partial(_out_body, final=final),
        grid=(n_batch, nt - off),
        in_specs=in_specs,
        out_specs=out_spec,
        out_shape=out_shape,
        input_output_aliases=aliases,
        compiler_params=_cparams(("parallel", "arbitrary")),
        name="out_proj_final" if final else "out_proj",
    )(*args)


def _deinterleave(n):
    return np.concatenate([np.arange(0, n, 2), np.arange(1, n, 2)])


def _take_cols(w, cols):
    cols = np.asarray(cols)
    picked = jnp.take(w, jnp.asarray(np.maximum(cols, 0)), axis=1)
    return jnp.where(jnp.asarray(cols >= 0)[None, :], picked, 0.0)


def _even_layout():
    perm = _deinterleave(HEAD_DIM)
    qa0, ka0 = 0, A_HEADS * HEAD_DIM
    va0 = ka0 + A_KV_HEADS * HEAD_DIM
    za0 = va0 + A_KV_HEADS * HEAD_DIM
    cq0 = za0 + A_HEADS * HEAD_DIM
    ckv0 = cq0 + B_Q_LORA
    kr0 = ckv0 + B_KV_LORA
    zb0 = kr0 + B_ROPE
    cols = []
    cols += [qa0 + hd * HEAD_DIM + perm for hd in _A_SLOTS]
    cols += [ka0 + kv * HEAD_DIM + perm for kv in range(A_KV_HEADS)]
    cols += [va0 + np.arange(A_KV_HEADS * HEAD_DIM)]
    cols += [za0 + hd * HEAD_DIM + np.arange(HEAD_DIM) for hd in _A_SLOTS]
    cols += [zb0 + np.arange(B_HEADS * B_V)]
    cols += [cq0 + np.arange(B_Q_LORA), ckv0 + np.arange(B_KV_LORA)]
    cols += [np.full(B_NOPE, -1), kr0 + _deinterleave(B_ROPE), np.full(LANES - B_NOPE - B_ROPE, -1)]
    in_cols = np.concatenate(cols)
    assert in_cols.shape[0] == _AB_COLS

    uq = []
    for hd in range(B_HEADS):
        base = hd * (B_NOPE + B_ROPE)
        uq += [base + np.arange(B_NOPE), base + B_NOPE + _deinterleave(B_ROPE),
               np.full(LANES - B_NOPE - B_ROPE, -1)]
    ukv_k, ukv_v = [], []
    pad = np.full(LANES - B_NOPE, -1)
    for hd in range(B_HEADS):
        base = hd * (B_NOPE + B_V)
        ukv_k += [base + np.arange(B_NOPE), pad]
        ukv_v += [base + B_NOPE + np.arange(B_V)]
    out_rows = np.concatenate([hd * HEAD_DIM + np.arange(HEAD_DIM) for hd in _A_SLOTS]
                              + [A_HEADS * HEAD_DIM + np.arange(B_HEADS * B_V)])
    return in_cols, np.concatenate(uq), np.concatenate(ukv_k), np.concatenate(ukv_v), out_rows


def _rope_tables(seq_len, ctx_len):
    t = jnp.arange(seq_len, dtype=jnp.int32)
    row = (t // GRID_W).astype(jnp.float32)
    col = (t % GRID_W).astype(jnp.float32)

    def angles(rot_dim):
        ppa = rot_dim // 4
        inv = ROPE_THETA ** (-jnp.arange(ppa, dtype=jnp.float32) / ppa)
        ang = jnp.concatenate([row[:, None] * inv, col[:, None] * inv], axis=-1)
        return jnp.cos(ang), jnp.sin(ang)

    def with_ctx(c, s1, s2):
        pad = lambda a, v: jnp.concatenate([jnp.full((ctx_len, LANES), v, jnp.float32), a], axis=0)
        return pad(c, 1.0), pad(s1, 0.0), pad(s2, 0.0)

    cos_a, sin_a = angles(HEAD_DIM)
    z32 = jnp.zeros_like(sin_a)
    tabs_a = with_ctx(jnp.tile(cos_a, (1, 4)), jnp.tile(jnp.concatenate([-sin_a, z32], 1), (1, 2)),
                      jnp.tile(jnp.concatenate([z32, sin_a], 1), (1, 2)))
    cos_b, sin_b = angles(B_ROPE)
    one64 = jnp.ones((seq_len, B_NOPE), jnp.float32)
    z64 = jnp.zeros((seq_len, B_NOPE), jnp.float32)
    z16 = jnp.zeros_like(sin_b)
    one32 = jnp.ones((seq_len, LANES - B_NOPE - B_ROPE), jnp.float32)
    z32b = jnp.zeros((seq_len, LANES - B_NOPE - B_ROPE), jnp.float32)
    tabs_b = with_ctx(jnp.concatenate([one64, cos_b, cos_b, one32], 1),
                      jnp.concatenate([z64, -sin_b, z16, z32b], 1),
                      jnp.concatenate([z64, z16, sin_b, z32b], 1))
    return tabs_a, tabs_b


def kernel(x, c, ctx, c_ctx, ada_w, ada_b, norm_g, ab_in_w, ab_out_w, a_sink, b_q_norm_g, b_w_uq,
           b_kv_norm_g, b_w_ukv, c_in_w, c_out_w, c_rpb, final_g):
    n_batch, seq_len, d = x.shape
    ctx_len = ctx.shape[1]
    depth = ada_w.shape[0]
    assert d == D_MODEL and ctx_len == TILE and seq_len % MLA_CHUNK == 0 and seq_len % GRID_W == 0
    assert depth % 2 == 0, "the final norm is fused into the last (odd) layer's output projection"
    bf = jnp.bfloat16

    mod_rows = -(-(n_batch + 1) // 8) * 8
    cc = jnp.concatenate([c, c_ctx[None, :], jnp.zeros((mod_rows - n_batch - 1, d), jnp.float32)], axis=0)
    mods = _mods(cc, ada_w, ada_b).reshape(depth, mod_rows, 1, 3 * d)

    in_cols, uq_cols, uk_cols, uv_cols, out_rows = _even_layout()
    tabs_a, tabs_b = _rope_tables(seq_len, ctx_len)
    h = jnp.concatenate([ctx, x], axis=1)
    row2 = lambda v: v.reshape(1, -1)

    out = None
    for layer in range(depth):
        i = layer // 2
        last = layer == depth - 1
        g = row2(norm_g[layer])
        if layer % 2 == 0:
            w_in = _take_cols(ab_in_w[i], in_cols).astype(bf)
            wuq = _take_cols(b_w_uq[i], uq_cols).astype(bf)
            wuk = _take_cols(b_w_ukv[i], uk_cols).astype(bf)
            wvt = _take_cols(b_w_ukv[i], uv_cols).T.astype(bf)
            w_out = jnp.take(ab_out_w[i], jnp.asarray(out_rows), axis=0).astype(bf)
            sink = jnp.take(a_sink[i], jnp.asarray(_A_SLOTS))
            qa, ka, va, gate, qb, kb, vbt = _proj_even(h, mods[layer], g, w_in, row2(b_q_norm_g[i]), wuq,
                                                       row2(b_kv_norm_g[i]), wuk, wvt, tabs_a, tabs_b)
            o1 = _win_attn(qa, ka, va, sink, ctx_len)
            o2 = _mla_attn(qb, kb, vbt, ctx_len)
            o2_block = 0
        else:
            w = c_in_w[i]
            w_qkz = jnp.concatenate([w[:, :2 * d], w[:, 3 * d:]], axis=1).astype(bf)
            q, k, vt, gate = _proj_odd(h, mods[layer], g, w_qkz, w[:, 2 * d:3 * d].T.astype(bf))
            o1 = o2 = _nbr_attn(q, k, vt, _nbr_toeplitz(c_rpb[i]), ctx_len)
            o2_block = 1
            w_out = c_out_w[i].astype(bf)
        if last:
            out = _out_proj(h, mods[layer], o1, o2, gate, w_out, o2_block, final_g=row2(final_g))
        else:
            h = _out_proj(h, mods[layer], o1, o2, gate, w_out, o2_block)
    return out
```

```python
import functools
import math

import numpy as np
import jax
import jax.numpy as jnp
from jax import lax
from jax.experimental import pallas as pl
from jax.experimental.pallas import tpu as pltpu

D_MODEL = 1024
HEAD_DIM = 64
GRID_W = 64
ROPE_THETA = 10000.0
NORM_EPS = 1e-6
A_HEADS = 8
A_KV_HEADS = 2
A_WINDOW = 128
B_HEADS = 8
B_Q_LORA = 384
B_KV_LORA = 256
B_NOPE = 64
B_ROPE = 32
B_V = 64
C_HEADS = 16
C_KH = 8
C_KW = 16

LANES = 128
TILE = 256
MLA_CHUNK = 1024
LOG2E = math.log2(math.e)
NEG = -1e30
VMEM_LIMIT = 48 * 1024 * 1024

_QA, _KA, _ZA, _ZB, _CQ, _CKV, _KR, _AB_COLS = 0, 512, 640, 1152, 1664, 2048, 2304, 2432
_A_SLOTS = (0, 4, 1, 5, 2, 6, 3, 7)


def _cparams(sem):
    return pltpu.CompilerParams(dimension_semantics=sem, vmem_limit_bytes=VMEM_LIMIT)


def _silu(x):
    return x * (1.0 / (1.0 + jnp.exp(-x)))


def _dot(a, b):
    return jnp.dot(a, b, preferred_element_type=jnp.float32)


def _dot_t(a, b):
    return lax.dot_general(a, b, (((1,), (1,)), ((), ())), preferred_element_type=jnp.float32)


def _mods_body(c_ref, w_ref, b_ref, o_ref):
    sc = _silu(c_ref[...])
    o_ref[...] = jnp.dot(sc, w_ref[...], preferred_element_type=jnp.float32,
                         precision=lax.Precision.HIGHEST) + b_ref[...]


def _mods(cc, ada_w, ada_b):
    depth, d, d3 = ada_w.shape
    rows = cc.shape[0]
    tn = 512
    return pl.pallas_call(
        _mods_body,
        grid=(depth, d3 // tn),
        in_specs=[
            pl.BlockSpec((rows, d), lambda l, j: (0, 0)),
            pl.BlockSpec((None, d, tn), lambda l, j: (l, 0, j)),
            pl.BlockSpec((None, 1, tn), lambda l, j: (l, 0, j)),
        ],
        out_specs=pl.BlockSpec((None, rows, tn), lambda l, j: (l, 0, j)),
        out_shape=jax.ShapeDtypeStruct((depth, rows, d3), jnp.float32),
        compiler_params=_cparams(("arbitrary", "arbitrary")),
        name="mods",
    )(cc, ada_w, ada_b.reshape(depth, 1, d3))


def _mod_spec(n_batch):
    return pl.BlockSpec((None, 1, 3 * D_MODEL), lambda b, i: (jnp.where(i == 0, n_batch, b), 0, 0))


def _norm_mod(h, g, mod):
    y = h * lax.rsqrt(jnp.mean(h * h, axis=-1, keepdims=True) + NORM_EPS) * g
    return y * (1.0 + mod[:, D_MODEL:2 * D_MODEL]) + mod[:, :D_MODEL]


def _rms(x, g):
    return x * lax.rsqrt(jnp.mean(x * x, axis=-1, keepdims=True) + NORM_EPS) * g


def _rope(x, c, s1, s2, half):
    return x * c + pltpu.roll(x, LANES - half, 1) * s1 + pltpu.roll(x, half, 1) * s2


def _proj_even_body(h_ref, mod_ref, g_ref, w_ref, wvat_ref, qg_ref, wuq_ref, kvg_ref, wukv_ref, wvt_ref,
                    ca_ref, sa1_ref, sa2_ref, cb_ref, sb1_ref, sb2_ref,
                    qa_ref, ka_ref, vat_ref, gate_ref, qb_ref, kb_ref, vbt_ref):
    u = _norm_mod(h_ref[...], g_ref[...], mod_ref[...]).astype(jnp.bfloat16)
    ca, sa1, sa2 = ca_ref[...], sa1_ref[...], sa2_ref[...]
    cb, sb1, sb2 = cb_ref[...], sb1_ref[...], sb2_ref[...]

    qk = _dot(u, w_ref[:, _QA:_ZA])
    a_scale = HEAD_DIM ** -0.5 * LOG2E
    for j in range(4):
        blk = _rope(qk[:, j * LANES:(j + 1) * LANES], ca, sa1, sa2, HEAD_DIM // 2)
        qa_ref[:, j * LANES:(j + 1) * LANES] = (blk * a_scale).astype(jnp.bfloat16)
    ka_ref[...] = _rope(qk[:, _KA:_ZA], ca, sa1, sa2, HEAD_DIM // 2).astype(jnp.bfloat16)
    vat = _dot_t(wvat_ref[...], u).astype(jnp.bfloat16)
    for j in range(TILE // LANES):
        vat_ref[j] = vat[:, j * LANES:(j + 1) * LANES]
    gate_ref[...] = _silu(_dot(u, w_ref[:, _ZA:_CQ])).astype(jnp.bfloat16)

    lat = _dot(u, w_ref[:, _CQ:_AB_COLS])
    cq = _rms(lat[:, :B_Q_LORA], qg_ref[...]).astype(jnp.bfloat16)
    ckv = _rms(lat[:, B_Q_LORA:B_Q_LORA + B_KV_LORA], kvg_ref[...]).astype(jnp.bfloat16)
    kr = _rope(lat[:, B_Q_LORA + B_KV_LORA:], cb, sb1, sb2, B_ROPE // 2)

    qb = _dot(cq, wuq_ref[...])
    b_scale = (B_NOPE + B_ROPE) ** -0.5 * LOG2E
    kb = _dot(ckv, wukv_ref[...])
    for hd in range(B_HEADS):
        sl = slice(hd * LANES, (hd + 1) * LANES)
        qb_ref[:, sl] = (_rope(qb[:, sl], cb, sb1, sb2, B_ROPE // 2) * b_scale).astype(jnp.bfloat16)
        kb_ref[:, sl] = (kb[:, sl] + kr).astype(jnp.bfloat16)
    vbt_ref[...] = _dot_t(wvt_ref[...], ckv).astype(jnp.bfloat16)


def _proj_even(h, mods_l, g, w_in, wvat, qg, wuq, kvg, wukv, wvt, tabs_a, tabs_b):
    n_batch, t_len, d = h.shape
    nt = t_len // TILE
    row = lambda w: pl.BlockSpec((None, TILE, w), lambda b, i: (b, i, 0))
    full = lambda a: pl.BlockSpec(a.shape, lambda b, i: (0,) * a.ndim)
    tab = pl.BlockSpec((TILE, LANES), lambda b, i: (i, 0))
    bf = jnp.bfloat16
    sds = lambda *shape: jax.ShapeDtypeStruct((n_batch,) + shape, bf)
    n_vt = B_HEADS * B_V
    per_tile = TILE // LANES
    return pl.pallas_call(
        _proj_even_body,
        grid=(n_batch, nt),
        in_specs=[row(d), _mod_spec(n_batch), full(g), full(w_in), full(wvat), full(qg), full(wuq), full(kvg),
                  full(wukv), full(wvt)] + [tab] * 6,
        out_specs=[row(512), row(LANES),
                   pl.BlockSpec((None, per_tile, LANES, LANES), lambda b, i: (b, i, 0, 0)),
                   row(d), row(d), row(d),
                   pl.BlockSpec((None, None, n_vt, TILE), lambda b, i: (b, i, 0, 0))],
        out_shape=[sds(t_len, 512), sds(t_len, LANES), sds(t_len // LANES, LANES, LANES),
                   sds(t_len, d), sds(t_len, d), sds(t_len, d), sds(nt, n_vt, TILE)],
        compiler_params=_cparams(("parallel", "arbitrary")),
        name="proj_even",
    )(h, mods_l, g, w_in, wvat, qg, wuq, kvg, wukv, wvt, *tabs_a, *tabs_b)


def _proj_odd_body(h_ref, mod_ref, g_ref, w_ref, wvt_ref, q_ref, k_ref, vt_ref, gate_ref):
    u = _norm_mod(h_ref[...], g_ref[...], mod_ref[...]).astype(jnp.bfloat16)
    d = D_MODEL
    q_ref[...] = (_dot(u, w_ref[:, :d]) * (HEAD_DIM ** -0.5 * LOG2E)).astype(jnp.bfloat16)
    k_ref[...] = _dot(u, w_ref[:, d:2 * d]).astype(jnp.bfloat16)
    vt_ref[...] = _dot_t(wvt_ref[...], u).astype(jnp.bfloat16)
    gate_ref[...] = _silu(_dot(u, w_ref[:, 2 * d:])).astype(jnp.bfloat16)


def _proj_odd(h, mods_l, g, w_qkz, wvt):
    n_batch, t_len, d = h.shape
    nt = t_len // TILE
    row = pl.BlockSpec((None, TILE, d), lambda b, i: (b, i, 0))
    row_t = pl.BlockSpec((None, None, d, TILE), lambda b, i: (b, i, 0, 0))
    full = lambda a: pl.BlockSpec(a.shape, lambda b, i: (0,) * a.ndim)
    out = jax.ShapeDtypeStruct((n_batch, t_len, d), jnp.bfloat16)
    return pl.pallas_call(
        _proj_odd_body,
        grid=(n_batch, nt),
        in_specs=[row, _mod_spec(n_batch), full(g), full(w_qkz), full(wvt)],
        out_specs=[row, row, row_t, row],
        out_shape=[out, out, jax.ShapeDtypeStruct((n_batch, nt, d, TILE), jnp.bfloat16), out],
        compiler_params=_cparams(("parallel", "arbitrary")),
        name="proj_odd",
    )(h, mods_l, g, w_qkz, wvt)


def _lane_half_masks(rows, dtype):
    lane = lax.broadcasted_iota(jnp.int32, (rows, LANES), 1)
    lo = lane < HEAD_DIM
    return lo, jnp.logical_not(lo)


def _win_fill_mask(mask_ref):
    r = lax.broadcasted_iota(jnp.int32, (2 * TILE, TILE), 0)
    c = lax.broadcasted_iota(jnp.int32, (2 * TILE, TILE), 1)
    for case, (back, first) in enumerate(((A_WINDOW, A_WINDOW), (A_WINDOW, 0), (2 * A_WINDOW, 0))):
        ok = (jnp.abs(r - back - c) <= A_WINDOW) & (r >= first)
        mask_ref[case] = jnp.where(ok, 0.0, NEG)
    mask_ref[3] = jnp.full((2 * TILE, TILE), NEG, jnp.float32)


def _win_body(sink_ref, q_ref, k_ref, vt_ref, o_ref, mask_ref, sa_ref, sb_ref, p_ref, *, nt):
    assert TILE == 2 * A_WINDOW
    pair = pl.program_id(1)
    _win_fill_mask(mask_ref)
    n_keys = 3 * TILE
    n128 = nt * (TILE // LANES)
    ones = jnp.ones((16, n_keys), jnp.bfloat16)
    zero = jnp.zeros((), jnp.bfloat16)
    blk = LANES
    masks = _lane_half_masks(TILE, None)
    sinks = tuple(sink_ref[2 * pair + hh] * LOG2E for hh in range(2))

    def window(i):
        return jnp.clip(2 * i - 1, 0, n128 - 4)

    def scores(i, s_ref):
        case = jnp.where(i == 0, 3, jnp.where(i == 1, 0, jnp.where(i == nt - 1, 2, 1)))
        k_ctx = k_ref[:TILE, :]
        k_loc = k_ref[pl.ds(pl.multiple_of(window(i) * LANES, LANES), 2 * TILE), :]
        q2 = q_ref[pl.ds(pl.multiple_of(i * TILE, TILE), TILE), :]
        m = []
        for hh in range(2):
            qm = jnp.where(masks[hh], q2, zero)
            s_ctx = _dot_t(k_ctx, qm)
            s_ref[hh, :TILE, :] = s_ctx
            s_loc = _dot_t(k_loc, qm) + mask_ref[case]
            s_ref[hh, TILE:, :] = s_loc
            m.append(jnp.maximum(jnp.maximum(jnp.max(s_ctx, axis=0, keepdims=True),
                                             jnp.max(s_loc, axis=0, keepdims=True)), sinks[hh]))
        return tuple(m)

    def update(i, s_ref, m):
        w0 = window(i)
        vt = jnp.concatenate([vt_ref[0], vt_ref[1]] + [vt_ref[w0 + j] for j in range(4)], axis=1)
        outs = []
        for hh in range(2):
            for c in range(n_keys // blk):
                p = jnp.exp2(s_ref[hh, c * blk:(c + 1) * blk, :] - m[hh])
                p_ref[hh, c * blk:(c + 1) * blk, :] = p.astype(jnp.bfloat16)
            vt_h = jnp.concatenate([vt[hh * HEAD_DIM:(hh + 1) * HEAD_DIM], ones], axis=0)
            acc = _dot(vt_h, p_ref[hh])
            denom = acc[HEAD_DIM:HEAD_DIM + 1] + jnp.exp2(sinks[hh] - m[hh])
            outs.append(acc[:HEAD_DIM] * (1.0 / denom))
        o_ref[pl.ds(pl.multiple_of(i * TILE, TILE), TILE), :] = jnp.concatenate(outs, axis=0).T.astype(jnp.bfloat16)

    def body(t, m_a):
        m_b = scores(2 * t + 1, sb_ref)
        update(2 * t, sa_ref, m_a)
        m_a = scores(2 * t + 2, sa_ref)
        update(2 * t + 1, sb_ref, m_b)
        return m_a

    m_a = lax.fori_loop(0, (nt - 1) // 2, body, scores(0, sa_ref))
    update(nt - 1, sa_ref, m_a)


def _win_attn(qa, ka, vat, sink, ctx_len):
    n_batch, t_len, _ = qa.shape
    nt = t_len // TILE
    assert ctx_len == TILE and nt >= 4 and nt % 2 == 1
    return pl.pallas_call(
        functools.partial(_win_body, nt=nt),
        grid=(n_batch, A_HEADS // 2),
        in_specs=[pl.BlockSpec(memory_space=pltpu.SMEM),
                  pl.BlockSpec((None, t_len, LANES), lambda b, p: (b, 0, p)),
                  pl.BlockSpec((None, t_len, LANES), lambda b, p: (b, 0, 0)),
                  pl.BlockSpec((None,) + vat.shape[1:], lambda b, p: (b, 0, 0, 0))],
        out_specs=pl.BlockSpec((None, t_len, LANES), lambda b, p: (b, 0, p)),
        out_shape=jax.ShapeDtypeStruct((n_batch, t_len, A_HEADS * HEAD_DIM), jnp.bfloat16),
        scratch_shapes=[pltpu.VMEM((4, 2 * TILE, TILE), jnp.float32),
                        pltpu.VMEM((2, 3 * TILE, TILE), jnp.float32),
                        pltpu.VMEM((2, 3 * TILE, TILE), jnp.float32),
                        pltpu.VMEM((2, 3 * TILE, TILE), jnp.bfloat16)],
        compiler_params=_cparams(("parallel", "parallel")),
        name="win_attn",
    )(sink, qa, ka, vat)


def _mla_body(q_ref, k_ref, vt_ref, o_ref, sa_ref, sb_ref, p_ref, *, ctx_len, n_chunks):
    i = pl.program_id(2)
    q = (q_ref[:, :LANES], q_ref[:, LANES:])
    sub = MLA_CHUNK // TILE
    blk = LANES

    def scores(chunk, s_ref):
        start, size = (0, ctx_len) if chunk is None else (pl.multiple_of(ctx_len + chunk * MLA_CHUNK, TILE), MLA_CHUNK)
        cm = []
        for hh in range(2):
            s = _dot_t(k_ref[pl.ds(start, size), hh * LANES:(hh + 1) * LANES], q[hh])
            s_ref[hh, :size, :] = s
            cm.append(jnp.max(s, axis=0, keepdims=True))
        return tuple(cm)

    def update(state, cm, s_ref, chunk):
        tile0, n_tiles = (0, 1) if chunk is None else (1 + chunk * sub, sub)
        size = n_tiles * TILE
        new = []
        for hh in range(2):
            m_prev, acc = state[hh]
            m_new = jnp.maximum(m_prev, cm[hh])
            alpha = jnp.exp2(m_prev - m_new)
            for c in range(size // blk):
                p = jnp.exp2(s_ref[hh, c * blk:(c + 1) * blk, :] - m_new)
                p_ref[hh, c * blk:(c + 1) * blk, :] = p.astype(jnp.bfloat16)
            vt = jnp.concatenate([vt_ref[tile0 + c, hh * B_V:(hh + 1) * B_V, :] for c in range(n_tiles)], axis=1)
            vt = jnp.concatenate([vt, jnp.ones((ones_rows, size), jnp.bfloat16)], axis=0)
            new.append((m_new, acc * alpha + _dot(vt, p_ref[hh, :size, :])))
        return tuple(new)

    def finish(state):
        o_t = jnp.concatenate([acc[:B_V] * (1.0 / acc[B_V:B_V + 1]) for _, acc in state], axis=0)
        o_ref[...] = o_t.T.astype(jnp.bfloat16)

    ones_rows = 16
    init = tuple((jnp.full((1, TILE), NEG, jnp.float32), jnp.zeros((B_V + ones_rows, TILE), jnp.float32))
                 for _ in range(2))

    @pl.when(i == 0)
    def _():
        finish(update(init, scores(None, sa_ref), sa_ref, None))

    @pl.when(i > 0)
    def _():
        state = update(init, scores(None, sa_ref), sa_ref, None)
        cm_a = scores(0, sa_ref)

        def body(t, carry):
            state, cm_a = carry
            cm_b = scores(2 * t + 1, sb_ref)
            state = update(state, cm_a, sa_ref, 2 * t)
            cm_a = scores(2 * t + 2, sa_ref)
            state = update(state, cm_b, sb_ref, 2 * t + 1)
            return state, cm_a

        state, cm_a = lax.fori_loop(0, n_chunks // 2 - 1, body, (state, cm_a))
        cm_b = scores(n_chunks - 1, sb_ref)
        state = update(state, cm_a, sa_ref, n_chunks - 2)
        finish(update(state, cm_b, sb_ref, n_chunks - 1))


def _mla_attn(qb, kb, vbt, ctx_len):
    n_batch, t_len, _ = qb.shape
    nt = t_len // TILE
    seq = t_len - ctx_len
    assert ctx_len == TILE and seq % (2 * MLA_CHUNK) == 0, "the chunk loop handles two chunks per trip"
    body = functools.partial(_mla_body, ctx_len=ctx_len, n_chunks=seq // MLA_CHUNK)
    s_buf = pltpu.VMEM((2, MLA_CHUNK, TILE), jnp.float32)
    return pl.pallas_call(
        body,
        grid=(n_batch, B_HEADS // 2, nt),
        in_specs=[pl.BlockSpec((None, TILE, 2 * LANES), lambda b, p, i: (b, i, p)),
                  pl.BlockSpec((None, t_len, 2 * LANES), lambda b, p, i: (b, 0, p)),
                  pl.BlockSpec((None, nt, 2 * B_V, TILE), lambda b, p, i: (b, 0, p, 0))],
        out_specs=pl.BlockSpec((None, TILE, LANES), lambda b, p, i: (b, i, p)),
        out_shape=jax.ShapeDtypeStruct((n_batch, t_len, B_HEADS * B_V), jnp.bfloat16),
        scratch_shapes=[s_buf, s_buf, pltpu.VMEM((2, MLA_CHUNK, TILE), jnp.bfloat16)],
        compiler_params=_cparams(("parallel", "parallel", "arbitrary")),
        name="mla_attn",
    )(qb, kb, vbt)


def _nbr_row_valid(case, jr, t):
    rpt = TILE // GRID_W
    lo = (rpt, jr + rpt - C_KH // 2, 2 * rpt - C_KH)[case]
    return lo <= t < lo + C_KH


def _nbr_fill_bias(tz_ref, bias_ref):
    rpt = TILE // GRID_W
    lane_lo = lax.broadcasted_iota(jnp.int32, (GRID_W, LANES), 1) < GRID_W
    neg = jnp.full((GRID_W, LANES), NEG, jnp.float32)
    for hh in range(2):
        bias_ref[3, hh] = jnp.full((3 * TILE, TILE), NEG, jnp.float32)
        for case in range(3):
            for t in range(3 * rpt):
                for jp in range(rpt // 2):
                    ok0 = _nbr_row_valid(case, 2 * jp, t)
                    ok1 = _nbr_row_valid(case, 2 * jp + 1, t)
                    if ok0 or ok1:
                        blk = tz_ref[hh, t - rpt - 2 * jp + C_KH - 2]
                        if not ok1:
                            blk = jnp.where(lane_lo, blk, NEG)
                        elif not ok0:
                            blk = jnp.where(lane_lo, NEG, blk)
                    else:
                        blk = neg
                    bias_ref[case, hh, t * GRID_W:(t + 1) * GRID_W, jp * LANES:(jp + 1) * LANES] = blk


def _nbr_body(q_ref, k_ref, vt_ref, tz_ref, o_ref, bias_ref, sa_ref, sb_ref, p_ref, *, nt):
    _nbr_fill_bias(tz_ref, bias_ref)
    n_keys = 4 * TILE
    ones = jnp.ones((16, n_keys), jnp.bfloat16)
    zero = jnp.zeros((), jnp.bfloat16)
    blk = LANES
    masks = _lane_half_masks(TILE, None)

    def key_tiles(i):
        return 0, jnp.clip(i - 1, 1, nt - 1), i, jnp.clip(i + 1, 1, nt - 1)

    def rows(t):
        return pl.ds(pl.multiple_of(t * TILE, TILE), TILE)

    def scores(i, s_ref):
        case = jnp.where(i == 0, 3, jnp.where(i == 1, 0, jnp.where(i == nt - 1, 2, 1)))
        tc, t0, t1, t2 = key_tiles(i)
        k_ctx = k_ref[rows(tc), :]
        k_loc = jnp.concatenate([k_ref[rows(t0), :], k_ref[rows(t1), :], k_ref[rows(t2), :]], axis=0)
        q2 = q_ref[rows(i), :]
        m = []
        for hh in range(2):
            qm = jnp.where(masks[hh], q2, zero)
            s_ctx = _dot_t(k_ctx, qm)
            s_ref[hh, :TILE, :] = s_ctx
            s_loc = _dot_t(k_loc, qm) + bias_ref[case, hh]
            s_ref[hh, TILE:, :] = s_loc
            m.append(jnp.maximum(jnp.max(s_ctx, axis=0, keepdims=True), jnp.max(s_loc, axis=0, keepdims=True)))
        return tuple(m)

    def update(i, s_ref, m):
        vt = jnp.concatenate([vt_ref[t] for t in key_tiles(i)], axis=1)
        outs = []
        for hh in range(2):
            for c in range(n_keys // blk):
                p = jnp.exp2(s_ref[hh, c * blk:(c + 1) * blk, :] - m[hh])
                p_ref[hh, c * blk:(c + 1) * blk, :] = p.astype(jnp.bfloat16)
            vt_h = jnp.concatenate([vt[hh * HEAD_DIM:(hh + 1) * HEAD_DIM], ones], axis=0)
            acc = _dot(vt_h, p_ref[hh])
            outs.append(acc[:HEAD_DIM] * (1.0 / acc[HEAD_DIM:HEAD_DIM + 1]))
        o_ref[rows(i), :] = jnp.concatenate(outs, axis=0).T.astype(jnp.bfloat16)

    def body(t, m_a):
        m_b = scores(2 * t + 1, sb_ref)
        update(2 * t, sa_ref, m_a)
        m_a = scores(2 * t + 2, sa_ref)
        update(2 * t + 1, sb_ref, m_b)
        return m_a

    m_a = lax.fori_loop(0, (nt - 1) // 2, body, scores(0, sa_ref))
    update(nt - 1, sa_ref, m_a)


def _nbr_attn(q, k, vt, tz, ctx_len):
    n_batch, t_len, d = q.shape
    nt = t_len // TILE
    assert ctx_len == TILE and TILE % GRID_W == 0 and C_KH == 2 * (TILE // GRID_W)
    assert nt >= 4, "needs distinct first / interior / last latent tiles"
    assert nt % 2 == 1, "the tile loop handles two tiles per trip plus a last one"
    seq = pl.BlockSpec((None, t_len, LANES), lambda b, p: (b, 0, p))
    return pl.pallas_call(
        functools.partial(_nbr_body, nt=nt),
        grid=(n_batch, C_HEADS // 2),
        in_specs=[seq, seq, pl.BlockSpec((None, nt, LANES, TILE), lambda b, p: (b, 0, p, 0)),
                  pl.BlockSpec((2,) + tz.shape[1:], lambda b, p: (p, 0, 0, 0))],
        out_specs=seq,
        out_shape=jax.ShapeDtypeStruct((n_batch, t_len, d), jnp.bfloat16),
        scratch_shapes=[pltpu.VMEM((4, 2, 3 * TILE, TILE), jnp.float32),
                        pltpu.VMEM((2, 4 * TILE, TILE), jnp.float32),
                        pltpu.VMEM((2, 4 * TILE, TILE), jnp.float32),
                        pltpu.VMEM((2, 4 * TILE, TILE), jnp.bfloat16)],
        compiler_params=_cparams(("parallel", "parallel")),
        name="nbr_attn",
    )(q, k, vt, tz)


def _nbr_toeplitz(rpb):
    n_off = 2 * C_KW - 1
    pad = GRID_W - C_KW
    ext = jnp.pad(rpb, ((0, 0), (0, 0), (pad, pad)), mode="edge")[:, :, ::-1]
    assert ext.shape[-1] == n_off + 2 * pad == 2 * GRID_W - 1
    tz = jnp.stack([ext[:, :, GRID_W - 1 - kc:2 * GRID_W - 1 - kc] for kc in range(GRID_W)], axis=2)
    qc = np.arange(GRID_W)[None, :]
    kc = np.arange(GRID_W)[:, None]
    cs = np.clip(qc - C_KW // 2, 0, GRID_W - C_KW)
    col_ok = jnp.asarray((kc >= cs) & (kc < cs + C_KW))
    tz = jnp.where(col_ok, tz * LOG2E, NEG)
    return jnp.concatenate([tz[:, 1:], tz[:, :-1]], axis=-1)


def _out_body(h_ref, mod_ref, o1_ref, o2_ref, gate_ref, w_ref, *rest, final):
    half = D_MODEL // 2
    g = gate_ref[...].astype(jnp.float32)
    og1 = (o1_ref[...].astype(jnp.float32) * g[:, :half]).astype(jnp.bfloat16)
    og2 = (o2_ref[...].astype(jnp.float32) * g[:, half:]).astype(jnp.bfloat16)
    y = _dot(og1, w_ref[:half, :]) + _dot(og2, w_ref[half:, :])
    h_new = h_ref[...] + mod_ref[:, 2 * D_MODEL:] * y
    if final:
        fg_ref, out_ref = rest
        out_ref[...] = _rms(h_new, fg_ref[...])
    else:
        rest[0][...] = h_new


def _out_proj(h, mods_l, o1, o2, gate, w, o2_block, final_g=None):
    n_batch, t_len, d = h.shape
    nt = t_len // TILE
    final = final_g is not None
    off = 1 if final else 0
    row = lambda wd, blk=0: pl.BlockSpec((None, TILE, wd), lambda b, i: (b, i + off, blk))
    mod = pl.BlockSpec((None, 1, 3 * D_MODEL), lambda b, i: (jnp.where(i + off == 0, n_batch, b), 0, 0))
    full = lambda a: pl.BlockSpec(a.shape, lambda b, i: (0,) * a.ndim)
    in_specs = [row(d), mod, row(d // 2), row(d // 2, o2_block), row(d), full(w)]
    args = [h, mods_l, o1, o2, gate, w]
    if final:
        in_specs.append(full(final_g))
        args.append(final_g)
        out_spec = pl.BlockSpec((None, TILE, d), lambda b, i: (b, i, 0))
        out_shape = jax.ShapeDtypeStruct((n_batch, t_len - TILE, d), jnp.float32)
        aliases = {}
    else:
        out_spec = row(d)
        out_shape = jax.ShapeDtypeStruct(h.shape, jnp.float32)
        aliases = {0: 0}
    return pl.pallas_call(
        functools.partial(_out_body, final=final),
        grid=(n_batch, nt - off),
        in_specs=in_specs,
        out_specs=out_spec,
        out_shape=out_shape,
        input_output_aliases=aliases,
        compiler_params=_cparams(("parallel", "arbitrary")),
        name="out_proj_final" if final else "out_proj",
    )(*args)


def _deinterleave(n):
    return np.concatenate([np.arange(0, n, 2), np.arange(1, n, 2)])


def _take_cols(w, cols):
    cols = np.asarray(cols)
    picked = jnp.take(w, jnp.asarray(np.maximum(cols, 0)), axis=1)
    return jnp.where(jnp.asarray(cols >= 0)[None, :], picked, 0.0)


def _even_layout():
    perm = _deinterleave(HEAD_DIM)
    qa0, ka0 = 0, A_HEADS * HEAD_DIM
    va0 = ka0 + A_KV_HEADS * HEAD_DIM
    za0 = va0 + A_KV_HEADS * HEAD_DIM
    cq0 = za0 + A_HEADS * HEAD_DIM
    ckv0 = cq0 + B_Q_LORA
    kr0 = ckv0 + B_KV_LORA
    zb0 = kr0 + B_ROPE
    cols = []
    cols += [qa0 + hd * HEAD_DIM + perm for hd in _A_SLOTS]
    cols += [ka0 + kv * HEAD_DIM + perm for kv in range(A_KV_HEADS)]
    va_cols = va0 + np.arange(A_KV_HEADS * HEAD_DIM)
    cols += [za0 + hd * HEAD_DIM + np.arange(HEAD_DIM) for hd in _A_SLOTS]
    cols += [zb0 + np.arange(B_HEADS * B_V)]
    cols += [cq0 + np.arange(B_Q_LORA), ckv0 + np.arange(B_KV_LORA)]
    cols += [np.full(B_NOPE, -1), kr0 + _deinterleave(B_ROPE), np.full(LANES - B_NOPE - B_ROPE, -1)]
    in_cols = np.concatenate(cols)
    assert in_cols.shape[0] == _AB_COLS

    uq = []
    for hd in range(B_HEADS):
        base = hd * (B_NOPE + B_ROPE)
        uq += [base + np.arange(B_NOPE), base + B_NOPE + _deinterleave(B_ROPE),
               np.full(LANES - B_NOPE - B_ROPE, -1)]
    ukv_k, ukv_v = [], []
    pad = np.full(LANES - B_NOPE, -1)
    for hd in range(B_HEADS):
        base = hd * (B_NOPE + B_V)
        ukv_k += [base + np.arange(B_NOPE), pad]
        ukv_v += [base + B_NOPE + np.arange(B_V)]
    out_rows = np.concatenate([hd * HEAD_DIM + np.arange(HEAD_DIM) for hd in _A_SLOTS]
                              + [A_HEADS * HEAD_DIM + np.arange(B_HEADS * B_V)])
    return in_cols, va_cols, np.concatenate(uq), np.concatenate(ukv_k), np.concatenate(ukv_v), out_rows


def _rope_tables(seq_len, ctx_len):
    t = jnp.arange(seq_len, dtype=jnp.int32)
    row = (t // GRID_W).astype(jnp.float32)
    col = (t % GRID_W).astype(jnp.float32)

    def angles(rot_dim):
        ppa = rot_dim // 4
        inv = ROPE_THETA ** (-jnp.arange(ppa, dtype=jnp.float32) / ppa)
        ang = jnp.concatenate([row[:, None] * inv, col[:, None] * inv], axis=-1)
        return jnp.cos(ang), jnp.sin(ang)

    def with_ctx(c, s1, s2):
        pad = lambda a, v: jnp.concatenate([jnp.full((ctx_len, LANES), v, jnp.float32), a], axis=0)
        return pad(c, 1.0), pad(s1, 0.0), pad(s2, 0.0)

    cos_a, sin_a = angles(HEAD_DIM)
    z32 = jnp.zeros_like(sin_a)
    tabs_a = with_ctx(jnp.tile(cos_a, (1, 4)), jnp.tile(jnp.concatenate([-sin_a, z32], 1), (1, 2)),
                      jnp.tile(jnp.concatenate([z32, sin_a], 1), (1, 2)))
    cos_b, sin_b = angles(B_ROPE)
    one64 = jnp.ones((seq_len, B_NOPE), jnp.float32)
    z64 = jnp.zeros((seq_len, B_NOPE), jnp.float32)
    z16 = jnp.zeros_like(sin_b)
    one32 = jnp.ones((seq_len, LANES - B_NOPE - B_ROPE), jnp.float32)
    z32b = jnp.zeros((seq_len, LANES - B_NOPE - B_ROPE), jnp.float32)
    tabs_b = with_ctx(jnp.concatenate([one64, cos_b, cos_b, one32], 1),
                      jnp.concatenate([z64, -sin_b, z16, z32b], 1),
                      jnp.concatenate([z64, z16, sin_b, z32b], 1))
    return tabs_a, tabs_b


def kernel(x, c, ctx, c_ctx, ada_w, ada_b, norm_g, ab_in_w, ab_out_w, a_sink, b_q_norm_g, b_w_uq,
           b_kv_norm_g, b_w_ukv, c_in_w, c_out_w, c_rpb, final_g):
    n_batch, seq_len, d = x.shape
    ctx_len = ctx.shape[1]
    depth = ada_w.shape[0]
    assert d == D_MODEL and ctx_len == TILE and seq_len % MLA_CHUNK == 0 and seq_len % GRID_W == 0
    assert depth % 2 == 0, "the final norm is fused into the last (odd) layer's output projection"
    bf = jnp.bfloat16

    mod_rows = -(-(n_batch + 1) // 8) * 8
    cc = jnp.concatenate([c, c_ctx[None, :], jnp.zeros((mod_rows - n_batch - 1, d), jnp.float32)], axis=0)
    mods = _mods(cc, ada_w, ada_b).reshape(depth, mod_rows, 1, 3 * d)

    in_cols, va_cols, uq_cols, uk_cols, uv_cols, out_rows = _even_layout()
    tabs_a, tabs_b = _rope_tables(seq_len, ctx_len)
    h = jnp.concatenate([ctx, x], axis=1)
    row2 = lambda v: v.reshape(1, -1)

    out = None
    for layer in range(depth):
        i = layer // 2
        last = layer == depth - 1
        g = row2(norm_g[layer])
        if layer % 2 == 0:
            w_in = _take_cols(ab_in_w[i], in_cols).astype(bf)
            wuq = _take_cols(b_w_uq[i], uq_cols).astype(bf)
            wuk = _take_cols(b_w_ukv[i], uk_cols).astype(bf)
            wvt = _take_cols(b_w_ukv[i], uv_cols).T.astype(bf)
            w_out = jnp.take(ab_out_w[i], jnp.asarray(out_rows), axis=0).astype(bf)
            sink = jnp.take(a_sink[i], jnp.asarray(_A_SLOTS))
            wvat = _take_cols(ab_in_w[i], va_cols).T.astype(bf)
            qa, ka, vat, gate, qb, kb, vbt = _proj_even(h, mods[layer], g, w_in, wvat, row2(b_q_norm_g[i]), wuq,
                                                        row2(b_kv_norm_g[i]), wuk, wvt, tabs_a, tabs_b)
            o1 = _win_attn(qa, ka, vat, sink, ctx_len)
            o2 = _mla_attn(qb, kb, vbt, ctx_len)
            o2_block = 0
        else:
            w = c_in_w[i]
            w_qkz = jnp.concatenate([w[:, :2 * d], w[:, 3 * d:]], axis=1).astype(bf)
            q, k, vt, gate = _proj_odd(h, mods[layer], g, w_qkz, w[:, 2 * d:3 * d].T.astype(bf))
            o1 = o2 = _nbr_attn(q, k, vt, _nbr_toeplitz(c_rpb[i]), ctx_len)
            o2_block = 1
            w_out = c_out_w[i].astype(bf)
        if last:
            out = _out_proj(h, mods[layer], o1, o2, gate, w_out, o2_block, final_g=row2(final_g))
        else:
            h = _out_proj(h, mods[layer], o1, o2, gate, w_out, o2_block)
    return out
```

```python
import functools
import math

import numpy as np
import jax
import jax.numpy as jnp
from jax import lax
from jax.experimental import pallas as pl
from jax.experimental.pallas import tpu as pltpu

D_MODEL = 1024
HEAD_DIM = 64
GRID_W = 64
ROPE_THETA = 10000.0
NORM_EPS = 1e-6
A_HEADS = 8
A_KV_HEADS = 2
A_WINDOW = 128
B_HEADS = 8
B_Q_LORA = 384
B_KV_LORA = 256
B_NOPE = 64
B_ROPE = 32
B_V = 64
C_HEADS = 16
C_KH = 8
C_KW = 16

LANES = 128
TILE = 256
MLA_CHUNK = 1024
LOG2E = math.log2(math.e)
NEG = -1e30
VMEM_LIMIT = 48 * 1024 * 1024

_QA, _KA, _ZA, _ZB, _CQ, _CKV, _KR, _AB_COLS = 0, 512, 640, 1152, 1664, 2048, 2304, 2432
_A_SLOTS = (0, 4, 1, 5, 2, 6, 3, 7)


def _cparams(sem):
    return pltpu.CompilerParams(dimension_semantics=sem, vmem_limit_bytes=VMEM_LIMIT)


def _silu(x):
    return x * (1.0 / (1.0 + jnp.exp(-x)))


def _dot(a, b):
    return jnp.dot(a, b, preferred_element_type=jnp.float32)


def _dot_t(a, b):
    return lax.dot_general(a, b, (((1,), (1,)), ((), ())), preferred_element_type=jnp.float32)


def _mods_body(c_ref, w_ref, b_ref, o_ref):
    sc = _silu(c_ref[...])
    o_ref[...] = jnp.dot(sc, w_ref[...], preferred_element_type=jnp.float32,
                         precision=lax.Precision.HIGHEST) + b_ref[...]


def _mods(cc, ada_w, ada_b):
    depth, d, d3 = ada_w.shape
    rows = cc.shape[0]
    tn = 512
    return pl.pallas_call(
        _mods_body,
        grid=(depth, d3 // tn),
        in_specs=[
            pl.BlockSpec((rows, d), lambda l, j: (0, 0)),
            pl.BlockSpec((None, d, tn), lambda l, j: (l, 0, j)),
            pl.BlockSpec((None, 1, tn), lambda l, j: (l, 0, j)),
        ],
        out_specs=pl.BlockSpec((None, rows, tn), lambda l, j: (l, 0, j)),
        out_shape=jax.ShapeDtypeStruct((depth, rows, d3), jnp.float32),
        compiler_params=_cparams(("arbitrary", "arbitrary")),
        name="mods",
    )(cc, ada_w, ada_b.reshape(depth, 1, d3))


def _mod_spec(n_batch):
    return pl.BlockSpec((None, 1, 3 * D_MODEL), lambda b, i: (jnp.where(i == 0, n_batch, b), 0, 0))


def _norm_mod(h, g, mod):
    y = h * lax.rsqrt(jnp.mean(h * h, axis=-1, keepdims=True) + NORM_EPS) * g
    return y * (1.0 + mod[:, D_MODEL:2 * D_MODEL]) + mod[:, :D_MODEL]


def _rms(x, g):
    return x * lax.rsqrt(jnp.mean(x * x, axis=-1, keepdims=True) + NORM_EPS) * g


def _rope(x, c, s1, s2, half):
    return x * c + pltpu.roll(x, LANES - half, 1) * s1 + pltpu.roll(x, half, 1) * s2


def _proj_even_body(h_ref, mod_ref, g_ref, w_ref, wvat_ref, qg_ref, wuq_ref, kvg_ref, wukv_ref, wvt_ref,
                    ca_ref, sa1_ref, sa2_ref, cb_ref, sb1_ref, sb2_ref,
                    qa_ref, ka_ref, vat_ref, gate_ref, qb_ref, kb_ref, vbt_ref):
    u = _norm_mod(h_ref[...], g_ref[...], mod_ref[...]).astype(jnp.bfloat16)
    ca, sa1, sa2 = ca_ref[...], sa1_ref[...], sa2_ref[...]
    cb, sb1, sb2 = cb_ref[...], sb1_ref[...], sb2_ref[...]

    qk = _dot(u, w_ref[:, _QA:_ZA])
    a_scale = HEAD_DIM ** -0.5 * LOG2E
    for j in range(4):
        blk = _rope(qk[:, j * LANES:(j + 1) * LANES], ca, sa1, sa2, HEAD_DIM // 2)
        qa_ref[:, j * LANES:(j + 1) * LANES] = (blk * a_scale).astype(jnp.bfloat16)
    ka_ref[...] = _rope(qk[:, _KA:_ZA], ca, sa1, sa2, HEAD_DIM // 2).astype(jnp.bfloat16)
    vat = _dot_t(wvat_ref[...], u).astype(jnp.bfloat16)
    for j in range(TILE // LANES):
        vat_ref[j] = vat[:, j * LANES:(j + 1) * LANES]
    gate_ref[...] = _silu(_dot(u, w_ref[:, _ZA:_CQ])).astype(jnp.bfloat16)

    lat = _dot(u, w_ref[:, _CQ:_AB_COLS])
    cq = _rms(lat[:, :B_Q_LORA], qg_ref[...]).astype(jnp.bfloat16)
    ckv = _rms(lat[:, B_Q_LORA:B_Q_LORA + B_KV_LORA], kvg_ref[...]).astype(jnp.bfloat16)
    kr = _rope(lat[:, B_Q_LORA + B_KV_LORA:], cb, sb1, sb2, B_ROPE // 2)

    qb = _dot(cq, wuq_ref[...])
    b_scale = (B_NOPE + B_ROPE) ** -0.5 * LOG2E
    kb = _dot(ckv, wukv_ref[...])
    for hd in range(B_HEADS):
        sl = slice(hd * LANES, (hd + 1) * LANES)
        qb_ref[:, sl] = (_rope(qb[:, sl], cb, sb1, sb2, B_ROPE // 2) * b_scale).astype(jnp.bfloat16)
        kb_ref[:, sl] = (kb[:, sl] + kr).astype(jnp.bfloat16)
    vbt_ref[...] = _dot_t(wvt_ref[...], ckv).astype(jnp.bfloat16)


def _proj_even(h, mods_l, g, w_in, wvat, qg, wuq, kvg, wukv, wvt, tabs_a, tabs_b):
    n_batch, t_len, d = h.shape
    nt = t_len // TILE
    row = lambda w: pl.BlockSpec((None, TILE, w), lambda b, i: (b, i, 0))
    full = lambda a: pl.BlockSpec(a.shape, lambda b, i: (0,) * a.ndim)
    tab = pl.BlockSpec((TILE, LANES), lambda b, i: (i, 0))
    bf = jnp.bfloat16
    sds = lambda *shape: jax.ShapeDtypeStruct((n_batch,) + shape, bf)
    n_vt = B_HEADS * B_V
    per_tile = TILE // LANES
    return pl.pallas_call(
        _proj_even_body,
        grid=(n_batch, nt),
        in_specs=[row(d), _mod_spec(n_batch), full(g), full(w_in), full(wvat), full(qg), full(wuq), full(kvg),
                  full(wukv), full(wvt)] + [tab] * 6,
        out_specs=[row(512), row(LANES),
                   pl.BlockSpec((None, per_tile, LANES, LANES), lambda b, i: (b, i, 0, 0)),
                   row(d), row(d), row(d),
                   pl.BlockSpec((None, None, n_vt, TILE), lambda b, i: (b, i, 0, 0))],
        out_shape=[sds(t_len, 512), sds(t_len, LANES), sds(t_len // LANES, LANES, LANES),
                   sds(t_len, d), sds(t_len, d), sds(t_len, d), sds(nt, n_vt, TILE)],
        compiler_params=_cparams(("parallel", "arbitrary")),
        name="proj_even",
    )(h, mods_l, g, w_in, wvat, qg, wuq, kvg, wukv, wvt, *tabs_a, *tabs_b)


def _proj_odd_body(h_ref, mod_ref, g_ref, w_ref, wvt_ref, q_ref, k_ref, vt_ref, gate_ref):
    u = _norm_mod(h_ref[...], g_ref[...], mod_ref[...]).astype(jnp.bfloat16)
    d = D_MODEL
    q_ref[...] = (_dot(u, w_ref[:, :d]) * (HEAD_DIM ** -0.5 * LOG2E)).astype(jnp.bfloat16)
    k_ref[...] = _dot(u, w_ref[:, d:2 * d]).astype(jnp.bfloat16)
    vt_ref[...] = _dot_t(wvt_ref[...], u).astype(jnp.bfloat16)
    gate_ref[...] = _silu(_dot(u, w_ref[:, 2 * d:])).astype(jnp.bfloat16)


def _proj_odd(h, mods_l, g, w_qkz, wvt):
    n_batch, t_len, d = h.shape
    nt = t_len // TILE
    row = pl.BlockSpec((None, TILE, d), lambda b, i: (b, i, 0))
    row_t = pl.BlockSpec((None, None, d, TILE), lambda b, i: (b, i, 0, 0))
    full = lambda a: pl.BlockSpec(a.shape, lambda b, i: (0,) * a.ndim)
    out = jax.ShapeDtypeStruct((n_batch, t_len, d), jnp.bfloat16)
    return pl.pallas_call(
        _proj_odd_body,
        grid=(n_batch, nt),
        in_specs=[row, _mod_spec(n_batch), full(g), full(w_qkz), full(wvt)],
        out_specs=[row, row, row_t, row],
        out_shape=[out, out, jax.ShapeDtypeStruct((n_batch, nt, d, TILE), jnp.bfloat16), out],
        compiler_params=_cparams(("parallel", "arbitrary")),
        name="proj_odd",
    )(h, mods_l, g, w_qkz, wvt)


def _lane_half_masks(rows, dtype):
    lane = lax.broadcasted_iota(jnp.int32, (rows, LANES), 1)
    lo = lane < HEAD_DIM
    return lo, jnp.logical_not(lo)


def _win_fill_mask(mask_ref):
    r = lax.broadcasted_iota(jnp.int32, (2 * TILE, TILE), 0)
    c = lax.broadcasted_iota(jnp.int32, (2 * TILE, TILE), 1)
    for case, (back, first) in enumerate(((A_WINDOW, A_WINDOW), (A_WINDOW, 0), (2 * A_WINDOW, 0))):
        ok = (jnp.abs(r - back - c) <= A_WINDOW) & (r >= first)
        mask_ref[case] = jnp.where(ok, 0.0, NEG)
    mask_ref[3] = jnp.full((2 * TILE, TILE), NEG, jnp.float32)


def _win_body(sink_ref, q_ref, k_ref, vt_ref, o_ref, mask_ref, sa_ref, sb_ref, p_ref, *, nt):
    assert TILE == 2 * A_WINDOW
    pair = pl.program_id(1)
    _win_fill_mask(mask_ref)
    n_keys = 3 * TILE
    n128 = nt * (TILE // LANES)
    ones = jnp.ones((16, n_keys), jnp.bfloat16)
    zero = jnp.zeros((), jnp.bfloat16)
    blk = LANES
    masks = _lane_half_masks(TILE, None)
    sinks = tuple(sink_ref[2 * pair + hh] * LOG2E for hh in range(2))

    def window(i):
        return jnp.clip(2 * i - 1, 0, n128 - 4)

    def scores(i, s_ref):
        case = jnp.where(i == 0, 3, jnp.where(i == 1, 0, jnp.where(i == nt - 1, 2, 1)))
        k_ctx = k_ref[:TILE, :]
        k_loc = k_ref[pl.ds(pl.multiple_of(window(i) * LANES, LANES), 2 * TILE), :]
        q2 = q_ref[pl.ds(pl.multiple_of(i * TILE, TILE), TILE), :]
        m = []
        for hh in range(2):
            qm = jnp.where(masks[hh], q2, zero)
            s_ctx = _dot_t(k_ctx, qm)
            s_ref[hh, :TILE, :] = s_ctx
            s_loc = _dot_t(k_loc, qm) + mask_ref[case]
            s_ref[hh, TILE:, :] = s_loc
            m.append(jnp.maximum(jnp.maximum(jnp.max(s_ctx, axis=0, keepdims=True),
                                             jnp.max(s_loc, axis=0, keepdims=True)), sinks[hh]))
        return tuple(m)

    def update(i, s_ref, m):
        w0 = window(i)
        vt = jnp.concatenate([vt_ref[0], vt_ref[1]] + [vt_ref[w0 + j] for j in range(4)], axis=1)
        outs = []
        for hh in range(2):
            for c in range(n_keys // blk):
                p = jnp.exp2(s_ref[hh, c * blk:(c + 1) * blk, :] - m[hh])
                p_ref[hh, c * blk:(c + 1) * blk, :] = p.astype(jnp.bfloat16)
            vt_h = jnp.concatenate([vt[hh * HEAD_DIM:(hh + 1) * HEAD_DIM], ones], axis=0)
            acc = _dot(vt_h, p_ref[hh])
            denom = acc[HEAD_DIM:HEAD_DIM + 1] + jnp.exp2(sinks[hh] - m[hh])
            outs.append(acc[:HEAD_DIM] * (1.0 / denom))
        o_ref[pl.ds(pl.multiple_of(i * TILE, TILE), TILE), :] = jnp.concatenate(outs, axis=0).T.astype(jnp.bfloat16)

    def body(t, m_a):
        m_b = scores(2 * t + 1, sb_ref)
        update(2 * t, sa_ref, m_a)
        m_a = scores(2 * t + 2, sa_ref)
        update(2 * t + 1, sb_ref, m_b)
        return m_a

    m_a = lax.fori_loop(0, (nt - 1) // 2, body, scores(0, sa_ref))
    update(nt - 1, sa_ref, m_a)


def _win_attn(qa, ka, vat, sink, ctx_len):
    n_batch, t_len, _ = qa.shape
    nt = t_len // TILE
    assert ctx_len == TILE and nt >= 4 and nt % 2 == 1, "the tile loop handles two tiles per trip plus a last one"
    return pl.pallas_call(
        functools.partial(_win_body, nt=nt),
        grid=(n_batch, A_HEADS // 2),
        in_specs=[pl.BlockSpec(memory_space=pltpu.SMEM),
                  pl.BlockSpec((None, t_len, LANES), lambda b, p: (b, 0, p)),
                  pl.BlockSpec((None, t_len, LANES), lambda b, p: (b, 0, 0)),
                  pl.BlockSpec((None,) + vat.shape[1:], lambda b, p: (b, 0, 0, 0))],
        out_specs=pl.BlockSpec((None, t_len, LANES), lambda b, p: (b, 0, p)),
        out_shape=jax.ShapeDtypeStruct((n_batch, t_len, A_HEADS * HEAD_DIM), jnp.bfloat16),
        scratch_shapes=[pltpu.VMEM((4, 2 * TILE, TILE), jnp.float32),
                        pltpu.VMEM((2, 3 * TILE, TILE), jnp.float32),
                        pltpu.VMEM((2, 3 * TILE, TILE), jnp.float32),
                        pltpu.VMEM((2, 3 * TILE, TILE), jnp.bfloat16)],
        compiler_params=_cparams(("parallel", "parallel")),
        name="win_attn",
    )(sink, qa, ka, vat)


def _mla_body(q_ref, k_ref, vt_ref, o_ref, sa_ref, sb_ref, sc_ref, p_ref, *, nt, n_chunks):
    sub = MLA_CHUNK // TILE
    blk = LANES
    ones_rows = 16
    n_items = 1 + n_chunks
    bufs = (sa_ref, sb_ref, sc_ref)
    assert n_items % len(bufs) == 0

    def tile_rows(t):
        return pl.ds(pl.multiple_of(t * TILE, TILE), TILE)

    def keys(n):
        return (0, 0, 1) if n == 0 else (TILE + (n - 1) * MLA_CHUNK, 1 + (n - 1) * sub, sub)

    def scores(qi, n):
        start, _, n_tiles = keys(n)
        size = n_tiles * TILE
        s_ref = bufs[n % len(bufs)]
        cm = []
        for hh in range(2):
            cols = slice(hh * LANES, (hh + 1) * LANES)
            s = _dot_t(k_ref[start:start + size, cols], q_ref[tile_rows(qi), cols])
            s_ref[hh, :size, :] = s
            cm.append(jnp.max(s, axis=0, keepdims=True))
        return tuple(cm)

    def update(state, cm, n):
        _, tile0, n_tiles = keys(n)
        size = n_tiles * TILE
        s_ref = bufs[n % len(bufs)]
        new = []
        for hh in range(2):
            m_prev, acc = state[hh]
            m_new = jnp.maximum(m_prev, cm[hh])
            alpha = jnp.exp2(m_prev - m_new)
            for c in range(size // blk):
                p = jnp.exp2(s_ref[hh, c * blk:(c + 1) * blk, :] - m_new)
                p_ref[hh, c * blk:(c + 1) * blk, :] = p.astype(jnp.bfloat16)
            vt = jnp.concatenate([vt_ref[tile0 + c, hh * B_V:(hh + 1) * B_V, :] for c in range(n_tiles)], axis=1)
            vt = jnp.concatenate([vt, jnp.ones((ones_rows, size), jnp.bfloat16)], axis=0)
            new.append((m_new, acc * alpha + _dot(vt, p_ref[hh, :size, :])))
        return tuple(new)

    def finish(state, qi):
        o_t = jnp.concatenate([acc[:B_V] * (1.0 / acc[B_V:B_V + 1]) for _, acc in state], axis=0)
        o_ref[tile_rows(qi), :] = o_t.T.astype(jnp.bfloat16)

    init = tuple((jnp.full((1, TILE), NEG, jnp.float32), jnp.zeros((B_V + ones_rows, TILE), jnp.float32))
                 for _ in range(2))
    finish(update(init, scores(0, 0), 0), 0)

    def body(qi, cm):
        state = init
        nxt = jnp.minimum(qi + 1, nt - 1)
        for n in range(n_items):
            cm_next = scores(qi, n + 1) if n + 1 < n_items else scores(nxt, 0)
            state = update(state, cm, n)
            cm = cm_next
        finish(state, qi)
        return cm

    lax.fori_loop(1, nt, body, scores(1, 0))


def _mla_attn(qb, kb, vbt, ctx_len):
    n_batch, t_len, _ = qb.shape
    nt = t_len // TILE
    seq = t_len - ctx_len
    assert ctx_len == TILE and seq % MLA_CHUNK == 0 and MLA_CHUNK % TILE == 0
    n_chunks = seq // MLA_CHUNK
    assert (1 + n_chunks) % 3 == 0, "context keys + latent chunks rotate over three score buffers"
    body = functools.partial(_mla_body, nt=nt, n_chunks=n_chunks)
    s_buf = pltpu.VMEM((2, MLA_CHUNK, TILE), jnp.float32)
    return pl.pallas_call(
        body,
        grid=(n_batch, B_HEADS // 2),
        in_specs=[pl.BlockSpec((None, t_len, 2 * LANES), lambda b, p: (b, 0, p)),
                  pl.BlockSpec((None, t_len, 2 * LANES), lambda b, p: (b, 0, p)),
                  pl.BlockSpec((None, nt, 2 * B_V, TILE), lambda b, p: (b, 0, p, 0))],
        out_specs=pl.BlockSpec((None, t_len, LANES), lambda b, p: (b, 0, p)),
        out_shape=jax.ShapeDtypeStruct((n_batch, t_len, B_HEADS * B_V), jnp.bfloat16),
        scratch_shapes=[s_buf, s_buf, s_buf, pltpu.VMEM((2, MLA_CHUNK, TILE), jnp.bfloat16)],
        compiler_params=_cparams(("parallel", "parallel")),
        name="mla_attn",
    )(qb, kb, vbt)


def _nbr_row_valid(case, jr, t):
    rpt = TILE // GRID_W
    lo = (rpt, jr + rpt - C_KH // 2, 2 * rpt - C_KH)[case]
    return lo <= t < lo + C_KH


def _nbr_fill_bias(tz_ref, bias_ref):
    rpt = TILE // GRID_W
    lane_lo = lax.broadcasted_iota(jnp.int32, (GRID_W, LANES), 1) < GRID_W
    neg = jnp.full((GRID_W, LANES), NEG, jnp.float32)
    for hh in range(2):
        bias_ref[3, hh] = jnp.full((3 * TILE, TILE), NEG, jnp.float32)
        for case in range(3):
            for t in range(3 * rpt):
                for jp in range(rpt // 2):
                    ok0 = _nbr_row_valid(case, 2 * jp, t)
                    ok1 = _nbr_row_valid(case, 2 * jp + 1, t)
                    if ok0 or ok1:
                        blk = tz_ref[hh, t - rpt - 2 * jp + C_KH - 2]
                        if not ok1:
                            blk = jnp.where(lane_lo, blk, NEG)
                        elif not ok0:
                            blk = jnp.where(lane_lo, NEG, blk)
                    else:
                        blk = neg
                    bias_ref[case, hh, t * GRID_W:(t + 1) * GRID_W, jp * LANES:(jp + 1) * LANES] = blk


def _nbr_body(q_ref, k_ref, vt_ref, tz_ref, o_ref, bias_ref, sa_ref, sb_ref, p_ref, *, nt):
    _nbr_fill_bias(tz_ref, bias_ref)
    n_keys = 4 * TILE
    ones = jnp.ones((16, n_keys), jnp.bfloat16)
    zero = jnp.zeros((), jnp.bfloat16)
    blk = LANES
    masks = _lane_half_masks(TILE, None)

    def key_tiles(i):
        return 0, jnp.clip(i - 1, 1, nt - 1), i, jnp.clip(i + 1, 1, nt - 1)

    def rows(t):
        return pl.ds(pl.multiple_of(t * TILE, TILE), TILE)

    def scores(i, s_ref):
        case = jnp.where(i == 0, 3, jnp.where(i == 1, 0, jnp.where(i == nt - 1, 2, 1)))
        tc, t0, t1, t2 = key_tiles(i)
        k_ctx = k_ref[rows(tc), :]
        k_loc = jnp.concatenate([k_ref[rows(t0), :], k_ref[rows(t1), :], k_ref[rows(t2), :]], axis=0)
        q2 = q_ref[rows(i), :]
        m = []
        for hh in range(2):
            qm = jnp.where(masks[hh], q2, zero)
            s_ctx = _dot_t(k_ctx, qm)
            s_ref[hh, :TILE, :] = s_ctx
            s_loc = _dot_t(k_loc, qm) + bias_ref[case, hh]
            s_ref[hh, TILE:, :] = s_loc
            m.append(jnp.maximum(jnp.max(s_ctx, axis=0, keepdims=True), jnp.max(s_loc, axis=0, keepdims=True)))
        return tuple(m)

    def update(i, s_ref, m):
        vt = jnp.concatenate([vt_ref[t] for t in key_tiles(i)], axis=1)
        outs = []
        for hh in range(2):
            for c in range(n_keys // blk):
                p = jnp.exp2(s_ref[hh, c * blk:(c + 1) * blk, :] - m[hh])
                p_ref[hh, c * blk:(c + 1) * blk, :] = p.astype(jnp.bfloat16)
            vt_h = jnp.concatenate([vt[hh * HEAD_DIM:(hh + 1) * HEAD_DIM], ones], axis=0)
            acc = _dot(vt_h, p_ref[hh])
            outs.append(acc[:HEAD_DIM] * (1.0 / acc[HEAD_DIM:HEAD_DIM + 1]))
        o_ref[rows(i), :] = jnp.concatenate(outs, axis=0).T.astype(jnp.bfloat16)

    def body(t, m_a):
        m_b = scores(2 * t + 1, sb_ref)
        update(2 * t, sa_ref, m_a)
        m_a = scores(2 * t + 2, sa_ref)
        update(2 * t + 1, sb_ref, m_b)
        return m_a

    m_a = lax.fori_loop(0, (nt - 1) // 2, body, scores(0, sa_ref))
    update(nt - 1, sa_ref, m_a)


def _nbr_attn(q, k, vt, tz, ctx_len):
    n_batch, t_len, d = q.shape
    nt = t_len // TILE
    assert ctx_len == TILE and TILE % GRID_W == 0 and C_KH == 2 * (TILE // GRID_W)
    assert nt >= 4, "needs distinct first / interior / last latent tiles"
    assert nt % 2 == 1, "the tile loop handles two tiles per trip plus a last one"
    seq = pl.BlockSpec((None, t_len, LANES), lambda b, p: (b, 0, p))
    return pl.pallas_call(
        functools.partial(_nbr_body, nt=nt),
        grid=(n_batch, C_HEADS // 2),
        in_specs=[seq, seq, pl.BlockSpec((None, nt, LANES, TILE), lambda b, p: (b, 0, p, 0)),
                  pl.BlockSpec((2,) + tz.shape[1:], lambda b, p: (p, 0, 0, 0))],
        out_specs=seq,
        out_shape=jax.ShapeDtypeStruct((n_batch, t_len, d), jnp.bfloat16),
        scratch_shapes=[pltpu.VMEM((4, 2, 3 * TILE, TILE), jnp.float32),
                        pltpu.VMEM((2, 4 * TILE, TILE), jnp.float32),
                        pltpu.VMEM((2, 4 * TILE, TILE), jnp.float32),
                        pltpu.VMEM((2, 4 * TILE, TILE), jnp.bfloat16)],
        compiler_params=_cparams(("parallel", "parallel")),
        name="nbr_attn",
    )(q, k, vt, tz)


def _nbr_toeplitz(rpb):
    n_off = 2 * C_KW - 1
    pad = GRID_W - C_KW
    ext = jnp.pad(rpb, ((0, 0), (0, 0), (pad, pad)), mode="edge")[:, :, ::-1]
    assert ext.shape[-1] == n_off + 2 * pad == 2 * GRID_W - 1
    tz = jnp.stack([ext[:, :, GRID_W - 1 - kc:2 * GRID_W - 1 - kc] for kc in range(GRID_W)], axis=2)
    qc = np.arange(GRID_W)[None, :]
    kc = np.arange(GRID_W)[:, None]
    cs = np.clip(qc - C_KW // 2, 0, GRID_W - C_KW)
    col_ok = jnp.asarray((kc >= cs) & (kc < cs + C_KW))
    tz = jnp.where(col_ok, tz * LOG2E, NEG)
    return jnp.concatenate([tz[:, 1:], tz[:, :-1]], axis=-1)


def _out_body(h_ref, mod_ref, o1_ref, o2_ref, gate_ref, w_ref, *rest, final):
    half = D_MODEL // 2
    g = gate_ref[...].astype(jnp.float32)
    og1 = (o1_ref[...].astype(jnp.float32) * g[:, :half]).astype(jnp.bfloat16)
    og2 = (o2_ref[...].astype(jnp.float32) * g[:, half:]).astype(jnp.bfloat16)
    y = _dot(og1, w_ref[:half, :]) + _dot(og2, w_ref[half:, :])
    h_new = h_ref[...] + mod_ref[:, 2 * D_MODEL:] * y
    if final:
        fg_ref, out_ref = rest
        out_ref[...] = _rms(h_new, fg_ref[...])
    else:
        rest[0][...] = h_new


def _out_proj(h, mods_l, o1, o2, gate, w, o2_block, final_g=None):
    n_batch, t_len, d = h.shape
    nt = t_len // TILE
    final = final_g is not None
    off = 1 if final else 0
    row = lambda wd, blk=0: pl.BlockSpec((None, TILE, wd), lambda b, i: (b, i + off, blk))
    mod = pl.BlockSpec((None, 1, 3 * D_MODEL), lambda b, i: (jnp.where(i + off == 0, n_batch, b), 0, 0))
    full = lambda a: pl.BlockSpec(a.shape, lambda b, i: (0,) * a.ndim)
    in_specs = [row(d), mod, row(d // 2), row(d // 2, o2_block), row(d), full(w)]
    args = [h, mods_l, o1, o2, gate, w]
    if final:
        in_specs.append(full(final_g))
        args.append(final_g)
        out_spec = pl.BlockSpec((None, TILE, d), lambda b, i: (b, i, 0))
        out_shape = jax.ShapeDtypeStruct((n_batch, t_len - TILE, d), jnp.float32)
        aliases = {}
    else:
        out_spec = row(d)
        out_shape = jax.ShapeDtypeStruct(h.shape, jnp.float32)
        aliases = {0: 0}
    return pl.pallas_call(
        functools.partial(_out_body, final=final),
        grid=(n_batch, nt - off),
        in_specs=in_specs,
        out_specs=out_spec,
        out_shape=out_shape,
        input_output_aliases=aliases,
        compiler_params=_cparams(("parallel", "arbitrary")),
        name="out_proj_final" if final else "out_proj",
    )(*args)


def _deinterleave(n):
    return np.concatenate([np.arange(0, n, 2), np.arange(1, n, 2)])


def _take_cols(w, cols):
    cols = np.asarray(cols)
    picked = jnp.take(w, jnp.asarray(np.maximum(cols, 0)), axis=1)
    return jnp.where(jnp.asarray(cols >= 0)[None, :], picked, 0.0)


def _even_layout():
    perm = _deinterleave(HEAD_DIM)
    qa0, ka0 = 0, A_HEADS * HEAD_DIM
    va0 = ka0 + A_KV_HEADS * HEAD_DIM
    za0 = va0 + A_KV_HEADS * HEAD_DIM
    cq0 = za0 + A_HEADS * HEAD_DIM
    ckv0 = cq0 + B_Q_LORA
    kr0 = ckv0 + B_KV_LORA
    zb0 = kr0 + B_ROPE
    cols = []
    cols += [qa0 + hd * HEAD_DIM + perm for hd in _A_SLOTS]
    cols += [ka0 + kv * HEAD_DIM + perm for kv in range(A_KV_HEADS)]
    va_cols = va0 + np.arange(A_KV_HEADS * HEAD_DIM)
    cols += [za0 + hd * HEAD_DIM + np.arange(HEAD_DIM) for hd in _A_SLOTS]
    cols += [zb0 + np.arange(B_HEADS * B_V)]
    cols += [cq0 + np.arange(B_Q_LORA), ckv0 + np.arange(B_KV_LORA)]
    cols += [np.full(B_NOPE, -1), kr0 + _deinterleave(B_ROPE), np.full(LANES - B_NOPE - B_ROPE, -1)]
    in_cols = np.concatenate(cols)
    assert in_cols.shape[0] == _AB_COLS

    uq = []
    for hd in range(B_HEADS):
        base = hd * (B_NOPE + B_ROPE)
        uq += [base + np.arange(B_NOPE), base + B_NOPE + _deinterleave(B_ROPE),
               np.full(LANES - B_NOPE - B_ROPE, -1)]
    ukv_k, ukv_v = [], []
    pad = np.full(LANES - B_NOPE, -1)
    for hd in range(B_HEADS):
        base = hd * (B_NOPE + B_V)
        ukv_k += [base + np.arange(B_NOPE), pad]
        ukv_v += [base + B_NOPE + np.arange(B_V)]
    out_rows = np.concatenate([hd * HEAD_DIM + np.arange(HEAD_DIM) for hd in _A_SLOTS]
                              + [A_HEADS * HEAD_DIM + np.arange(B_HEADS * B_V)])
    return in_cols, va_cols, np.concatenate(uq), np.concatenate(ukv_k), np.concatenate(ukv_v), out_rows


def _rope_tables(seq_len, ctx_len):
    t = jnp.arange(seq_len, dtype=jnp.int32)
    row = (t // GRID_W).astype(jnp.float32)
    col = (t % GRID_W).astype(jnp.float32)

    def angles(rot_dim):
        ppa = rot_dim // 4
        inv = ROPE_THETA ** (-jnp.arange(ppa, dtype=jnp.float32) / ppa)
        ang = jnp.concatenate([row[:, None] * inv, col[:, None] * inv], axis=-1)
        return jnp.cos(ang), jnp.sin(ang)

    def with_ctx(c, s1, s2):
        pad = lambda a, v: jnp.concatenate([jnp.full((ctx_len, LANES), v, jnp.float32), a], axis=0)
        return pad(c, 1.0), pad(s1, 0.0), pad(s2, 0.0)

    cos_a, sin_a = angles(HEAD_DIM)
    z32 = jnp.zeros_like(sin_a)
    tabs_a = with_ctx(jnp.tile(cos_a, (1, 4)), jnp.tile(jnp.concatenate([-sin_a, z32], 1), (1, 2)),
                      jnp.tile(jnp.concatenate([z32, sin_a], 1), (1, 2)))
    cos_b, sin_b = angles(B_ROPE)
    one64 = jnp.ones((seq_len, B_NOPE), jnp.float32)
    z64 = jnp.zeros((seq_len, B_NOPE), jnp.float32)
    z16 = jnp.zeros_like(sin_b)
    one32 = jnp.ones((seq_len, LANES - B_NOPE - B_ROPE), jnp.float32)
    z32b = jnp.zeros((seq_len, LANES - B_NOPE - B_ROPE), jnp.float32)
    tabs_b = with_ctx(jnp.concatenate([one64, cos_b, cos_b, one32], 1),
                      jnp.concatenate([z64, -sin_b, z16, z32b], 1),
                      jnp.concatenate([z64, z16, sin_b, z32b], 1))
    return tabs_a, tabs_b


def kernel(x, c, ctx, c_ctx, ada_w, ada_b, norm_g, ab_in_w, ab_out_w, a_sink, b_q_norm_g, b_w_uq,
           b_kv_norm_g, b_w_ukv, c_in_w, c_out_w, c_rpb, final_g):
    n_batch, seq_len, d = x.shape
    ctx_len = ctx.shape[1]
    depth = ada_w.shape[0]
    assert d == D_MODEL and ctx_len == TILE and seq_len % TILE == 0
    assert depth % 2 == 0, "the final norm is fused into the last (odd) layer's output projection"
    bf = jnp.bfloat16

    mod_rows = -(-(n_batch + 1) // 8) * 8
    cc = jnp.concatenate([c, c_ctx[None, :], jnp.zeros((mod_rows - n_batch - 1, d), jnp.float32)], axis=0)
    mods = _mods(cc, ada_w, ada_b).reshape(depth, mod_rows, 1, 3 * d)

    in_cols, va_cols, uq_cols, uk_cols, uv_cols, out_rows = _even_layout()
    tabs_a, tabs_b = _rope_tables(seq_len, ctx_len)
    h = jnp.concatenate([ctx, x], axis=1)
    row2 = lambda v: v.reshape(1, -1)

    out = None
    for layer in range(depth):
        i = layer // 2
        last = layer == depth - 1
        g = row2(norm_g[layer])
        if layer % 2 == 0:
            w_in = _take_cols(ab_in_w[i], in_cols).astype(bf)
            wuq = _take_cols(b_w_uq[i], uq_cols).astype(bf)
            wuk = _take_cols(b_w_ukv[i], uk_cols).astype(bf)
            wvt = _take_cols(b_w_ukv[i], uv_cols).T.astype(bf)
            w_out = jnp.take(ab_out_w[i], jnp.asarray(out_rows), axis=0).astype(bf)
            sink = jnp.take(a_sink[i], jnp.asarray(_A_SLOTS))
            wvat = _take_cols(ab_in_w[i], va_cols).T.astype(bf)
            qa, ka, vat, gate, qb, kb, vbt = _proj_even(h, mods[layer], g, w_in, wvat, row2(b_q_norm_g[i]), wuq,
                                                        row2(b_kv_norm_g[i]), wuk, wvt, tabs_a, tabs_b)
            o1 = _win_attn(qa, ka, vat, sink, ctx_len)
            o2 = _mla_attn(qb, kb, vbt, ctx_len)
            o2_block = 0
        else:
            w = c_in_w[i]
            w_qkz = jnp.concatenate([w[:, :2 * d], w[:, 3 * d:]], axis=1).astype(bf)
            q, k, vt, gate = _proj_odd(h, mods[layer], g, w_qkz, w[:, 2 * d:3 * d].T.astype(bf))
            o1 = o2 = _nbr_attn(q, k, vt, _nbr_toeplitz(c_rpb[i]), ctx_len)
            o2_block = 1
            w_out = c_out_w[i].astype(bf)
        if last:
            out = _out_proj(h, mods[layer], o1, o2, gate, w_out, o2_block, final_g=row2(final_g))
        else:
            h = _out_proj(h, mods[layer], o1, o2, gate, w_out, o2_block)
    return out
```

```python
import functools
import math

import numpy as np
import jax
import jax.numpy as jnp
from jax import lax
from jax.experimental import pallas as pl
from jax.experimental.pallas import tpu as pltpu

D_MODEL = 1024
HEAD_DIM = 64
GRID_W = 64
ROPE_THETA = 10000.0
NORM_EPS = 1e-6
A_HEADS = 8
A_KV_HEADS = 2
A_WINDOW = 128
B_HEADS = 8
B_Q_LORA = 384
B_KV_LORA = 256
B_NOPE = 64
B_ROPE = 32
B_V = 64
C_HEADS = 16
C_KH = 8
C_KW = 16

LANES = 128
TILE = 256
MLA_CHUNK = 1024
LOG2E = math.log2(math.e)
NEG = -1e30
VMEM_LIMIT = 48 * 1024 * 1024

_QA, _KA, _ZA, _ZB, _CQ, _CKV, _KR, _AB_COLS = 0, 512, 640, 1152, 1664, 2048, 2304, 2432
_A_SLOTS = (0, 4, 1, 5, 2, 6, 3, 7)


def _cparams(sem):
    return pltpu.CompilerParams(dimension_semantics=sem, vmem_limit_bytes=VMEM_LIMIT)


def _silu(x):
    return x * (1.0 / (1.0 + jnp.exp(-x)))


def _dot(a, b):
    return jnp.dot(a, b, preferred_element_type=jnp.float32)


def _dot_t(a, b):
    return lax.dot_general(a, b, (((1,), (1,)), ((), ())), preferred_element_type=jnp.float32)


def _mods_body(c_ref, w_ref, b_ref, o_ref):
    sc = _silu(c_ref[...])
    o_ref[...] = jnp.dot(sc, w_ref[...], preferred_element_type=jnp.float32,
                         precision=lax.Precision.HIGHEST) + b_ref[...]


def _mods(cc, ada_w, ada_b):
    depth, d, d3 = ada_w.shape
    rows = cc.shape[0]
    tn = 512
    return pl.pallas_call(
        _mods_body,
        grid=(depth, d3 // tn),
        in_specs=[
            pl.BlockSpec((rows, d), lambda l, j: (0, 0)),
            pl.BlockSpec((None, d, tn), lambda l, j: (l, 0, j)),
            pl.BlockSpec((None, 1, tn), lambda l, j: (l, 0, j)),
        ],
        out_specs=pl.BlockSpec((None, rows, tn), lambda l, j: (l, 0, j)),
        out_shape=jax.ShapeDtypeStruct((depth, rows, d3), jnp.float32),
        compiler_params=_cparams(("arbitrary", "arbitrary")),
        name="mods",
    )(cc, ada_w, ada_b.reshape(depth, 1, d3))


def _mod_spec(n_batch):
    return pl.BlockSpec((None, 1, 3 * D_MODEL), lambda b, i: (jnp.where(i == 0, n_batch, b), 0, 0))


def _norm_mod(h, g, mod):
    y = h * lax.rsqrt(jnp.mean(h * h, axis=-1, keepdims=True) + NORM_EPS) * g
    return y * (1.0 + mod[:, D_MODEL:2 * D_MODEL]) + mod[:, :D_MODEL]


def _rms(x, g):
    return x * lax.rsqrt(jnp.mean(x * x, axis=-1, keepdims=True) + NORM_EPS) * g


def _rope(x, c, s1, s2, half):
    return x * c + pltpu.roll(x, LANES - half, 1) * s1 + pltpu.roll(x, half, 1) * s2


def _proj_even_body(h_ref, mod_ref, g_ref, w_ref, wvat_ref, qg_ref, wuq_ref, kvg_ref, wukv_ref, wvt_ref,
                    ca_ref, sa1_ref, sa2_ref, cb_ref, sb1_ref, sb2_ref,
                    qa_ref, ka_ref, vat_ref, gate_ref, qb_ref, kb_ref, vbt_ref):
    u = _norm_mod(h_ref[...], g_ref[...], mod_ref[...]).astype(jnp.bfloat16)
    ca, sa1, sa2 = ca_ref[...], sa1_ref[...], sa2_ref[...]
    cb, sb1, sb2 = cb_ref[...], sb1_ref[...], sb2_ref[...]

    qk = _dot(u, w_ref[:, _QA:_ZA])
    a_scale = HEAD_DIM ** -0.5 * LOG2E
    for j in range(4):
        blk = _rope(qk[:, j * LANES:(j + 1) * LANES], ca, sa1, sa2, HEAD_DIM // 2)
        qa_ref[:, j * LANES:(j + 1) * LANES] = (blk * a_scale).astype(jnp.bfloat16)
    ka_ref[...] = _rope(qk[:, _KA:_ZA], ca, sa1, sa2, HEAD_DIM // 2).astype(jnp.bfloat16)
    vat = _dot_t(wvat_ref[...], u).astype(jnp.bfloat16)
    for j in range(TILE // LANES):
        vat_ref[j] = vat[:, j * LANES:(j + 1) * LANES]
    gate_ref[...] = _silu(_dot(u, w_ref[:, _ZA:_CQ])).astype(jnp.bfloat16)

    lat = _dot(u, w_ref[:, _CQ:_AB_COLS])
    cq = _rms(lat[:, :B_Q_LORA], qg_ref[...]).astype(jnp.bfloat16)
    ckv = _rms(lat[:, B_Q_LORA:B_Q_LORA + B_KV_LORA], kvg_ref[...]).astype(jnp.bfloat16)
    kr = _rope(lat[:, B_Q_LORA + B_KV_LORA:], cb, sb1, sb2, B_ROPE // 2)

    qb = _dot(cq, wuq_ref[...])
    b_scale = (B_NOPE + B_ROPE) ** -0.5 * LOG2E
    kb = _dot(ckv, wukv_ref[...])
    for hd in range(B_HEADS):
        sl = slice(hd * LANES, (hd + 1) * LANES)
        qb_ref[:, sl] = (_rope(qb[:, sl], cb, sb1, sb2, B_ROPE // 2) * b_scale).astype(jnp.bfloat16)
        kb_ref[:, sl] = (kb[:, sl] + kr).astype(jnp.bfloat16)
    vbt_ref[...] = _dot_t(wvt_ref[...], ckv).astype(jnp.bfloat16)


def _proj_even(h, mods_l, g, w_in, wvat, qg, wuq, kvg, wukv, wvt, tabs_a, tabs_b):
    n_batch, t_len, d = h.shape
    nt = t_len // TILE
    row = lambda w: pl.BlockSpec((None, TILE, w), lambda b, i: (b, i, 0))
    full = lambda a: pl.BlockSpec(a.shape, lambda b, i: (0,) * a.ndim)
    tab = pl.BlockSpec((TILE, LANES), lambda b, i: (i, 0))
    bf = jnp.bfloat16
    sds = lambda *shape: jax.ShapeDtypeStruct((n_batch,) + shape, bf)
    n_vt = B_HEADS * B_V
    per_tile = TILE // LANES
    return pl.pallas_call(
        _proj_even_body,
        grid=(n_batch, nt),
        in_specs=[row(d), _mod_spec(n_batch), full(g), full(w_in), full(wvat), full(qg), full(wuq), full(kvg),
                  full(wukv), full(wvt)] + [tab] * 6,
        out_specs=[row(512), row(LANES),
                   pl.BlockSpec((None, per_tile, LANES, LANES), lambda b, i: (b, i, 0, 0)),
                   row(d), row(d), row(d),
                   pl.BlockSpec((None, None, n_vt, TILE), lambda b, i: (b, i, 0, 0))],
        out_shape=[sds(t_len, 512), sds(t_len, LANES), sds(t_len // LANES, LANES, LANES),
                   sds(t_len, d), sds(t_len, d), sds(t_len, d), sds(nt, n_vt, TILE)],
        compiler_params=_cparams(("parallel", "arbitrary")),
        name="proj_even",
    )(h, mods_l, g, w_in, wvat, qg, wuq, kvg, wukv, wvt, *tabs_a, *tabs_b)


def _proj_odd_body(h_ref, mod_ref, g_ref, w_ref, wvt_ref, q_ref, k_ref, vt_ref, gate_ref):
    u = _norm_mod(h_ref[...], g_ref[...], mod_ref[...]).astype(jnp.bfloat16)
    d = D_MODEL
    q_ref[...] = (_dot(u, w_ref[:, :d]) * (HEAD_DIM ** -0.5 * LOG2E)).astype(jnp.bfloat16)
    k_ref[...] = _dot(u, w_ref[:, d:2 * d]).astype(jnp.bfloat16)
    vt_ref[...] = _dot_t(wvt_ref[...], u).astype(jnp.bfloat16)
    gate_ref[...] = _silu(_dot(u, w_ref[:, 2 * d:])).astype(jnp.bfloat16)


def _proj_odd(h, mods_l, g, w_qkz, wvt):
    n_batch, t_len, d = h.shape
    nt = t_len // TILE
    row = pl.BlockSpec((None, TILE, d), lambda b, i: (b, i, 0))
    row_t = pl.BlockSpec((None, None, d, TILE), lambda b, i: (b, i, 0, 0))
    full = lambda a: pl.BlockSpec(a.shape, lambda b, i: (0,) * a.ndim)
    out = jax.ShapeDtypeStruct((n_batch, t_len, d), jnp.bfloat16)
    return pl.pallas_call(
        _proj_odd_body,
        grid=(n_batch, nt),
        in_specs=[row, _mod_spec(n_batch), full(g), full(w_qkz), full(wvt)],
        out_specs=[row, row, row_t, row],
        out_shape=[out, out, jax.ShapeDtypeStruct((n_batch, nt, d, TILE), jnp.bfloat16), out],
        compiler_params=_cparams(("parallel", "arbitrary")),
        name="proj_odd",
    )(h, mods_l, g, w_qkz, wvt)


def _lane_half_masks(rows, dtype):
    lane = lax.broadcasted_iota(jnp.int32, (rows, LANES), 1)
    lo = lane < HEAD_DIM
    return lo, jnp.logical_not(lo)


def _win_fill_mask(mask_ref):
    r = lax.broadcasted_iota(jnp.int32, (2 * TILE, TILE), 0)
    c = lax.broadcasted_iota(jnp.int32, (2 * TILE, TILE), 1)
    for case, (back, first) in enumerate(((A_WINDOW, A_WINDOW), (A_WINDOW, 0), (2 * A_WINDOW, 0))):
        ok = (jnp.abs(r - back - c) <= A_WINDOW) & (r >= first)
        mask_ref[case] = jnp.where(ok, 0.0, NEG)
    mask_ref[3] = jnp.full((2 * TILE, TILE), NEG, jnp.float32)


def _win_body(sink_ref, q_ref, k_ref, vt_ref, o_ref, mask_ref, sa_ref, sb_ref, p_ref, *, nt):
    assert TILE == 2 * A_WINDOW
    pair = pl.program_id(1)
    _win_fill_mask(mask_ref)
    n_keys = 3 * TILE
    n128 = nt * (TILE // LANES)
    ones = jnp.ones((16, n_keys), jnp.bfloat16)
    zero = jnp.zeros((), jnp.bfloat16)
    blk = LANES
    masks = _lane_half_masks(TILE, None)
    sinks = tuple(sink_ref[2 * pair + hh] * LOG2E for hh in range(2))

    def window(i):
        return jnp.clip(2 * i - 1, 0, n128 - 4)

    def scores(i, s_ref, heads):
        case = jnp.where(i == 0, 3, jnp.where(i == 1, 0, jnp.where(i == nt - 1, 2, 1)))
        k_ctx = k_ref[:TILE, :]
        k_loc = k_ref[pl.ds(pl.multiple_of(window(i) * LANES, LANES), 2 * TILE), :]
        q2 = q_ref[pl.ds(pl.multiple_of(i * TILE, TILE), TILE), :]
        m = []
        for hh in heads:
            qm = jnp.where(masks[hh], q2, zero)
            s_ctx = _dot_t(k_ctx, qm)
            s_ref[hh, :TILE, :] = s_ctx
            s_loc = _dot_t(k_loc, qm) + mask_ref[case]
            s_ref[hh, TILE:, :] = s_loc
            m.append(jnp.maximum(jnp.maximum(jnp.max(s_ctx, axis=0, keepdims=True),
                                             jnp.max(s_loc, axis=0, keepdims=True)), sinks[hh]))
        return m

    def update(i, s_ref, m, heads):
        w0 = window(i)
        vt = jnp.concatenate([vt_ref[0], vt_ref[1]] + [vt_ref[w0 + j] for j in range(4)], axis=1)
        outs = []
        for hh in heads:
            for c in range(n_keys // blk):
                p = jnp.exp2(s_ref[hh, c * blk:(c + 1) * blk, :] - m[hh])
                p_ref[hh, c * blk:(c + 1) * blk, :] = p.astype(jnp.bfloat16)
            vt_h = jnp.concatenate([vt[hh * HEAD_DIM:(hh + 1) * HEAD_DIM], ones], axis=0)
            acc = _dot(vt_h, p_ref[hh])
            denom = acc[HEAD_DIM:HEAD_DIM + 1] + jnp.exp2(sinks[hh] - m[hh])
            outs.append(acc[:HEAD_DIM] * (1.0 / denom))
        return outs

    def store(i, outs):
        o_ref[pl.ds(pl.multiple_of(i * TILE, TILE), TILE), :] = jnp.concatenate(outs, axis=0).T.astype(jnp.bfloat16)

    def step(i_next, s_next, i_cur, s_cur, m_cur):
        m_next, outs = [], []
        for hh in range(2):
            m_next += scores(i_next, s_next, (hh,))
            outs += update(i_cur, s_cur, m_cur, (hh,))
        store(i_cur, outs)
        return tuple(m_next)

    def body(t, m_a):
        m_b = step(2 * t + 1, sb_ref, 2 * t, sa_ref, m_a)
        return step(2 * t + 2, sa_ref, 2 * t + 1, sb_ref, m_b)

    m_a = lax.fori_loop(0, (nt - 1) // 2, body, tuple(scores(0, sa_ref, (0, 1))))
    store(nt - 1, update(nt - 1, sa_ref, m_a, (0, 1)))


def _win_attn(qa, ka, vat, sink, ctx_len):
    n_batch, t_len, _ = qa.shape
    nt = t_len // TILE
    assert ctx_len == TILE and nt >= 4 and nt % 2 == 1, "the tile loop handles two tiles per trip plus a last one"
    return pl.pallas_call(
        functools.partial(_win_body, nt=nt),
        grid=(n_batch, A_HEADS // 2),
        in_specs=[pl.BlockSpec(memory_space=pltpu.SMEM),
                  pl.BlockSpec((None, t_len, LANES), lambda b, p: (b, 0, p)),
                  pl.BlockSpec((None, t_len, LANES), lambda b, p: (b, 0, 0)),
                  pl.BlockSpec((None,) + vat.shape[1:], lambda b, p: (b, 0, 0, 0))],
        out_specs=pl.BlockSpec((None, t_len, LANES), lambda b, p: (b, 0, p)),
        out_shape=jax.ShapeDtypeStruct((n_batch, t_len, A_HEADS * HEAD_DIM), jnp.bfloat16),
        scratch_shapes=[pltpu.VMEM((4, 2 * TILE, TILE), jnp.float32),
                        pltpu.VMEM((2, 3 * TILE, TILE), jnp.float32),
                        pltpu.VMEM((2, 3 * TILE, TILE), jnp.float32),
                        pltpu.VMEM((2, 3 * TILE, TILE), jnp.bfloat16)],
        compiler_params=_cparams(("parallel", "parallel")),
        name="win_attn",
    )(sink, qa, ka, vat)


def _mla_body(q_ref, k_ref, vt_ref, o_ref, sa_ref, sb_ref, sc_ref, p_ref, *, nt, n_chunks):
    sub = MLA_CHUNK // TILE
    blk = LANES
    ones_rows = 16
    n_items = 1 + n_chunks
    bufs = (sa_ref, sb_ref, sc_ref)
    assert n_items % len(bufs) == 0

    def tile_rows(t):
        return pl.ds(pl.multiple_of(t * TILE, TILE), TILE)

    def keys(n):
        return (0, 0, 1) if n == 0 else (TILE + (n - 1) * MLA_CHUNK, 1 + (n - 1) * sub, sub)

    def scores_h(qi, n, hh):
        start, _, n_tiles = keys(n)
        size = n_tiles * TILE
        cols = slice(hh * LANES, (hh + 1) * LANES)
        s = _dot_t(k_ref[start:start + size, cols], q_ref[tile_rows(qi), cols])
        bufs[n % len(bufs)][hh, :size, :] = s
        return jnp.max(s, axis=0, keepdims=True)

    def update_h(st, cm, n, hh):
        _, tile0, n_tiles = keys(n)
        size = n_tiles * TILE
        s_ref = bufs[n % len(bufs)]
        m_prev, acc = st
        m_new = jnp.maximum(m_prev, cm)
        alpha = jnp.exp2(m_prev - m_new)
        for c in range(size // blk):
            p = jnp.exp2(s_ref[hh, c * blk:(c + 1) * blk, :] - m_new)
            p_ref[hh, c * blk:(c + 1) * blk, :] = p.astype(jnp.bfloat16)
        vt = jnp.concatenate([vt_ref[tile0 + c, hh * B_V:(hh + 1) * B_V, :] for c in range(n_tiles)], axis=1)
        vt = jnp.concatenate([vt, jnp.ones((ones_rows, size), jnp.bfloat16)], axis=0)
        return m_new, acc * alpha + _dot(vt, p_ref[hh, :size, :])

    def scores(qi, n):
        return tuple(scores_h(qi, n, hh) for hh in range(2))

    def update(state, cm, n):
        return tuple(update_h(state[hh], cm[hh], n, hh) for hh in range(2))

    def finish(state, qi):
        o_t = jnp.concatenate([acc[:B_V] * (1.0 / acc[B_V:B_V + 1]) for _, acc in state], axis=0)
        o_ref[tile_rows(qi), :] = o_t.T.astype(jnp.bfloat16)

    init = tuple((jnp.full((1, TILE), NEG, jnp.float32), jnp.zeros((B_V + ones_rows, TILE), jnp.float32))
                 for _ in range(2))
    finish(update(init, scores(0, 0), 0), 0)

    def body(qi, cm):
        state = list(init)
        cm = list(cm)
        nxt = jnp.minimum(qi + 1, nt - 1)
        for n in range(n_items):
            for hh in range(2):
                cm_next = scores_h(qi, n + 1, hh) if n + 1 < n_items else scores_h(nxt, 0, hh)
                state[hh] = update_h(state[hh], cm[hh], n, hh)
                cm[hh] = cm_next
        finish(state, qi)
        return tuple(cm)

    lax.fori_loop(1, nt, body, scores(1, 0))


def _mla_attn(qb, kb, vbt, ctx_len):
    n_batch, t_len, _ = qb.shape
    nt = t_len // TILE
    seq = t_len - ctx_len
    assert ctx_len == TILE and seq % MLA_CHUNK == 0 and MLA_CHUNK % TILE == 0
    n_chunks = seq // MLA_CHUNK
    assert (1 + n_chunks) % 3 == 0, "context keys + latent chunks rotate over three score buffers"
    body = functools.partial(_mla_body, nt=nt, n_chunks=n_chunks)
    s_buf = pltpu.VMEM((2, MLA_CHUNK, TILE), jnp.float32)
    return pl.pallas_call(
        body,
        grid=(n_batch, B_HEADS // 2),
        in_specs=[pl.BlockSpec((None, t_len, 2 * LANES), lambda b, p: (b, 0, p)),
                  pl.BlockSpec((None, t_len, 2 * LANES), lambda b, p: (b, 0, p)),
                  pl.BlockSpec((None, nt, 2 * B_V, TILE), lambda b, p: (b, 0, p, 0))],
        out_specs=pl.BlockSpec((None, t_len, LANES), lambda b, p: (b, 0, p)),
        out_shape=jax.ShapeDtypeStruct((n_batch, t_len, B_HEADS * B_V), jnp.bfloat16),
        scratch_shapes=[s_buf, s_buf, s_buf, pltpu.VMEM((2, MLA_CHUNK, TILE), jnp.bfloat16)],
        compiler_params=_cparams(("parallel", "parallel")),
        name="mla_attn",
    )(qb, kb, vbt)


def _nbr_row_valid(case, jr, t):
    rpt = TILE // GRID_W
    lo = (rpt, jr + rpt - C_KH // 2, 2 * rpt - C_KH)[case]
    return lo <= t < lo + C_KH


def _nbr_fill_bias(tz_ref, bias_ref):
    rpt = TILE // GRID_W
    lane_lo = lax.broadcasted_iota(jnp.int32, (GRID_W, LANES), 1) < GRID_W
    neg = jnp.full((GRID_W, LANES), NEG, jnp.float32)
    for hh in range(2):
        bias_ref[3, hh] = jnp.full((3 * TILE, TILE), NEG, jnp.float32)
        for case in range(3):
            for t in range(3 * rpt):
                for jp in range(rpt // 2):
                    ok0 = _nbr_row_valid(case, 2 * jp, t)
                    ok1 = _nbr_row_valid(case, 2 * jp + 1, t)
                    if ok0 or ok1:
                        blk = tz_ref[hh, t - rpt - 2 * jp + C_KH - 2]
                        if not ok1:
                            blk = jnp.where(lane_lo, blk, NEG)
                        elif not ok0:
                            blk = jnp.where(lane_lo, NEG, blk)
                    else:
                        blk = neg
                    bias_ref[case, hh, t * GRID_W:(t + 1) * GRID_W, jp * LANES:(jp + 1) * LANES] = blk


def _nbr_body(q_ref, k_ref, vt_ref, tz_ref, o_ref, bias_ref, sa_ref, sb_ref, p_ref, *, nt):
    _nbr_fill_bias(tz_ref, bias_ref)
    n_keys = 4 * TILE
    ones = jnp.ones((16, n_keys), jnp.bfloat16)
    zero = jnp.zeros((), jnp.bfloat16)
    blk = LANES
    masks = _lane_half_masks(TILE, None)

    def key_tiles(i):
        return 0, jnp.clip(i - 1, 1, nt - 1), i, jnp.clip(i + 1, 1, nt - 1)

    def rows(t):
        return pl.ds(pl.multiple_of(t * TILE, TILE), TILE)

    def scores(i, s_ref, heads):
        case = jnp.where(i == 0, 3, jnp.where(i == 1, 0, jnp.where(i == nt - 1, 2, 1)))
        tc, t0, t1, t2 = key_tiles(i)
        k_ctx = k_ref[rows(tc), :]
        k_loc = jnp.concatenate([k_ref[rows(t0), :], k_ref[rows(t1), :], k_ref[rows(t2), :]], axis=0)
        q2 = q_ref[rows(i), :]
        m = []
        for hh in heads:
            qm = jnp.where(masks[hh], q2, zero)
            s_ctx = _dot_t(k_ctx, qm)
            s_ref[hh, :TILE, :] = s_ctx
            s_loc = _dot_t(k_loc, qm) + bias_ref[case, hh]
            s_ref[hh, TILE:, :] = s_loc
            m.append(jnp.maximum(jnp.max(s_ctx, axis=0, keepdims=True), jnp.max(s_loc, axis=0, keepdims=True)))
        return m

    def update(i, s_ref, m, heads):
        vt = jnp.concatenate([vt_ref[t] for t in key_tiles(i)], axis=1)
        outs = []
        for hh in heads:
            for c in range(n_keys // blk):
                p = jnp.exp2(s_ref[hh, c * blk:(c + 1) * blk, :] - m[hh])
                p_ref[hh, c * blk:(c + 1) * blk, :] = p.astype(jnp.bfloat16)
            vt_h = jnp.concatenate([vt[hh * HEAD_DIM:(hh + 1) * HEAD_DIM], ones], axis=0)
            acc = _dot(vt_h, p_ref[hh])
            outs.append(acc[:HEAD_DIM] * (1.0 / acc[HEAD_DIM:HEAD_DIM + 1]))
        return outs

    def store(i, outs):
        o_ref[rows(i), :] = jnp.concatenate(outs, axis=0).T.astype(jnp.bfloat16)

    def step(i_next, s_next, i_cur, s_cur, m_cur):
        m_next, outs = [], []
        for hh in range(2):
            m_next += scores(i_next, s_next, (hh,))
            outs += update(i_cur, s_cur, m_cur, (hh,))
        store(i_cur, outs)
        return tuple(m_next)

    def body(t, m_a):
        m_b = step(2 * t + 1, sb_ref, 2 * t, sa_ref, m_a)
        return step(2 * t + 2, sa_ref, 2 * t + 1, sb_ref, m_b)

    m_a = lax.fori_loop(0, (nt - 1) // 2, body, tuple(scores(0, sa_ref, (0, 1))))
    store(nt - 1, update(nt - 1, sa_ref, m_a, (0, 1)))


def _nbr_attn(q, k, vt, tz, ctx_len):
    n_batch, t_len, d = q.shape
    nt = t_len // TILE
    assert ctx_len == TILE and TILE % GRID_W == 0 and C_KH == 2 * (TILE // GRID_W)
    assert nt >= 4, "needs distinct first / interior / last latent tiles"
    assert nt % 2 == 1, "the tile loop handles two tiles per trip plus a last one"
    seq = pl.BlockSpec((None, t_len, LANES), lambda b, p: (b, 0, p))
    return pl.pallas_call(
        functools.partial(_nbr_body, nt=nt),
        grid=(n_batch, C_HEADS // 2),
        in_specs=[seq, seq, pl.BlockSpec((None, nt, LANES, TILE), lambda b, p: (b, 0, p, 0)),
                  pl.BlockSpec((2,) + tz.shape[1:], lambda b, p: (p, 0, 0, 0))],
        out_specs=seq,
        out_shape=jax.ShapeDtypeStruct((n_batch, t_len, d), jnp.bfloat16),
        scratch_shapes=[pltpu.VMEM((4, 2, 3 * TILE, TILE), jnp.float32),
                        pltpu.VMEM((2, 4 * TILE, TILE), jnp.float32),
                        pltpu.VMEM((2, 4 * TILE, TILE), jnp.float32),
                        pltpu.VMEM((2, 4 * TILE, TILE), jnp.bfloat16)],
        compiler_params=_cparams(("parallel", "parallel")),
        name="nbr_attn",
    )(q, k, vt, tz)


def _nbr_toeplitz(rpb):
    n_off = 2 * C_KW - 1
    pad = GRID_W - C_KW
    ext = jnp.pad(rpb, ((0, 0), (0, 0), (pad, pad)), mode="edge")[:, :, ::-1]
    assert ext.shape[-1] == n_off + 2 * pad == 2 * GRID_W - 1
    tz = jnp.stack([ext[:, :, GRID_W - 1 - kc:2 * GRID_W - 1 - kc] for kc in range(GRID_W)], axis=2)
    qc = np.arange(GRID_W)[None, :]
    kc = np.arange(GRID_W)[:, None]
    cs = np.clip(qc - C_KW // 2, 0, GRID_W - C_KW)
    col_ok = jnp.asarray((kc >= cs) & (kc < cs + C_KW))
    tz = jnp.where(col_ok, tz * LOG2E, NEG)
    return jnp.concatenate([tz[:, 1:], tz[:, :-1]], axis=-1)


def _out_body(h_ref, mod_ref, o1_ref, o2_ref, gate_ref, w_ref, *rest, final):
    half = D_MODEL // 2
    g = gate_ref[...].astype(jnp.float32)
    og1 = (o1_ref[...].astype(jnp.float32) * g[:, :half]).astype(jnp.bfloat16)
    og2 = (o2_ref[...].astype(jnp.float32) * g[:, half:]).astype(jnp.bfloat16)
    y = _dot(og1, w_ref[:half, :]) + _dot(og2, w_ref[half:, :])
    h_new = h_ref[...] + mod_ref[:, 2 * D_MODEL:] * y
    if final:
        fg_ref, out_ref = rest
        out_ref[...] = _rms(h_new, fg_ref[...])
    else:
        rest[0][...] = h_new


def _out_proj(h, mods_l, o1, o2, gate, w, o2_block, final_g=None):
    n_batch, t_len, d = h.shape
    nt = t_len // TILE
    final = final_g is not None
    off = 1 if final else 0
    row = lambda wd, blk=0: pl.BlockSpec((None, TILE, wd), lambda b, i: (b, i + off, blk))
    mod = pl.BlockSpec((None, 1, 3 * D_MODEL), lambda b, i: (jnp.where(i + off == 0, n_batch, b), 0, 0))
    full = lambda a: pl.BlockSpec(a.shape, lambda b, i: (0,) * a.ndim)
    in_specs = [row(d), mod, row(d // 2), row(d // 2, o2_block), row(d), full(w)]
    args = [h, mods_l, o1, o2, gate, w]
    if final:
        in_specs.append(full(final_g))
        args.append(final_g)
        out_spec = pl.BlockSpec((None, TILE, d), lambda b, i: (b, i, 0))
        out_shape = jax.ShapeDtypeStruct((n_batch, t_len - TILE, d), jnp.float32)
        aliases = {}
    else:
        out_spec = row(d)
        out_shape = jax.ShapeDtypeStruct(h.shape, jnp.float32)
        aliases = {0: 0}
    return pl.pallas_call(
        functools.partial(_out_body, final=final),
        grid=(n_batch, nt - off),
        in_specs=in_specs,
        out_specs=out_spec,
        out_shape=out_shape,
        input_output_aliases=aliases,
        compiler_params=_cparams(("parallel", "arbitrary")),
        name="out_proj_final" if final else "out_proj",
    )(*args)


def _deinterleave(n):
    return np.concatenate([np.arange(0, n, 2), np.arange(1, n, 2)])


def _take_cols(w, cols):
    cols = np.asarray(cols)
    picked = jnp.take(w, jnp.asarray(np.maximum(cols, 0)), axis=1)
    return jnp.where(jnp.asarray(cols >= 0)[None, :], picked, 0.0)


def _even_layout():
    perm = _deinterleave(HEAD_DIM)
    qa0, ka0 = 0, A_HEADS * HEAD_DIM
    va0 = ka0 + A_KV_HEADS * HEAD_DIM
    za0 = va0 + A_KV_HEADS * HEAD_DIM
    cq0 = za0 + A_HEADS * HEAD_DIM
    ckv0 = cq0 + B_Q_LORA
    kr0 = ckv0 + B_KV_LORA
    zb0 = kr0 + B_ROPE
    cols = []
    cols += [qa0 + hd * HEAD_DIM + perm for hd in _A_SLOTS]
    cols += [ka0 + kv * HEAD_DIM + perm for kv in range(A_KV_HEADS)]
    va_cols = va0 + np.arange(A_KV_HEADS * HEAD_DIM)
    cols += [za0 + hd * HEAD_DIM + np.arange(HEAD_DIM) for hd in _A_SLOTS]
    cols += [zb0 + np.arange(B_HEADS * B_V)]
    cols += [cq0 + np.arange(B_Q_LORA), ckv0 + np.arange(B_KV_LORA)]
    cols += [np.full(B_NOPE, -1), kr0 + _deinterleave(B_ROPE), np.full(LANES - B_NOPE - B_ROPE, -1)]
    in_cols = np.concatenate(cols)
    assert in_cols.shape[0] == _AB_COLS

    uq = []
    for hd in range(B_HEADS):
        base = hd * (B_NOPE + B_ROPE)
        uq += [base + np.arange(B_NOPE), base + B_NOPE + _deinterleave(B_ROPE),
               np.full(LANES - B_NOPE - B_ROPE, -1)]
    ukv_k, ukv_v = [], []
    pad = np.full(LANES - B_NOPE, -1)
    for hd in range(B_HEADS):
        base = hd * (B_NOPE + B_V)
        ukv_k += [base + np.arange(B_NOPE), pad]
        ukv_v += [base + B_NOPE + np.arange(B_V)]
    out_rows = np.concatenate([hd * HEAD_DIM + np.arange(HEAD_DIM) for hd in _A_SLOTS]
                              + [A_HEADS * HEAD_DIM + np.arange(B_HEADS * B_V)])
    return in_cols, va_cols, np.concatenate(uq), np.concatenate(ukv_k), np.concatenate(ukv_v), out_rows


def _rope_tables(seq_len, ctx_len):
    t = jnp.arange(seq_len, dtype=jnp.int32)
    row = (t // GRID_W).astype(jnp.float32)
    col = (t % GRID_W).astype(jnp.float32)

    def angles(rot_dim):
        ppa = rot_dim // 4
        inv = ROPE_THETA ** (-jnp.arange(ppa, dtype=jnp.float32) / ppa)
        ang = jnp.concatenate([row[:, None] * inv, col[:, None] * inv], axis=-1)
        return jnp.cos(ang), jnp.sin(ang)

    def with_ctx(c, s1, s2):
        pad = lambda a, v: jnp.concatenate([jnp.full((ctx_len, LANES), v, jnp.float32), a], axis=0)
        return pad(c, 1.0), pad(s1, 0.0), pad(s2, 0.0)

    cos_a, sin_a = angles(HEAD_DIM)
    z32 = jnp.zeros_like(sin_a)
    tabs_a = with_ctx(jnp.tile(cos_a, (1, 4)), jnp.tile(jnp.concatenate([-sin_a, z32], 1), (1, 2)),
                      jnp.tile(jnp.concatenate([z32, sin_a], 1), (1, 2)))
    cos_b, sin_b = angles(B_ROPE)
    one64 = jnp.ones((seq_len, B_NOPE), jnp.float32)
    z64 = jnp.zeros((seq_len, B_NOPE), jnp.float32)
    z16 = jnp.zeros_like(sin_b)
    one32 = jnp.ones((seq_len, LANES - B_NOPE - B_ROPE), jnp.float32)
    z32b = jnp.zeros((seq_len, LANES - B_NOPE - B_ROPE), jnp.float32)
    tabs_b = with_ctx(jnp.concatenate([one64, cos_b, cos_b, one32], 1),
                      jnp.concatenate([z64, -sin_b, z16, z32b], 1),
                      jnp.concatenate([z64, z16, sin_b, z32b], 1))
    return tabs_a, tabs_b


def kernel(x, c, ctx, c_ctx, ada_w, ada_b, norm_g, ab_in_w, ab_out_w, a_sink, b_q_norm_g, b_w_uq,
           b_kv_norm_g, b_w_ukv, c_in_w, c_out_w, c_rpb, final_g):
    n_batch, seq_len, d = x.shape
    ctx_len = ctx.shape[1]
    depth = ada_w.shape[0]
    assert d == D_MODEL and ctx_len == TILE and seq_len % TILE == 0
    assert depth % 2 == 0, "the final norm is fused into the last (odd) layer's output projection"
    bf = jnp.bfloat16

    mod_rows = -(-(n_batch + 1) // 8) * 8
    cc = jnp.concatenate([c, c_ctx[None, :], jnp.zeros((mod_rows - n_batch - 1, d), jnp.float32)], axis=0)
    mods = _mods(cc, ada_w, ada_b).reshape(depth, mod_rows, 1, 3 * d)

    in_cols, va_cols, uq_cols, uk_cols, uv_cols, out_rows = _even_layout()
    tabs_a, tabs_b = _rope_tables(seq_len, ctx_len)
    h = jnp.concatenate([ctx, x], axis=1)
    row2 = lambda v: v.reshape(1, -1)

    out = None
    for layer in range(depth):
        i = layer // 2
        last = layer == depth - 1
        g = row2(norm_g[layer])
        if layer % 2 == 0:
            w_in = _take_cols(ab_in_w[i], in_cols).astype(bf)
            wuq = _take_cols(b_w_uq[i], uq_cols).astype(bf)
            wuk = _take_cols(b_w_ukv[i], uk_cols).astype(bf)
            wvt = _take_cols(b_w_ukv[i], uv_cols).T.astype(bf)
            w_out = jnp.take(ab_out_w[i], jnp.asarray(out_rows), axis=0).astype(bf)
            sink = jnp.take(a_sink[i], jnp.asarray(_A_SLOTS))
            wvat = _take_cols(ab_in_w[i], va_cols).T.astype(bf)
            qa, ka, vat, gate, qb, kb, vbt = _proj_even(h, mods[layer], g, w_in, wvat, row2(b_q_norm_g[i]), wuq,
                                                        row2(b_kv_norm_g[i]), wuk, wvt, tabs_a, tabs_b)
            o1 = _win_attn(qa, ka, vat, sink, ctx_len)
            o2 = _mla_attn(qb, kb, vbt, ctx_len)
            o2_block = 0
        else:
            w = c_in_w[i]
            w_qkz = jnp.concatenate([w[:, :2 * d], w[:, 3 * d:]], axis=1).astype(bf)
            q, k, vt, gate = _proj_odd(h, mods[layer], g, w_qkz, w[:, 2 * d:3 * d].T.astype(bf))
            o1 = o2 = _nbr_attn(q, k, vt, _nbr_toeplitz(c_rpb[i]), ctx_len)
            o2_block = 1
            w_out = c_out_w[i].astype(bf)
        if last:
            out = _out_proj(h, mods[layer], o1, o2, gate, w_out, o2_block, final_g=row2(final_g))
        else:
            h = _out_proj(h, mods[layer], o1, o2, gate, w_out, o2_block)
    return out
```

```python
import functools
import math

import numpy as np
import jax
import jax.numpy as jnp
from jax import lax
from jax.experimental import pallas as pl
from jax.experimental.pallas import tpu as pltpu

D_MODEL = 1024
HEAD_DIM = 64
GRID_W = 64
ROPE_THETA = 10000.0
NORM_EPS = 1e-6
A_HEADS = 8
A_KV_HEADS = 2
A_WINDOW = 128
B_HEADS = 8
B_Q_LORA = 384
B_KV_LORA = 256
B_NOPE = 64
B_ROPE = 32
B_V = 64
C_HEADS = 16
C_KH = 8
C_KW = 16

LANES = 128
TILE = 256
MLA_CHUNK = 1024
TILES_PER_TRIP = 4
LOG2E = math.log2(math.e)
NEG = -1e30
VMEM_LIMIT = 48 * 1024 * 1024

_QA, _KA, _ZA, _ZB, _CQ, _CKV, _KR, _AB_COLS = 0, 512, 640, 1152, 1664, 2048, 2304, 2432
_A_SLOTS = (0, 4, 1, 5, 2, 6, 3, 7)


def _cparams(sem):
    return pltpu.CompilerParams(dimension_semantics=sem, vmem_limit_bytes=VMEM_LIMIT)


def _silu(x):
    return x * (1.0 / (1.0 + jnp.exp(-x)))


def _dot(a, b):
    return jnp.dot(a, b, preferred_element_type=jnp.float32)


def _dot_t(a, b):
    return lax.dot_general(a, b, (((1,), (1,)), ((), ())), preferred_element_type=jnp.float32)


def _mods_body(c_ref, w_ref, b_ref, o_ref):
    sc = _silu(c_ref[...])
    o_ref[...] = jnp.dot(sc, w_ref[...], preferred_element_type=jnp.float32,
                         precision=lax.Precision.HIGHEST) + b_ref[...]


def _mods(cc, ada_w, ada_b):
    depth, d, d3 = ada_w.shape
    rows = cc.shape[0]
    tn = 512
    return pl.pallas_call(
        _mods_body,
        grid=(depth, d3 // tn),
        in_specs=[
            pl.BlockSpec((rows, d), lambda l, j: (0, 0)),
            pl.BlockSpec((None, d, tn), lambda l, j: (l, 0, j)),
            pl.BlockSpec((None, 1, tn), lambda l, j: (l, 0, j)),
        ],
        out_specs=pl.BlockSpec((None, rows, tn), lambda l, j: (l, 0, j)),
        out_shape=jax.ShapeDtypeStruct((depth, rows, d3), jnp.float32),
        compiler_params=_cparams(("arbitrary", "arbitrary")),
        name="mods",
    )(cc, ada_w, ada_b.reshape(depth, 1, d3))


def _mod_spec(n_batch):
    return pl.BlockSpec((None, 1, 3 * D_MODEL), lambda b, i: (jnp.where(i == 0, n_batch, b), 0, 0))


def _norm_mod(h, g, mod):
    y = h * lax.rsqrt(jnp.mean(h * h, axis=-1, keepdims=True) + NORM_EPS) * g
    return y * (1.0 + mod[:, D_MODEL:2 * D_MODEL]) + mod[:, :D_MODEL]


def _rms(x, g):
    return x * lax.rsqrt(jnp.mean(x * x, axis=-1, keepdims=True) + NORM_EPS) * g


def _rope(x, c, s1, s2, half):
    return x * c + pltpu.roll(x, LANES - half, 1) * s1 + pltpu.roll(x, half, 1) * s2


def _proj_even_body(h_ref, mod_ref, g_ref, w_ref, wvat_ref, qg_ref, wuq_ref, kvg_ref, wukv_ref, wvt_ref,
                    ca_ref, sa1_ref, sa2_ref, cb_ref, sb1_ref, sb2_ref,
                    qa_ref, ka_ref, vat_ref, gate_ref, qb_ref, kb_ref, vbt_ref):
    u = _norm_mod(h_ref[...], g_ref[...], mod_ref[...]).astype(jnp.bfloat16)
    ca, sa1, sa2 = ca_ref[...], sa1_ref[...], sa2_ref[...]
    cb, sb1, sb2 = cb_ref[...], sb1_ref[...], sb2_ref[...]

    qk = _dot(u, w_ref[:, _QA:_ZA])
    a_scale = HEAD_DIM ** -0.5 * LOG2E
    for j in range(4):
        blk = _rope(qk[:, j * LANES:(j + 1) * LANES], ca, sa1, sa2, HEAD_DIM // 2)
        qa_ref[:, j * LANES:(j + 1) * LANES] = (blk * a_scale).astype(jnp.bfloat16)
    ka_ref[...] = _rope(qk[:, _KA:_ZA], ca, sa1, sa2, HEAD_DIM // 2).astype(jnp.bfloat16)
    vat = _dot_t(wvat_ref[...], u).astype(jnp.bfloat16)
    for j in range(TILE // LANES):
        vat_ref[j] = vat[:, j * LANES:(j + 1) * LANES]
    gate_ref[...] = _silu(_dot(u, w_ref[:, _ZA:_CQ])).astype(jnp.bfloat16)

    lat = _dot(u, w_ref[:, _CQ:_AB_COLS])
    cq = _rms(lat[:, :B_Q_LORA], qg_ref[...]).astype(jnp.bfloat16)
    ckv = _rms(lat[:, B_Q_LORA:B_Q_LORA + B_KV_LORA], kvg_ref[...]).astype(jnp.bfloat16)
    kr = _rope(lat[:, B_Q_LORA + B_KV_LORA:], cb, sb1, sb2, B_ROPE // 2)

    qb = _dot(cq, wuq_ref[...])
    b_scale = (B_NOPE + B_ROPE) ** -0.5 * LOG2E
    kb = _dot(ckv, wukv_ref[...])
    for hd in range(B_HEADS):
        sl = slice(hd * LANES, (hd + 1) * LANES)
        qb_ref[:, sl] = (_rope(qb[:, sl], cb, sb1, sb2, B_ROPE // 2) * b_scale).astype(jnp.bfloat16)
        kb_ref[:, sl] = (kb[:, sl] + kr).astype(jnp.bfloat16)
    vbt_ref[...] = _dot_t(wvt_ref[...], ckv).astype(jnp.bfloat16)


def _proj_even(h, mods_l, g, w_in, wvat, qg, wuq, kvg, wukv, wvt, tabs_a, tabs_b):
    n_batch, t_len, d = h.shape
    nt = t_len // TILE
    row = lambda w: pl.BlockSpec((None, TILE, w), lambda b, i: (b, i, 0))
    full = lambda a: pl.BlockSpec(a.shape, lambda b, i: (0,) * a.ndim)
    tab = pl.BlockSpec((TILE, LANES), lambda b, i: (i, 0))
    bf = jnp.bfloat16
    sds = lambda *shape: jax.ShapeDtypeStruct((n_batch,) + shape, bf)
    n_vt = B_HEADS * B_V
    per_tile = TILE // LANES
    return pl.pallas_call(
        _proj_even_body,
        grid=(n_batch, nt),
        in_specs=[row(d), _mod_spec(n_batch), full(g), full(w_in), full(wvat), full(qg), full(wuq), full(kvg),
                  full(wukv), full(wvt)] + [tab] * 6,
        out_specs=[row(512), row(LANES),
                   pl.BlockSpec((None, per_tile, LANES, LANES), lambda b, i: (b, i, 0, 0)),
                   row(d), row(d), row(d),
                   pl.BlockSpec((None, None, n_vt, TILE), lambda b, i: (b, i, 0, 0))],
        out_shape=[sds(t_len, 512), sds(t_len, LANES), sds(t_len // LANES, LANES, LANES),
                   sds(t_len, d), sds(t_len, d), sds(t_len, d), sds(nt, n_vt, TILE)],
        compiler_params=_cparams(("parallel", "arbitrary")),
        name="proj_even",
    )(h, mods_l, g, w_in, wvat, qg, wuq, kvg, wukv, wvt, *tabs_a, *tabs_b)


def _proj_odd_body(h_ref, mod_ref, g_ref, w_ref, wvt_ref, q_ref, k_ref, vt_ref, gate_ref):
    u = _norm_mod(h_ref[...], g_ref[...], mod_ref[...]).astype(jnp.bfloat16)
    d = D_MODEL
    q_ref[...] = (_dot(u, w_ref[:, :d]) * (HEAD_DIM ** -0.5 * LOG2E)).astype(jnp.bfloat16)
    k_ref[...] = _dot(u, w_ref[:, d:2 * d]).astype(jnp.bfloat16)
    vt_ref[...] = _dot_t(wvt_ref[...], u).astype(jnp.bfloat16)
    gate_ref[...] = _silu(_dot(u, w_ref[:, 2 * d:])).astype(jnp.bfloat16)


def _proj_odd(h, mods_l, g, w_qkz, wvt):
    n_batch, t_len, d = h.shape
    nt = t_len // TILE
    row = pl.BlockSpec((None, TILE, d), lambda b, i: (b, i, 0))
    row_t = pl.BlockSpec((None, None, d, TILE), lambda b, i: (b, i, 0, 0))
    full = lambda a: pl.BlockSpec(a.shape, lambda b, i: (0,) * a.ndim)
    out = jax.ShapeDtypeStruct((n_batch, t_len, d), jnp.bfloat16)
    return pl.pallas_call(
        _proj_odd_body,
        grid=(n_batch, nt),
        in_specs=[row, _mod_spec(n_batch), full(g), full(w_qkz), full(wvt)],
        out_specs=[row, row, row_t, row],
        out_shape=[out, out, jax.ShapeDtypeStruct((n_batch, nt, d, TILE), jnp.bfloat16), out],
        compiler_params=_cparams(("parallel", "arbitrary")),
        name="proj_odd",
    )(h, mods_l, g, w_qkz, wvt)


def _lane_half_masks(rows, dtype):
    lane = lax.broadcasted_iota(jnp.int32, (rows, LANES), 1)
    lo = lane < HEAD_DIM
    return lo, jnp.logical_not(lo)


def _win_fill_mask(mask_ref):
    r = lax.broadcasted_iota(jnp.int32, (2 * TILE, TILE), 0)
    c = lax.broadcasted_iota(jnp.int32, (2 * TILE, TILE), 1)
    for case, (back, first) in enumerate(((A_WINDOW, A_WINDOW), (A_WINDOW, 0), (2 * A_WINDOW, 0))):
        ok = (jnp.abs(r - back - c) <= A_WINDOW) & (r >= first)
        mask_ref[case] = jnp.where(ok, 0.0, NEG)
    mask_ref[3] = jnp.full((2 * TILE, TILE), NEG, jnp.float32)


def _win_body(sink_ref, q_ref, k_ref, vt_ref, o_ref, mask_ref, sa_ref, sb_ref, p_ref, *, nt):
    assert TILE == 2 * A_WINDOW
    pair = pl.program_id(1)
    _win_fill_mask(mask_ref)
    n_keys = 3 * TILE
    n128 = nt * (TILE // LANES)
    ones = jnp.ones((16, n_keys), jnp.bfloat16)
    zero = jnp.zeros((), jnp.bfloat16)
    blk = LANES
    masks = _lane_half_masks(TILE, None)
    sinks = tuple(sink_ref[2 * pair + hh] * LOG2E for hh in range(2))

    def window(i):
        return jnp.clip(2 * i - 1, 0, n128 - 4)

    def scores(i, s_ref, heads):
        case = jnp.where(i == 0, 3, jnp.where(i == 1, 0, jnp.where(i == nt - 1, 2, 1)))
        k_ctx = k_ref[:TILE, :]
        k_loc = k_ref[pl.ds(pl.multiple_of(window(i) * LANES, LANES), 2 * TILE), :]
        q2 = q_ref[pl.ds(pl.multiple_of(i * TILE, TILE), TILE), :]
        m = []
        for hh in heads:
            qm = jnp.where(masks[hh], q2, zero)
            s_ctx = _dot_t(k_ctx, qm)
            s_ref[hh, :TILE, :] = s_ctx
            s_loc = _dot_t(k_loc, qm) + mask_ref[case]
            s_ref[hh, TILE:, :] = s_loc
            m.append(jnp.maximum(jnp.maximum(jnp.max(s_ctx, axis=0, keepdims=True),
                                             jnp.max(s_loc, axis=0, keepdims=True)), sinks[hh]))
        return m

    def update(i, s_ref, m, heads):
        w0 = window(i)
        vt = jnp.concatenate([vt_ref[0], vt_ref[1]] + [vt_ref[w0 + j] for j in range(4)], axis=1)
        outs = []
        for hh in heads:
            for c in range(n_keys // blk):
                p = jnp.exp2(s_ref[hh, c * blk:(c + 1) * blk, :] - m[hh])
                p_ref[hh, c * blk:(c + 1) * blk, :] = p.astype(jnp.bfloat16)
            vt_h = jnp.concatenate([vt[hh * HEAD_DIM:(hh + 1) * HEAD_DIM], ones], axis=0)
            acc = _dot(vt_h, p_ref[hh])
            denom = acc[HEAD_DIM:HEAD_DIM + 1] + jnp.exp2(sinks[hh] - m[hh])
            outs.append(acc[:HEAD_DIM] * (1.0 / denom))
        return outs

    def store(i, outs):
        o_ref[pl.ds(pl.multiple_of(i * TILE, TILE), TILE), :] = jnp.concatenate(outs, axis=0).T.astype(jnp.bfloat16)

    def step(i_next, s_next, i_cur, s_cur, m_cur):
        m_next, outs = [], []
        for hh in range(2):
            m_next += scores(i_next, s_next, (hh,))
            outs += update(i_cur, s_cur, m_cur, (hh,))
        store(i_cur, outs)
        return tuple(m_next)

    def body(t, m_a):
        for j in range(0, TILES_PER_TRIP, 2):
            i = TILES_PER_TRIP * t + j
            m_b = step(i + 1, sb_ref, i, sa_ref, m_a)
            m_a = step(i + 2, sa_ref, i + 1, sb_ref, m_b)
        return m_a

    m_a = lax.fori_loop(0, (nt - 1) // TILES_PER_TRIP, body, tuple(scores(0, sa_ref, (0, 1))))
    store(nt - 1, update(nt - 1, sa_ref, m_a, (0, 1)))


def _win_attn(qa, ka, vat, sink, ctx_len):
    n_batch, t_len, _ = qa.shape
    nt = t_len // TILE
    assert ctx_len == TILE and nt >= 4 and (nt - 1) % TILES_PER_TRIP == 0, "whole loop trips plus a last tile"
    return pl.pallas_call(
        functools.partial(_win_body, nt=nt),
        grid=(n_batch, A_HEADS // 2),
        in_specs=[pl.BlockSpec(memory_space=pltpu.SMEM),
                  pl.BlockSpec((None, t_len, LANES), lambda b, p: (b, 0, p)),
                  pl.BlockSpec((None, t_len, LANES), lambda b, p: (b, 0, 0)),
                  pl.BlockSpec((None,) + vat.shape[1:], lambda b, p: (b, 0, 0, 0))],
        out_specs=pl.BlockSpec((None, t_len, LANES), lambda b, p: (b, 0, p)),
        out_shape=jax.ShapeDtypeStruct((n_batch, t_len, A_HEADS * HEAD_DIM), jnp.bfloat16),
        scratch_shapes=[pltpu.VMEM((4, 2 * TILE, TILE), jnp.float32),
                        pltpu.VMEM((2, 3 * TILE, TILE), jnp.float32),
                        pltpu.VMEM((2, 3 * TILE, TILE), jnp.float32),
                        pltpu.VMEM((2, 3 * TILE, TILE), jnp.bfloat16)],
        compiler_params=_cparams(("parallel", "parallel")),
        name="win_attn",
    )(sink, qa, ka, vat)


def _mla_body(q_ref, k_ref, vt_ref, o_ref, sa_ref, sb_ref, sc_ref, p_ref, *, nt, n_chunks):
    sub = MLA_CHUNK // TILE
    blk = LANES
    ones_rows = 16
    n_items = 1 + n_chunks
    bufs = (sa_ref, sb_ref, sc_ref)
    assert n_items % len(bufs) == 0

    def tile_rows(t):
        return pl.ds(pl.multiple_of(t * TILE, TILE), TILE)

    def keys(n):
        return (0, 0, 1) if n == 0 else (TILE + (n - 1) * MLA_CHUNK, 1 + (n - 1) * sub, sub)

    def scores_h(qi, n, hh):
        start, _, n_tiles = keys(n)
        size = n_tiles * TILE
        cols = slice(hh * LANES, (hh + 1) * LANES)
        s = _dot_t(k_ref[start:start + size, cols], q_ref[tile_rows(qi), cols])
        bufs[n % len(bufs)][hh, :size, :] = s
        return jnp.max(s, axis=0, keepdims=True)

    def update_h(st, cm, n, hh):
        _, tile0, n_tiles = keys(n)
        size = n_tiles * TILE
        s_ref = bufs[n % len(bufs)]
        m_prev, acc = st
        m_new = jnp.maximum(m_prev, cm)
        alpha = jnp.exp2(m_prev - m_new)
        for c in range(size // blk):
            p = jnp.exp2(s_ref[hh, c * blk:(c + 1) * blk, :] - m_new)
            p_ref[hh, c * blk:(c + 1) * blk, :] = p.astype(jnp.bfloat16)
        vt = jnp.concatenate([vt_ref[tile0 + c, hh * B_V:(hh + 1) * B_V, :] for c in range(n_tiles)], axis=1)
        vt = jnp.concatenate([vt, jnp.ones((ones_rows, size), jnp.bfloat16)], axis=0)
        return m_new, acc * alpha + _dot(vt, p_ref[hh, :size, :])

    def scores(qi, n):
        return tuple(scores_h(qi, n, hh) for hh in range(2))

    def update(state, cm, n):
        return tuple(update_h(state[hh], cm[hh], n, hh) for hh in range(2))

    def finish(state, qi):
        o_t = jnp.concatenate([acc[:B_V] * (1.0 / acc[B_V:B_V + 1]) for _, acc in state], axis=0)
        o_ref[tile_rows(qi), :] = o_t.T.astype(jnp.bfloat16)

    init = tuple((jnp.full((1, TILE), NEG, jnp.float32), jnp.zeros((B_V + ones_rows, TILE), jnp.float32))
                 for _ in range(2))
    finish(update(init, scores(0, 0), 0), 0)

    def body(qi, cm):
        state = list(init)
        cm = list(cm)
        nxt = jnp.minimum(qi + 1, nt - 1)
        for n in range(n_items):
            for hh in range(2):
                cm_next = scores_h(qi, n + 1, hh) if n + 1 < n_items else scores_h(nxt, 0, hh)
                state[hh] = update_h(state[hh], cm[hh], n, hh)
                cm[hh] = cm_next
        finish(state, qi)
        return tuple(cm)

    lax.fori_loop(1, nt, body, scores(1, 0))


def _mla_attn(qb, kb, vbt, ctx_len):
    n_batch, t_len, _ = qb.shape
    nt = t_len // TILE
    seq = t_len - ctx_len
    assert ctx_len == TILE and seq % MLA_CHUNK == 0 and MLA_CHUNK % TILE == 0
    n_chunks = seq // MLA_CHUNK
    assert (1 + n_chunks) % 3 == 0, "context keys + latent chunks rotate over three score buffers"
    body = functools.partial(_mla_body, nt=nt, n_chunks=n_chunks)
    s_buf = pltpu.VMEM((2, MLA_CHUNK, TILE), jnp.float32)
    return pl.pallas_call(
        body,
        grid=(n_batch, B_HEADS // 2),
        in_specs=[pl.BlockSpec((None, t_len, 2 * LANES), lambda b, p: (b, 0, p)),
                  pl.BlockSpec((None, t_len, 2 * LANES), lambda b, p: (b, 0, p)),
                  pl.BlockSpec((None, nt, 2 * B_V, TILE), lambda b, p: (b, 0, p, 0))],
        out_specs=pl.BlockSpec((None, t_len, LANES), lambda b, p: (b, 0, p)),
        out_shape=jax.ShapeDtypeStruct((n_batch, t_len, B_HEADS * B_V), jnp.bfloat16),
        scratch_shapes=[s_buf, s_buf, s_buf, pltpu.VMEM((2, MLA_CHUNK, TILE), jnp.bfloat16)],
        compiler_params=_cparams(("parallel", "parallel")),
        name="mla_attn",
    )(qb, kb, vbt)


def _nbr_row_valid(case, jr, t):
    rpt = TILE // GRID_W
    lo = (rpt, jr + rpt - C_KH // 2, 2 * rpt - C_KH)[case]
    return lo <= t < lo + C_KH


def _nbr_fill_bias(tz_ref, bias_ref):
    rpt = TILE // GRID_W
    lane_lo = lax.broadcasted_iota(jnp.int32, (GRID_W, LANES), 1) < GRID_W
    neg = jnp.full((GRID_W, LANES), NEG, jnp.float32)
    for hh in range(2):
        bias_ref[3, hh] = jnp.full((3 * TILE, TILE), NEG, jnp.float32)
        for case in range(3):
            for t in range(3 * rpt):
                for jp in range(rpt // 2):
                    ok0 = _nbr_row_valid(case, 2 * jp, t)
                    ok1 = _nbr_row_valid(case, 2 * jp + 1, t)
                    if ok0 or ok1:
                        blk = tz_ref[hh, t - rpt - 2 * jp + C_KH - 2]
                        if not ok1:
                            blk = jnp.where(lane_lo, blk, NEG)
                        elif not ok0:
                            blk = jnp.where(lane_lo, NEG, blk)
                    else:
                        blk = neg
                    bias_ref[case, hh, t * GRID_W:(t + 1) * GRID_W, jp * LANES:(jp + 1) * LANES] = blk


def _nbr_body(q_ref, k_ref, vt_ref, tz_ref, o_ref, bias_ref, sa_ref, sb_ref, p_ref, *, nt):
    _nbr_fill_bias(tz_ref, bias_ref)
    n_keys = 4 * TILE
    ones = jnp.ones((16, n_keys), jnp.bfloat16)
    zero = jnp.zeros((), jnp.bfloat16)
    blk = LANES
    masks = _lane_half_masks(TILE, None)

    def key_tiles(i):
        return 0, jnp.clip(i - 1, 1, nt - 1), i, jnp.clip(i + 1, 1, nt - 1)

    def rows(t):
        return pl.ds(pl.multiple_of(t * TILE, TILE), TILE)

    def scores(i, s_ref, heads):
        case = jnp.where(i == 0, 3, jnp.where(i == 1, 0, jnp.where(i == nt - 1, 2, 1)))
        tc, t0, t1, t2 = key_tiles(i)
        k_ctx = k_ref[rows(tc), :]
        k_loc = jnp.concatenate([k_ref[rows(t0), :], k_ref[rows(t1), :], k_ref[rows(t2), :]], axis=0)
        q2 = q_ref[rows(i), :]
        m = []
        for hh in heads:
            qm = jnp.where(masks[hh], q2, zero)
            s_ctx = _dot_t(k_ctx, qm)
            s_ref[hh, :TILE, :] = s_ctx
            s_loc = _dot_t(k_loc, qm) + bias_ref[case, hh]
            s_ref[hh, TILE:, :] = s_loc
            m.append(jnp.maximum(jnp.max(s_ctx, axis=0, keepdims=True), jnp.max(s_loc, axis=0, keepdims=True)))
        return m

    def update(i, s_ref, m, heads):
        vt = jnp.concatenate([vt_ref[t] for t in key_tiles(i)], axis=1)
        outs = []
        for hh in heads:
            for c in range(n_keys // blk):
                p = jnp.exp2(s_ref[hh, c * blk:(c + 1) * blk, :] - m[hh])
                p_ref[hh, c * blk:(c + 1) * blk, :] = p.astype(jnp.bfloat16)
            vt_h = jnp.concatenate([vt[hh * HEAD_DIM:(hh + 1) * HEAD_DIM], ones], axis=0)
            acc = _dot(vt_h, p_ref[hh])
            outs.append(acc[:HEAD_DIM] * (1.0 / acc[HEAD_DIM:HEAD_DIM + 1]))
        return outs

    def store(i, outs):
        o_ref[rows(i), :] = jnp.concatenate(outs, axis=0).T.astype(jnp.bfloat16)

    def step(i_next, s_next, i_cur, s_cur, m_cur):
        m_next, outs = [], []
        for hh in range(2):
            m_next += scores(i_next, s_next, (hh,))
            outs += update(i_cur, s_cur, m_cur, (hh,))
        store(i_cur, outs)
        return tuple(m_next)

    def body(t, m_a):
        for j in range(0, TILES_PER_TRIP, 2):
            i = TILES_PER_TRIP * t + j
            m_b = step(i + 1, sb_ref, i, sa_ref, m_a)
            m_a = step(i + 2, sa_ref, i + 1, sb_ref, m_b)
        return m_a

    m_a = lax.fori_loop(0, (nt - 1) // TILES_PER_TRIP, body, tuple(scores(0, sa_ref, (0, 1))))
    store(nt - 1, update(nt - 1, sa_ref, m_a, (0, 1)))


def _nbr_attn(q, k, vt, tz, ctx_len):
    n_batch, t_len, d = q.shape
    nt = t_len // TILE
    assert ctx_len == TILE and TILE % GRID_W == 0 and C_KH == 2 * (TILE // GRID_W)
    assert nt >= 4, "needs distinct first / interior / last latent tiles"
    assert (nt - 1) % TILES_PER_TRIP == 0, "whole loop trips plus a last tile"
    seq = pl.BlockSpec((None, t_len, LANES), lambda b, p: (b, 0, p))
    return pl.pallas_call(
        functools.partial(_nbr_body, nt=nt),
        grid=(n_batch, C_HEADS // 2),
        in_specs=[seq, seq, pl.BlockSpec((None, nt, LANES, TILE), lambda b, p: (b, 0, p, 0)),
                  pl.BlockSpec((2,) + tz.shape[1:], lambda b, p: (p, 0, 0, 0))],
        out_specs=seq,
        out_shape=jax.ShapeDtypeStruct((n_batch, t_len, d), jnp.bfloat16),
        scratch_shapes=[pltpu.VMEM((4, 2, 3 * TILE, TILE), jnp.float32),
                        pltpu.VMEM((2, 4 * TILE, TILE), jnp.float32),
                        pltpu.VMEM((2, 4 * TILE, TILE), jnp.float32),
                        pltpu.VMEM((2, 4 * TILE, TILE), jnp.bfloat16)],
        compiler_params=_cparams(("parallel", "parallel")),
        name="nbr_attn",
    )(q, k, vt, tz)


def _nbr_toeplitz(rpb):
    n_off = 2 * C_KW - 1
    pad = GRID_W - C_KW
    ext = jnp.pad(rpb, ((0, 0), (0, 0), (pad, pad)), mode="edge")[:, :, ::-1]
    assert ext.shape[-1] == n_off + 2 * pad == 2 * GRID_W - 1
    n_h, n_r, length = ext.shape
    flat = jnp.tile(jnp.pad(ext, ((0, 0), (0, 0), (0, 1))), (1, 1, GRID_W))
    tz = flat[:, :, GRID_W - 1:GRID_W - 1 + GRID_W * length].reshape(n_h, n_r, GRID_W, length)[..., :GRID_W]
    qc = np.arange(GRID_W)[None, :]
    kc = np.arange(GRID_W)[:, None]
    cs = np.clip(qc - C_KW // 2, 0, GRID_W - C_KW)
    col_ok = jnp.asarray((kc >= cs) & (kc < cs + C_KW))
    tz = jnp.where(col_ok, tz * LOG2E, NEG)
    return jnp.concatenate([tz[:, 1:], tz[:, :-1]], axis=-1)


def _out_body(h_ref, mod_ref, o1_ref, o2_ref, gate_ref, w_ref, *rest, final):
    half = D_MODEL // 2
    g = gate_ref[...].astype(jnp.float32)
    og1 = (o1_ref[...].astype(jnp.float32) * g[:, :half]).astype(jnp.bfloat16)
    og2 = (o2_ref[...].astype(jnp.float32) * g[:, half:]).astype(jnp.bfloat16)
    y = _dot(og1, w_ref[:half, :]) + _dot(og2, w_ref[half:, :])
    h_new = h_ref[...] + mod_ref[:, 2 * D_MODEL:] * y
    if final:
        fg_ref, out_ref = rest
        out_ref[...] = _rms(h_new, fg_ref[...])
    else:
        rest[0][...] = h_new


def _out_proj(h, mods_l, o1, o2, gate, w, o2_block, final_g=None):
    n_batch, t_len, d = h.shape
    nt = t_len // TILE
    final = final_g is not None
    off = 1 if final else 0
    row = lambda wd, blk=0: pl.BlockSpec((None, TILE, wd), lambda b, i: (b, i + off, blk))
    mod = pl.BlockSpec((None, 1, 3 * D_MODEL), lambda b, i: (jnp.where(i + off == 0, n_batch, b), 0, 0))
    full = lambda a: pl.BlockSpec(a.shape, lambda b, i: (0,) * a.ndim)
    in_specs = [row(d), mod, row(d // 2), row(d // 2, o2_block), row(d), full(w)]
    args = [h, mods_l, o1, o2, gate, w]
    if final:
        in_specs.append(full(final_g))
        args.append(final_g)
        out_spec = pl.BlockSpec((None, TILE, d), lambda b, i: (b, i, 0))
        out_shape = jax.ShapeDtypeStruct((n_batch, t_len - TILE, d), jnp.float32)
        aliases = {}
    else:
        out_spec = row(d)
        out_shape = jax.ShapeDtypeStruct(h.shape, jnp.float32)
        aliases = {0: 0}
    return pl.pallas_call(
        functools.partial(_out_body, final=final),
        grid=(n_batch, nt - off),
        in_specs=in_specs,
        out_specs=out_spec,
        out_shape=out_shape,
        input_output_aliases=aliases,
        compiler_params=_cparams(("parallel", "arbitrary")),
        name="out_proj_final" if final else "out_proj",
    )(*args)


def _deinterleave(n):
    return np.concatenate([np.arange(0, n, 2), np.arange(1, n, 2)])


def _take_cols(w, cols):
    cols = np.asarray(cols)
    picked = jnp.take(w, jnp.asarray(np.maximum(cols, 0)), axis=1)
    return jnp.where(jnp.asarray(cols >= 0)[None, :], picked, 0.0)


def _even_layout():
    perm = _deinterleave(HEAD_DIM)
    qa0, ka0 = 0, A_HEADS * HEAD_DIM
    va0 = ka0 + A_KV_HEADS * HEAD_DIM
    za0 = va0 + A_KV_HEADS * HEAD_DIM
    cq0 = za0 + A_HEADS * HEAD_DIM
    ckv0 = cq0 + B_Q_LORA
    kr0 = ckv0 + B_KV_LORA
    zb0 = kr0 + B_ROPE
    cols = []
    cols += [qa0 + hd * HEAD_DIM + perm for hd in _A_SLOTS]
    cols += [ka0 + kv * HEAD_DIM + perm for kv in range(A_KV_HEADS)]
    va_cols = va0 + np.arange(A_KV_HEADS * HEAD_DIM)
    cols += [za0 + hd * HEAD_DIM + np.arange(HEAD_DIM) for hd in _A_SLOTS]
    cols += [zb0 + np.arange(B_HEADS * B_V)]
    cols += [cq0 + np.arange(B_Q_LORA), ckv0 + np.arange(B_KV_LORA)]
    cols += [np.full(B_NOPE, -1), kr0 + _deinterleave(B_ROPE), np.full(LANES - B_NOPE - B_ROPE, -1)]
    in_cols = np.concatenate(cols)
    assert in_cols.shape[0] == _AB_COLS

    uq = []
    for hd in range(B_HEADS):
        base = hd * (B_NOPE + B_ROPE)
        uq += [base + np.arange(B_NOPE), base + B_NOPE + _deinterleave(B_ROPE),
               np.full(LANES - B_NOPE - B_ROPE, -1)]
    ukv_k, ukv_v = [], []
    pad = np.full(LANES - B_NOPE, -1)
    for hd in range(B_HEADS):
        base = hd * (B_NOPE + B_V)
        ukv_k += [base + np.arange(B_NOPE), pad]
        ukv_v += [base + B_NOPE + np.arange(B_V)]
    out_rows = np.concatenate([hd * HEAD_DIM + np.arange(HEAD_DIM) for hd in _A_SLOTS]
                              + [A_HEADS * HEAD_DIM + np.arange(B_HEADS * B_V)])
    return in_cols, va_cols, np.concatenate(uq), np.concatenate(ukv_k), np.concatenate(ukv_v), out_rows


def _rope_tables(seq_len, ctx_len):
    t = jnp.arange(seq_len, dtype=jnp.int32)
    row = (t // GRID_W).astype(jnp.float32)
    col = (t % GRID_W).astype(jnp.float32)

    def angles(rot_dim):
        ppa = rot_dim // 4
        inv = ROPE_THETA ** (-jnp.arange(ppa, dtype=jnp.float32) / ppa)
        ang = jnp.concatenate([row[:, None] * inv, col[:, None] * inv], axis=-1)
        return jnp.cos(ang), jnp.sin(ang)

    def with_ctx(c, s1, s2):
        pad = lambda a, v: jnp.concatenate([jnp.full((ctx_len, LANES), v, jnp.float32), a], axis=0)
        return pad(c, 1.0), pad(s1, 0.0), pad(s2, 0.0)

    cos_a, sin_a = angles(HEAD_DIM)
    z32 = jnp.zeros_like(sin_a)
    tabs_a = with_ctx(jnp.tile(cos_a, (1, 4)), jnp.tile(jnp.concatenate([-sin_a, z32], 1), (1, 2)),
                      jnp.tile(jnp.concatenate([z32, sin_a], 1), (1, 2)))
    cos_b, sin_b = angles(B_ROPE)
    one64 = jnp.ones((seq_len, B_NOPE), jnp.float32)
    z64 = jnp.zeros((seq_len, B_NOPE), jnp.float32)
    z16 = jnp.zeros_like(sin_b)
    one32 = jnp.ones((seq_len, LANES - B_NOPE - B_ROPE), jnp.float32)
    z32b = jnp.zeros((seq_len, LANES - B_NOPE - B_ROPE), jnp.float32)
    tabs_b = with_ctx(jnp.concatenate([one64, cos_b, cos_b, one32], 1),
                      jnp.concatenate([z64, -sin_b, z16, z32b], 1),
                      jnp.concatenate([z64, z16, sin_b, z32b], 1))
    return tabs_a, tabs_b


def kernel(x, c, ctx, c_ctx, ada_w, ada_b, norm_g, ab_in_w, ab_out_w, a_sink, b_q_norm_g, b_w_uq,
           b_kv_norm_g, b_w_ukv, c_in_w, c_out_w, c_rpb, final_g):
    n_batch, seq_len, d = x.shape
    ctx_len = ctx.shape[1]
    depth = ada_w.shape[0]
    assert d == D_MODEL and ctx_len == TILE and seq_len % TILE == 0
    assert depth % 2 == 0, "the final norm is fused into the last (odd) layer's output projection"
    bf = jnp.bfloat16

    mod_rows = -(-(n_batch + 1) // 8) * 8
    cc = jnp.concatenate([c, c_ctx[None, :], jnp.zeros((mod_rows - n_batch - 1, d), jnp.float32)], axis=0)
    mods = _mods(cc, ada_w, ada_b).reshape(depth, mod_rows, 1, 3 * d)

    in_cols, va_cols, uq_cols, uk_cols, uv_cols, out_rows = _even_layout()
    tabs_a, tabs_b = _rope_tables(seq_len, ctx_len)
    h = jnp.concatenate([ctx, x], axis=1)
    row2 = lambda v: v.reshape(1, -1)

    out = None
    for layer in range(depth):
        i = layer // 2
        last = layer == depth - 1
        g = row2(norm_g[layer])
        if layer % 2 == 0:
            w_in = _take_cols(ab_in_w[i], in_cols).astype(bf)
            wuq = _take_cols(b_w_uq[i], uq_cols).astype(bf)
            wuk = _take_cols(b_w_ukv[i], uk_cols).astype(bf)
            wvt = _take_cols(b_w_ukv[i], uv_cols).T.astype(bf)
            w_out = jnp.take(ab_out_w[i], jnp.asarray(out_rows), axis=0).astype(bf)
            sink = jnp.take(a_sink[i], jnp.asarray(_A_SLOTS))
            wvat = _take_cols(ab_in_w[i], va_cols).T.astype(bf)
            qa, ka, vat, gate, qb, kb, vbt = _proj_even(h, mods[layer], g, w_in, wvat, row2(b_q_norm_g[i]), wuq,
                                                        row2(b_kv_norm_g[i]), wuk, wvt, tabs_a, tabs_b)
            o1 = _win_attn(qa, ka, vat, sink, ctx_len)
            o2 = _mla_attn(qb, kb, vbt, ctx_len)
            o2_block = 0
        else:
            w = c_in_w[i]
            w_qkz = jnp.concatenate([w[:, :2 * d], w[:, 3 * d:]], axis=1).astype(bf)
            q, k, vt, gate = _proj_odd(h, mods[layer], g, w_qkz, w[:, 2 * d:3 * d].T.astype(bf))
            o1 = o2 = _nbr_attn(q, k, vt, _nbr_toeplitz(c_rpb[i]), ctx_len)
            o2_block = 1
            w_out = c_out_w[i].astype(bf)
        if last:
            out = _out_proj(h, mods[layer], o1, o2, gate, w_out, o2_block, final_g=row2(final_g))
        else:
            h = _out_proj(h, mods[layer], o1, o2, gate, w_out, o2_block)
    return out
```

```python
import functools
import math

import numpy as np
import jax
import jax.numpy as jnp
from jax import lax
from jax.experimental import pallas as pl
from jax.experimental.pallas import tpu as pltpu

D_MODEL = 1024
HEAD_DIM = 64
GRID_W = 64
ROPE_THETA = 10000.0
NORM_EPS = 1e-6
A_HEADS = 8
A_KV_HEADS = 2
A_WINDOW = 128
B_HEADS = 8
B_Q_LORA = 384
B_KV_LORA = 256
B_NOPE = 64
B_ROPE = 32
B_V = 64
C_HEADS = 16
C_KH = 8
C_KW = 16

LANES = 128
TILE = 256
MLA_CHUNK = 1024
TILES_PER_TRIP = 4
LOG2E = math.log2(math.e)
NEG = -1e30
VMEM_LIMIT = 48 * 1024 * 1024

_QA, _KA, _ZA, _ZB, _CQ, _CKV, _KR, _AB_COLS = 0, 512, 640, 1152, 1664, 2048, 2304, 2432
_A_SLOTS = (0, 4, 1, 5, 2, 6, 3, 7)


def _cparams(sem):
    return pltpu.CompilerParams(dimension_semantics=sem, vmem_limit_bytes=VMEM_LIMIT)


def _silu(x):
    return x * (1.0 / (1.0 + jnp.exp(-x)))


def _dot(a, b):
    return jnp.dot(a, b, preferred_element_type=jnp.float32)


def _dot_t(a, b):
    return lax.dot_general(a, b, (((1,), (1,)), ((), ())), preferred_element_type=jnp.float32)


def _mods_body(c_ref, w_ref, b_ref, o_ref):
    sc = _silu(c_ref[...])
    o_ref[...] = jnp.dot(sc, w_ref[...], preferred_element_type=jnp.float32,
                         precision=lax.Precision.HIGHEST) + b_ref[...]


def _mods(cc, ada_w, ada_b):
    depth, d, d3 = ada_w.shape
    rows = cc.shape[0]
    tn = 512
    return pl.pallas_call(
        _mods_body,
        grid=(depth, d3 // tn),
        in_specs=[
            pl.BlockSpec((rows, d), lambda l, j: (0, 0)),
            pl.BlockSpec((None, d, tn), lambda l, j: (l, 0, j)),
            pl.BlockSpec((None, 1, tn), lambda l, j: (l, 0, j)),
        ],
        out_specs=pl.BlockSpec((None, rows, tn), lambda l, j: (l, 0, j)),
        out_shape=jax.ShapeDtypeStruct((depth, rows, d3), jnp.float32),
        compiler_params=_cparams(("arbitrary", "arbitrary")),
        name="mods",
    )(cc, ada_w, ada_b.reshape(depth, 1, d3))


def _mod_spec(n_batch):
    return pl.BlockSpec((None, 1, 3 * D_MODEL), lambda b, i: (jnp.where(i == 0, n_batch, b), 0, 0))


def _norm_mod(h, g, mod):
    y = h * lax.rsqrt(jnp.mean(h * h, axis=-1, keepdims=True) + NORM_EPS) * g
    return y * (1.0 + mod[:, D_MODEL:2 * D_MODEL]) + mod[:, :D_MODEL]


def _rms(x, g):
    return x * lax.rsqrt(jnp.mean(x * x, axis=-1, keepdims=True) + NORM_EPS) * g


def _rope(x, c, s1, s2, half):
    return x * c + pltpu.roll(x, LANES - half, 1) * s1 + pltpu.roll(x, half, 1) * s2


def _proj_even_body(h_ref, *rest):
    _proj_even_core(h_ref[...], *rest)


def _proj_first_body(ctx_ref, x_ref, *rest):
    *rest, h_out_ref = rest
    h = jnp.where(pl.program_id(1) == 0, ctx_ref[...], x_ref[...])
    h_out_ref[...] = h
    _proj_even_core(h, *rest)


def _proj_even_core(h, mod_ref, g_ref, w_ref, wvat_ref, qg_ref, wuq_ref, kvg_ref, wukv_ref, wvt_ref,
                    ca_ref, sa1_ref, sa2_ref, cb_ref, sb1_ref, sb2_ref,
                    qa_ref, ka_ref, vat_ref, gate_ref, qb_ref, kb_ref, vbt_ref):
    u = _norm_mod(h, g_ref[...], mod_ref[...]).astype(jnp.bfloat16)
    ca, sa1, sa2 = ca_ref[...], sa1_ref[...], sa2_ref[...]
    cb, sb1, sb2 = cb_ref[...], sb1_ref[...], sb2_ref[...]

    qk = _dot(u, w_ref[:, _QA:_ZA])
    a_scale = HEAD_DIM ** -0.5 * LOG2E
    for j in range(4):
        blk = _rope(qk[:, j * LANES:(j + 1) * LANES], ca, sa1, sa2, HEAD_DIM // 2)
        qa_ref[:, j * LANES:(j + 1) * LANES] = (blk * a_scale).astype(jnp.bfloat16)
    ka_ref[...] = _rope(qk[:, _KA:_ZA], ca, sa1, sa2, HEAD_DIM // 2).astype(jnp.bfloat16)
    vat = _dot_t(wvat_ref[...], u).astype(jnp.bfloat16)
    for j in range(TILE // LANES):
        vat_ref[j] = vat[:, j * LANES:(j + 1) * LANES]
    gate_ref[...] = _silu(_dot(u, w_ref[:, _ZA:_CQ])).astype(jnp.bfloat16)

    lat = _dot(u, w_ref[:, _CQ:_AB_COLS])
    cq = _rms(lat[:, :B_Q_LORA], qg_ref[...]).astype(jnp.bfloat16)
    ckv = _rms(lat[:, B_Q_LORA:B_Q_LORA + B_KV_LORA], kvg_ref[...]).astype(jnp.bfloat16)
    kr = _rope(lat[:, B_Q_LORA + B_KV_LORA:], cb, sb1, sb2, B_ROPE // 2)

    qb = _dot(cq, wuq_ref[...])
    b_scale = (B_NOPE + B_ROPE) ** -0.5 * LOG2E
    kb = _dot(ckv, wukv_ref[...])
    for hd in range(B_HEADS):
        sl = slice(hd * LANES, (hd + 1) * LANES)
        qb_ref[:, sl] = (_rope(qb[:, sl], cb, sb1, sb2, B_ROPE // 2) * b_scale).astype(jnp.bfloat16)
        kb_ref[:, sl] = (kb[:, sl] + kr).astype(jnp.bfloat16)
    vbt_ref[...] = _dot_t(wvt_ref[...], ckv).astype(jnp.bfloat16)


def _proj_even(h, mods_l, g, w_in, wvat, qg, wuq, kvg, wukv, wvt, tabs_a, tabs_b):
    first = isinstance(h, tuple)
    if first:
        ctx, x = h
        n_batch, seq_len, d = x.shape
        t_len = ctx.shape[1] + seq_len
    else:
        n_batch, t_len, d = h.shape
    nt = t_len // TILE
    row = lambda w: pl.BlockSpec((None, TILE, w), lambda b, i: (b, i, 0))
    full = lambda a: pl.BlockSpec(a.shape, lambda b, i: (0,) * a.ndim)
    tab = pl.BlockSpec((TILE, LANES), lambda b, i: (i, 0))
    bf = jnp.bfloat16
    sds = lambda *shape: jax.ShapeDtypeStruct((n_batch,) + shape, bf)
    n_vt = B_HEADS * B_V
    per_tile = TILE // LANES
    if first:
        h_specs = [pl.BlockSpec((None, TILE, d), lambda b, i: (b, 0, 0)),
                   pl.BlockSpec((None, TILE, d), lambda b, i: (b, jnp.maximum(i - 1, 0), 0))]
        h_args = [ctx, x]
    else:
        h_specs, h_args = [row(d)], [h]
    out_specs = [row(512), row(LANES),
                 pl.BlockSpec((None, per_tile, LANES, LANES), lambda b, i: (b, i, 0, 0)),
                 row(d), row(d), row(d),
                 pl.BlockSpec((None, None, n_vt, TILE), lambda b, i: (b, i, 0, 0))]
    out_shape = [sds(t_len, 512), sds(t_len, LANES), sds(t_len // LANES, LANES, LANES),
                 sds(t_len, d), sds(t_len, d), sds(t_len, d), sds(nt, n_vt, TILE)]
    if first:
        out_specs.append(row(d))
        out_shape.append(jax.ShapeDtypeStruct((n_batch, t_len, d), jnp.float32))
    return pl.pallas_call(
        _proj_first_body if first else _proj_even_body,
        grid=(n_batch, nt),
        in_specs=h_specs + [_mod_spec(n_batch), full(g), full(w_in), full(wvat), full(qg), full(wuq), full(kvg),
                            full(wukv), full(wvt)] + [tab] * 6,
        out_specs=out_specs,
        out_shape=out_shape,
        compiler_params=_cparams(("parallel", "arbitrary")),
        name="proj_even",
    )(*h_args, mods_l, g, w_in, wvat, qg, wuq, kvg, wukv, wvt, *tabs_a, *tabs_b)


def _proj_odd_body(h_ref, mod_ref, g_ref, w_ref, wvt_ref, q_ref, k_ref, vt_ref, gate_ref):
    u = _norm_mod(h_ref[...], g_ref[...], mod_ref[...]).astype(jnp.bfloat16)
    d = D_MODEL
    q_ref[...] = (_dot(u, w_ref[:, :d]) * (HEAD_DIM ** -0.5 * LOG2E)).astype(jnp.bfloat16)
    k_ref[...] = _dot(u, w_ref[:, d:2 * d]).astype(jnp.bfloat16)
    vt_ref[...] = _dot_t(wvt_ref[...], u).astype(jnp.bfloat16)
    gate_ref[...] = _silu(_dot(u, w_ref[:, 2 * d:])).astype(jnp.bfloat16)


def _proj_odd(h, mods_l, g, w_qkz, wvt):
    n_batch, t_len, d = h.shape
    nt = t_len // TILE
    row = pl.BlockSpec((None, TILE, d), lambda b, i: (b, i, 0))
    row_t = pl.BlockSpec((None, None, d, TILE), lambda b, i: (b, i, 0, 0))
    full = lambda a: pl.BlockSpec(a.shape, lambda b, i: (0,) * a.ndim)
    out = jax.ShapeDtypeStruct((n_batch, t_len, d), jnp.bfloat16)
    return pl.pallas_call(
        _proj_odd_body,
        grid=(n_batch, nt),
        in_specs=[row, _mod_spec(n_batch), full(g), full(w_qkz), full(wvt)],
        out_specs=[row, row, row_t, row],
        out_shape=[out, out, jax.ShapeDtypeStruct((n_batch, nt, d, TILE), jnp.bfloat16), out],
        compiler_params=_cparams(("parallel", "arbitrary")),
        name="proj_odd",
    )(h, mods_l, g, w_qkz, wvt)


def _lane_half_masks(rows, dtype):
    lane = lax.broadcasted_iota(jnp.int32, (rows, LANES), 1)
    lo = lane < HEAD_DIM
    return lo, jnp.logical_not(lo)


def _win_fill_mask(mask_ref):
    r = lax.broadcasted_iota(jnp.int32, (2 * TILE, TILE), 0)
    c = lax.broadcasted_iota(jnp.int32, (2 * TILE, TILE), 1)
    for case, (back, first) in enumerate(((A_WINDOW, A_WINDOW), (A_WINDOW, 0), (2 * A_WINDOW, 0))):
        ok = (jnp.abs(r - back - c) <= A_WINDOW) & (r >= first)
        mask_ref[case] = jnp.where(ok, 0.0, NEG)
    mask_ref[3] = jnp.full((2 * TILE, TILE), NEG, jnp.float32)


def _win_body(sink_ref, q_ref, k_ref, vt_ref, o_ref, mask_ref, sa_ref, sb_ref, p_ref, *, nt):
    assert TILE == 2 * A_WINDOW
    pair = pl.program_id(1)
    _win_fill_mask(mask_ref)
    n_keys = 3 * TILE
    n128 = nt * (TILE // LANES)
    ones = jnp.ones((16, n_keys), jnp.bfloat16)
    zero = jnp.zeros((), jnp.bfloat16)
    blk = LANES
    masks = _lane_half_masks(TILE, None)
    sinks = tuple(sink_ref[2 * pair + hh] * LOG2E for hh in range(2))

    def window(i):
        return jnp.clip(2 * i - 1, 0, n128 - 4)

    def scores(i, s_ref, heads):
        case = jnp.where(i == 0, 3, jnp.where(i == 1, 0, jnp.where(i == nt - 1, 2, 1)))
        k_ctx = k_ref[:TILE, :]
        k_loc = k_ref[pl.ds(pl.multiple_of(window(i) * LANES, LANES), 2 * TILE), :]
        q2 = q_ref[pl.ds(pl.multiple_of(i * TILE, TILE), TILE), :]
        m = []
        for hh in heads:
            qm = jnp.where(masks[hh], q2, zero)
            s_ctx = _dot_t(k_ctx, qm)
            s_ref[hh, :TILE, :] = s_ctx
            s_loc = _dot_t(k_loc, qm) + mask_ref[case]
            s_ref[hh, TILE:, :] = s_loc
            m.append(jnp.maximum(jnp.maximum(jnp.max(s_ctx, axis=0, keepdims=True),
                                             jnp.max(s_loc, axis=0, keepdims=True)), sinks[hh]))
        return m

    def update(i, s_ref, m, heads):
        w0 = window(i)
        vt = jnp.concatenate([vt_ref[0], vt_ref[1]] + [vt_ref[w0 + j] for j in range(4)], axis=1)
        outs = []
        for hh in heads:
            for c in range(n_keys // blk):
                p = jnp.exp2(s_ref[hh, c * blk:(c + 1) * blk, :] - m[hh])
                p_ref[hh, c * blk:(c + 1) * blk, :] = p.astype(jnp.bfloat16)
            vt_h = jnp.concatenate([vt[hh * HEAD_DIM:(hh + 1) * HEAD_DIM], ones], axis=0)
            acc = _dot(vt_h, p_ref[hh])
            denom = acc[HEAD_DIM:HEAD_DIM + 1] + jnp.exp2(sinks[hh] - m[hh])
            outs.append(acc[:HEAD_DIM] * (1.0 / denom))
        return outs

    def store(i, outs):
        o_ref[pl.ds(pl.multiple_of(i * TILE, TILE), TILE), :] = jnp.concatenate(outs, axis=0).T.astype(jnp.bfloat16)

    def step(i_next, s_next, i_cur, s_cur, m_cur):
        m_next, outs = [], []
        for hh in range(2):
            m_next += scores(i_next, s_next, (hh,))
            outs += update(i_cur, s_cur, m_cur, (hh,))
        store(i_cur, outs)
        return tuple(m_next)

    def body(t, m_a):
        for j in range(0, TILES_PER_TRIP, 2):
            i = TILES_PER_TRIP * t + j
            m_b = step(i + 1, sb_ref, i, sa_ref, m_a)
            m_a = step(i + 2, sa_ref, i + 1, sb_ref, m_b)
        return m_a

    m_a = lax.fori_loop(0, (nt - 1) // TILES_PER_TRIP, body, tuple(scores(0, sa_ref, (0, 1))))
    store(nt - 1, update(nt - 1, sa_ref, m_a, (0, 1)))


def _win_attn(qa, ka, vat, sink, ctx_len):
    n_batch, t_len, _ = qa.shape
    nt = t_len // TILE
    assert ctx_len == TILE and nt >= 4 and (nt - 1) % TILES_PER_TRIP == 0, "whole loop trips plus a last tile"
    return pl.pallas_call(
        functools.partial(_win_body, nt=nt),
        grid=(n_batch, A_HEADS // 2),
        in_specs=[pl.BlockSpec(memory_space=pltpu.SMEM),
                  pl.BlockSpec((None, t_len, LANES), lambda b, p: (b, 0, p)),
                  pl.BlockSpec((None, t_len, LANES), lambda b, p: (b, 0, 0)),
                  pl.BlockSpec((None,) + vat.shape[1:], lambda b, p: (b, 0, 0, 0))],
        out_specs=pl.BlockSpec((None, t_len, LANES), lambda b, p: (b, 0, p)),
        out_shape=jax.ShapeDtypeStruct((n_batch, t_len, A_HEADS * HEAD_DIM), jnp.bfloat16),
        scratch_shapes=[pltpu.VMEM((4, 2 * TILE, TILE), jnp.float32),
                        pltpu.VMEM((2, 3 * TILE, TILE), jnp.float32),
                        pltpu.VMEM((2, 3 * TILE, TILE), jnp.float32),
                        pltpu.VMEM((2, 3 * TILE, TILE), jnp.bfloat16)],
        compiler_params=_cparams(("parallel", "parallel")),
        name="win_attn",
    )(sink, qa, ka, vat)


def _mla_body(q_ref, k_ref, vt_ref, o_ref, sa_ref, sb_ref, sc_ref, p_ref, *, nt, n_chunks):
    sub = MLA_CHUNK // TILE
    blk = LANES
    ones_rows = 16
    n_items = 1 + n_chunks
    bufs = (sa_ref, sb_ref, sc_ref)
    assert n_items % len(bufs) == 0

    def tile_rows(t):
        return pl.ds(pl.multiple_of(t * TILE, TILE), TILE)

    def keys(n):
        return (0, 0, 1) if n == 0 else (TILE + (n - 1) * MLA_CHUNK, 1 + (n - 1) * sub, sub)

    def scores_h(qi, n, hh):
        start, _, n_tiles = keys(n)
        size = n_tiles * TILE
        cols = slice(hh * LANES, (hh + 1) * LANES)
        s = _dot_t(k_ref[start:start + size, cols], q_ref[tile_rows(qi), cols])
        bufs[n % len(bufs)][hh, :size, :] = s
        return jnp.max(s, axis=0, keepdims=True)

    def update_h(st, cm, n, hh):
        _, tile0, n_tiles = keys(n)
        size = n_tiles * TILE
        s_ref = bufs[n % len(bufs)]
        m_prev, acc = st
        m_new = jnp.maximum(m_prev, cm)
        alpha = jnp.exp2(m_prev - m_new)
        for c in range(size // blk):
            p = jnp.exp2(s_ref[hh, c * blk:(c + 1) * blk, :] - m_new)
            p_ref[hh, c * blk:(c + 1) * blk, :] = p.astype(jnp.bfloat16)
        vt = jnp.concatenate([vt_ref[tile0 + c, hh * B_V:(hh + 1) * B_V, :] for c in range(n_tiles)], axis=1)
        vt = jnp.concatenate([vt, jnp.ones((ones_rows, size), jnp.bfloat16)], axis=0)
        return m_new, acc * alpha + _dot(vt, p_ref[hh, :size, :])

    def scores(qi, n):
        return tuple(scores_h(qi, n, hh) for hh in range(2))

    def update(state, cm, n):
        return tuple(update_h(state[hh], cm[hh], n, hh) for hh in range(2))

    def finish(state, qi):
        o_t = jnp.concatenate([acc[:B_V] * (1.0 / acc[B_V:B_V + 1]) for _, acc in state], axis=0)
        o_ref[tile_rows(qi), :] = o_t.T.astype(jnp.bfloat16)

    init = tuple((jnp.full((1, TILE), NEG, jnp.float32), jnp.zeros((B_V + ones_rows, TILE), jnp.float32))
                 for _ in range(2))
    finish(update(init, scores(0, 0), 0), 0)

    def body(qi, cm):
        state = list(init)
        cm = list(cm)
        nxt = jnp.minimum(qi + 1, nt - 1)
        for n in range(n_items):
            for hh in range(2):
                cm_next = scores_h(qi, n + 1, hh) if n + 1 < n_items else scores_h(nxt, 0, hh)
                state[hh] = update_h(state[hh], cm[hh], n, hh)
                cm[hh] = cm_next
        finish(state, qi)
        return tuple(cm)

    lax.fori_loop(1, nt, body, scores(1, 0))


def _mla_attn(qb, kb, vbt, ctx_len):
    n_batch, t_len, _ = qb.shape
    nt = t_len // TILE
    seq = t_len - ctx_len
    assert ctx_len == TILE and seq % MLA_CHUNK == 0 and MLA_CHUNK % TILE == 0
    n_chunks = seq // MLA_CHUNK
    assert (1 + n_chunks) % 3 == 0, "context keys + latent chunks rotate over three score buffers"
    body = functools.partial(_mla_body, nt=nt, n_chunks=n_chunks)
    s_buf = pltpu.VMEM((2, MLA_CHUNK, TILE), jnp.float32)
    return pl.pallas_call(
        body,
        grid=(n_batch, B_HEADS // 2),
        in_specs=[pl.BlockSpec((None, t_len, 2 * LANES), lambda b, p: (b, 0, p)),
                  pl.BlockSpec((None, t_len, 2 * LANES), lambda b, p: (b, 0, p)),
                  pl.BlockSpec((None, nt, 2 * B_V, TILE), lambda b, p: (b, 0, p, 0))],
        out_specs=pl.BlockSpec((None, t_len, LANES), lambda b, p: (b, 0, p)),
        out_shape=jax.ShapeDtypeStruct((n_batch, t_len, B_HEADS * B_V), jnp.bfloat16),
        scratch_shapes=[s_buf, s_buf, s_buf, pltpu.VMEM((2, MLA_CHUNK, TILE), jnp.bfloat16)],
        compiler_params=_cparams(("parallel", "parallel")),
        name="mla_attn",
    )(qb, kb, vbt)


def _nbr_row_valid(case, jr, t):
    rpt = TILE // GRID_W
    lo = (rpt, jr + rpt - C_KH // 2, 2 * rpt - C_KH)[case]
    return lo <= t < lo + C_KH


def _nbr_fill_bias(tz_ref, bias_ref):
    rpt = TILE // GRID_W
    lane_lo = lax.broadcasted_iota(jnp.int32, (GRID_W, LANES), 1) < GRID_W
    neg = jnp.full((GRID_W, LANES), NEG, jnp.float32)
    for hh in range(2):
        bias_ref[3, hh] = jnp.full((3 * TILE, TILE), NEG, jnp.float32)
        for case in range(3):
            for t in range(3 * rpt):
                for jp in range(rpt // 2):
                    ok0 = _nbr_row_valid(case, 2 * jp, t)
                    ok1 = _nbr_row_valid(case, 2 * jp + 1, t)
                    if ok0 or ok1:
                        blk = tz_ref[hh, t - rpt - 2 * jp + C_KH - 2]
                        if not ok1:
                            blk = jnp.where(lane_lo, blk, NEG)
                        elif not ok0:
                            blk = jnp.where(lane_lo, NEG, blk)
                    else:
                        blk = neg
                    bias_ref[case, hh, t * GRID_W:(t + 1) * GRID_W, jp * LANES:(jp + 1) * LANES] = blk


def _nbr_body(q_ref, k_ref, vt_ref, tz_ref, o_ref, bias_ref, sa_ref, sb_ref, p_ref, *, nt):
    _nbr_fill_bias(tz_ref, bias_ref)
    n_keys = 4 * TILE
    ones = jnp.ones((16, n_keys), jnp.bfloat16)
    zero = jnp.zeros((), jnp.bfloat16)
    blk = LANES
    masks = _lane_half_masks(TILE, None)

    def key_tiles(i):
        return 0, jnp.clip(i - 1, 1, nt - 1), i, jnp.clip(i + 1, 1, nt - 1)

    def rows(t):
        return pl.ds(pl.multiple_of(t * TILE, TILE), TILE)

    def scores(i, s_ref, heads):
        case = jnp.where(i == 0, 3, jnp.where(i == 1, 0, jnp.where(i == nt - 1, 2, 1)))
        tc, t0, t1, t2 = key_tiles(i)
        k_ctx = k_ref[rows(tc), :]
        k_loc = jnp.concatenate([k_ref[rows(t0), :], k_ref[rows(t1), :], k_ref[rows(t2), :]], axis=0)
        q2 = q_ref[rows(i), :]
        m = []
        for hh in heads:
            qm = jnp.where(masks[hh], q2, zero)
            s_ctx = _dot_t(k_ctx, qm)
            s_ref[hh, :TILE, :] = s_ctx
            s_loc = _dot_t(k_loc, qm) + bias_ref[case, hh]
            s_ref[hh, TILE:, :] = s_loc
            m.append(jnp.maximum(jnp.max(s_ctx, axis=0, keepdims=True), jnp.max(s_loc, axis=0, keepdims=True)))
        return m

    def update(i, s_ref, m, heads):
        vt = jnp.concatenate([vt_ref[t] for t in key_tiles(i)], axis=1)
        outs = []
        for hh in heads:
            for c in range(n_keys // blk):
                p = jnp.exp2(s_ref[hh, c * blk:(c + 1) * blk, :] - m[hh])
                p_ref[hh, c * blk:(c + 1) * blk, :] = p.astype(jnp.bfloat16)
            vt_h = jnp.concatenate([vt[hh * HEAD_DIM:(hh + 1) * HEAD_DIM], ones], axis=0)
            acc = _dot(vt_h, p_ref[hh])
            outs.append(acc[:HEAD_DIM] * (1.0 / acc[HEAD_DIM:HEAD_DIM + 1]))
        return outs

    def store(i, outs):
        o_ref[rows(i), :] = jnp.concatenate(outs, axis=0).T.astype(jnp.bfloat16)

    def step(i_next, s_next, i_cur, s_cur, m_cur):
        m_next, outs = [], []
        for hh in range(2):
            m_next += scores(i_next, s_next, (hh,))
            outs += update(i_cur, s_cur, m_cur, (hh,))
        store(i_cur, outs)
        return tuple(m_next)

    def body(t, m_a):
        for j in range(0, TILES_PER_TRIP, 2):
            i = TILES_PER_TRIP * t + j
            m_b = step(i + 1, sb_ref, i, sa_ref, m_a)
            m_a = step(i + 2, sa_ref, i + 1, sb_ref, m_b)
        return m_a

    m_a = lax.fori_loop(0, (nt - 1) // TILES_PER_TRIP, body, tuple(scores(0, sa_ref, (0, 1))))
    store(nt - 1, update(nt - 1, sa_ref, m_a, (0, 1)))


def _nbr_attn(q, k, vt, tz, ctx_len):
    n_batch, t_len, d = q.shape
    nt = t_len // TILE
    assert ctx_len == TILE and TILE % GRID_W == 0 and C_KH == 2 * (TILE // GRID_W)
    assert nt >= 4, "needs distinct first / interior / last latent tiles"
    assert (nt - 1) % TILES_PER_TRIP == 0, "whole loop trips plus a last tile"
    seq = pl.BlockSpec((None, t_len, LANES), lambda b, p: (b, 0, p))
    return pl.pallas_call(
        functools.partial(_nbr_body, nt=nt),
        grid=(n_batch, C_HEADS // 2),
        in_specs=[seq, seq, pl.BlockSpec((None, nt, LANES, TILE), lambda b, p: (b, 0, p, 0)),
                  pl.BlockSpec((2,) + tz.shape[1:], lambda b, p: (p, 0, 0, 0))],
        out_specs=seq,
        out_shape=jax.ShapeDtypeStruct((n_batch, t_len, d), jnp.bfloat16),
        scratch_shapes=[pltpu.VMEM((4, 2, 3 * TILE, TILE), jnp.float32),
                        pltpu.VMEM((2, 4 * TILE, TILE), jnp.float32),
                        pltpu.VMEM((2, 4 * TILE, TILE), jnp.float32),
                        pltpu.VMEM((2, 4 * TILE, TILE), jnp.bfloat16)],
        compiler_params=_cparams(("parallel", "parallel")),
        name="nbr_attn",
    )(q, k, vt, tz)


def _nbr_toeplitz(rpb):
    n_off = 2 * C_KW - 1
    pad = GRID_W - C_KW
    ext = jnp.pad(rpb, ((0, 0), (0, 0), (pad, pad)), mode="edge")[:, :, ::-1]
    assert ext.shape[-1] == n_off + 2 * pad == 2 * GRID_W - 1
    n_h, n_r, length = ext.shape
    flat = jnp.tile(jnp.pad(ext, ((0, 0), (0, 0), (0, 1))), (1, 1, GRID_W))
    tz = flat[:, :, GRID_W - 1:GRID_W - 1 + GRID_W * length].reshape(n_h, n_r, GRID_W, length)[..., :GRID_W]
    qc = np.arange(GRID_W)[None, :]
    kc = np.arange(GRID_W)[:, None]
    cs = np.clip(qc - C_KW // 2, 0, GRID_W - C_KW)
    col_ok = jnp.asarray((kc >= cs) & (kc < cs + C_KW))
    tz = jnp.where(col_ok, tz * LOG2E, NEG)
    return jnp.concatenate([tz[:, 1:], tz[:, :-1]], axis=-1)


def _out_body(h_ref, mod_ref, o1_ref, o2_ref, gate_ref, w_ref, *rest, final):
    half = D_MODEL // 2
    g = gate_ref[...].astype(jnp.float32)
    og1 = (o1_ref[...].astype(jnp.float32) * g[:, :half]).astype(jnp.bfloat16)
    og2 = (o2_ref[...].astype(jnp.float32) * g[:, half:]).astype(jnp.bfloat16)
    y = _dot(og1, w_ref[:half, :]) + _dot(og2, w_ref[half:, :])
    h_new = h_ref[...] + mod_ref[:, 2 * D_MODEL:] * y
    if final:
        fg_ref, out_ref = rest
        out_ref[...] = _rms(h_new, fg_ref[...])
    else:
        rest[0][...] = h_new


def _out_proj(h, mods_l, o1, o2, gate, w, o2_block, final_g=None):
    n_batch, t_len, d = h.shape
    nt = t_len // TILE
    final = final_g is not None
    off = 1 if final else 0
    row = lambda wd, blk=0: pl.BlockSpec((None, TILE, wd), lambda b, i: (b, i + off, blk))
    mod = pl.BlockSpec((None, 1, 3 * D_MODEL), lambda b, i: (jnp.where(i + off == 0, n_batch, b), 0, 0))
    full = lambda a: pl.BlockSpec(a.shape, lambda b, i: (0,) * a.ndim)
    in_specs = [row(d), mod, row(d // 2), row(d // 2, o2_block), row(d), full(w)]
    args = [h, mods_l, o1, o2, gate, w]
    if final:
        in_specs.append(full(final_g))
        args.append(final_g)
        out_spec = pl.BlockSpec((None, TILE, d), lambda b, i: (b, i, 0))
        out_shape = jax.ShapeDtypeStruct((n_batch, t_len - TILE, d), jnp.float32)
        aliases = {}
    else:
        out_spec = row(d)
        out_shape = jax.ShapeDtypeStruct(h.shape, jnp.float32)
        aliases = {0: 0}
    return pl.pallas_call(
        functools.partial(_out_body, final=final),
        grid=(n_batch, nt - off),
        in_specs=in_specs,
        out_specs=out_spec,
        out_shape=out_shape,
        input_output_aliases=aliases,
        compiler_params=_cparams(("parallel", "arbitrary")),
        name="out_proj_final" if final else "out_proj",
    )(*args)


def _deinterleave(n):
    return np.concatenate([np.arange(0, n, 2), np.arange(1, n, 2)])


def _take_cols(w, cols):
    cols = np.asarray(cols)
    picked = jnp.take(w, jnp.asarray(np.maximum(cols, 0)), axis=1)
    return jnp.where(jnp.asarray(cols >= 0)[None, :], picked, 0.0)


def _even_layout():
    perm = _deinterleave(HEAD_DIM)
    qa0, ka0 = 0, A_HEADS * HEAD_DIM
    va0 = ka0 + A_KV_HEADS * HEAD_DIM
    za0 = va0 + A_KV_HEADS * HEAD_DIM
    cq0 = za0 + A_HEADS * HEAD_DIM
    ckv0 = cq0 + B_Q_LORA
    kr0 = ckv0 + B_KV_LORA
    zb0 = kr0 + B_ROPE
    cols = []
    cols += [qa0 + hd * HEAD_DIM + perm for hd in _A_SLOTS]
    cols += [ka0 + kv * HEAD_DIM + perm for kv in range(A_KV_HEADS)]
    va_cols = va0 + np.arange(A_KV_HEADS * HEAD_DIM)
    cols += [za0 + hd * HEAD_DIM + np.arange(HEAD_DIM) for hd in _A_SLOTS]
    cols += [zb0 + np.arange(B_HEADS * B_V)]
    cols += [cq0 + np.arange(B_Q_LORA), ckv0 + np.arange(B_KV_LORA)]
    cols += [np.full(B_NOPE, -1), kr0 + _deinterleave(B_ROPE), np.full(LANES - B_NOPE - B_ROPE, -1)]
    in_cols = np.concatenate(cols)
    assert in_cols.shape[0] == _AB_COLS

    uq = []
    for hd in range(B_HEADS):
        base = hd * (B_NOPE + B_ROPE)
        uq += [base + np.arange(B_NOPE), base + B_NOPE + _deinterleave(B_ROPE),
               np.full(LANES - B_NOPE - B_ROPE, -1)]
    ukv_k, ukv_v = [], []
    pad = np.full(LANES - B_NOPE, -1)
    for hd in range(B_HEADS):
        base = hd * (B_NOPE + B_V)
        ukv_k += [base + np.arange(B_NOPE), pad]
        ukv_v += [base + B_NOPE + np.arange(B_V)]
    out_rows = np.concatenate([hd * HEAD_DIM + np.arange(HEAD_DIM) for hd in _A_SLOTS]
                              + [A_HEADS * HEAD_DIM + np.arange(B_HEADS * B_V)])
    return in_cols, va_cols, np.concatenate(uq), np.concatenate(ukv_k), np.concatenate(ukv_v), out_rows


def _rope_tables(seq_len, ctx_len):
    t = jnp.arange(seq_len, dtype=jnp.int32)
    row = (t // GRID_W).astype(jnp.float32)
    col = (t % GRID_W).astype(jnp.float32)

    def angles(rot_dim):
        ppa = rot_dim // 4
        inv = ROPE_THETA ** (-jnp.arange(ppa, dtype=jnp.float32) / ppa)
        ang = jnp.concatenate([row[:, None] * inv, col[:, None] * inv], axis=-1)
        return jnp.cos(ang), jnp.sin(ang)

    def with_ctx(c, s1, s2):
        pad = lambda a, v: jnp.concatenate([jnp.full((ctx_len, LANES), v, jnp.float32), a], axis=0)
        return pad(c, 1.0), pad(s1, 0.0), pad(s2, 0.0)

    cos_a, sin_a = angles(HEAD_DIM)
    z32 = jnp.zeros_like(sin_a)
    tabs_a = with_ctx(jnp.tile(cos_a, (1, 4)), jnp.tile(jnp.concatenate([-sin_a, z32], 1), (1, 2)),
                      jnp.tile(jnp.concatenate([z32, sin_a], 1), (1, 2)))
    cos_b, sin_b = angles(B_ROPE)
    one64 = jnp.ones((seq_len, B_NOPE), jnp.float32)
    z64 = jnp.zeros((seq_len, B_NOPE), jnp.float32)
    z16 = jnp.zeros_like(sin_b)
    one32 = jnp.ones((seq_len, LANES - B_NOPE - B_ROPE), jnp.float32)
    z32b = jnp.zeros((seq_len, LANES - B_NOPE - B_ROPE), jnp.float32)
    tabs_b = with_ctx(jnp.concatenate([one64, cos_b, cos_b, one32], 1),
                      jnp.concatenate([z64, -sin_b, z16, z32b], 1),
                      jnp.concatenate([z64, z16, sin_b, z32b], 1))
    return tabs_a, tabs_b


def kernel(x, c, ctx, c_ctx, ada_w, ada_b, norm_g, ab_in_w, ab_out_w, a_sink, b_q_norm_g, b_w_uq,
           b_kv_norm_g, b_w_ukv, c_in_w, c_out_w, c_rpb, final_g):
    n_batch, seq_len, d = x.shape
    ctx_len = ctx.shape[1]
    depth = ada_w.shape[0]
    assert d == D_MODEL and ctx_len == TILE and seq_len % TILE == 0
    assert depth % 2 == 0, "the final norm is fused into the last (odd) layer's output projection"
    bf = jnp.bfloat16

    mod_rows = -(-(n_batch + 1) // 8) * 8
    cc = jnp.concatenate([c, c_ctx[None, :], jnp.zeros((mod_rows - n_batch - 1, d), jnp.float32)], axis=0)
    mods = _mods(cc, ada_w, ada_b).reshape(depth, mod_rows, 1, 3 * d)

    in_cols, va_cols, uq_cols, uk_cols, uv_cols, out_rows = _even_layout()
    tabs_a, tabs_b = _rope_tables(seq_len, ctx_len)
    h = (ctx, x)
    row2 = lambda v: v.reshape(1, -1)

    out = None
    for layer in range(depth):
        i = layer // 2
        last = layer == depth - 1
        g = row2(norm_g[layer])
        if layer % 2 == 0:
            in_w, ukv_w = ab_in_w[i].astype(bf), b_w_ukv[i].astype(bf)
            w_in = _take_cols(in_w, in_cols)
            wuq = _take_cols(b_w_uq[i].astype(bf), uq_cols)
            wuk = _take_cols(ukv_w, uk_cols)
            wvt = _take_cols(ukv_w, uv_cols).T
            w_out = jnp.take(ab_out_w[i].astype(bf), jnp.asarray(out_rows), axis=0)
            sink = jnp.take(a_sink[i], jnp.asarray(_A_SLOTS))
            wvat = _take_cols(in_w, va_cols).T
            outs = _proj_even(h, mods[layer], g, w_in, wvat, row2(b_q_norm_g[i]), wuq,
                              row2(b_kv_norm_g[i]), wuk, wvt, tabs_a, tabs_b)
            if layer == 0:
                h = outs[-1]
            qa, ka, vat, gate, qb, kb, vbt = outs[:7]
            o1 = _win_attn(qa, ka, vat, sink, ctx_len)
            o2 = _mla_attn(qb, kb, vbt, ctx_len)
            o2_block = 0
        else:
            w = c_in_w[i].astype(bf)
            w_qkz = jnp.concatenate([w[:, :2 * d], w[:, 3 * d:]], axis=1)
            q, k, vt, gate = _proj_odd(h, mods[layer], g, w_qkz, w[:, 2 * d:3 * d].T)
            o1 = o2 = _nbr_attn(q, k, vt, _nbr_toeplitz(c_rpb[i]), ctx_len)
            o2_block = 1
            w_out = c_out_w[i].astype(bf)
        if last:
            out = _out_proj(h, mods[layer], o1, o2, gate, w_out, o2_block, final_g=row2(final_g))
        else:
            h = _out_proj(h, mods[layer], o1, o2, gate, w_out, o2_block)
    return out
```

```python
import functools
import math

import numpy as np
import jax
import jax.numpy as jnp
from jax import lax
from jax.experimental import pallas as pl
from jax.experimental.pallas import tpu as pltpu

D_MODEL = 1024
HEAD_DIM = 64
GRID_W = 64
ROPE_THETA = 10000.0
NORM_EPS = 1e-6
A_HEADS = 8
A_KV_HEADS = 2
A_WINDOW = 128
B_HEADS = 8
B_Q_LORA = 384
B_KV_LORA = 256
B_NOPE = 64
B_ROPE = 32
B_V = 64
C_HEADS = 16
C_KH = 8
C_KW = 16

LANES = 128
TILE = 256
MLA_CHUNK = 1024
TILES_PER_TRIP = 4
LOG2E = math.log2(math.e)
NEG = -1e30
VMEM_LIMIT = 48 * 1024 * 1024

_QA, _KA, _ZA, _ZB, _CQ, _CKV, _KR, _AB_COLS = 0, 512, 640, 1152, 1664, 2048, 2304, 2432
_A_SLOTS = (0, 4, 1, 5, 2, 6, 3, 7)


def _cparams(sem):
    return pltpu.CompilerParams(dimension_semantics=sem, vmem_limit_bytes=VMEM_LIMIT)


def _silu(x):
    return x * (1.0 / (1.0 + jnp.exp(-x)))


def _dot(a, b):
    return jnp.dot(a, b, preferred_element_type=jnp.float32)


def _dot_t(a, b):
    return lax.dot_general(a, b, (((1,), (1,)), ((), ())), preferred_element_type=jnp.float32)


def _mods_body(c_ref, w_ref, b_ref, o_ref):
    sc = _silu(c_ref[...])
    o_ref[...] = jnp.dot(sc, w_ref[...], preferred_element_type=jnp.float32,
                         precision=lax.Precision.HIGHEST) + b_ref[...]


def _mods(cc, ada_w, ada_b):
    depth, d, d3 = ada_w.shape
    rows = cc.shape[0]
    tn = 512
    return pl.pallas_call(
        _mods_body,
        grid=(depth, d3 // tn),
        in_specs=[
            pl.BlockSpec((rows, d), lambda l, j: (0, 0)),
            pl.BlockSpec((None, d, tn), lambda l, j: (l, 0, j)),
            pl.BlockSpec((None, 1, tn), lambda l, j: (l, 0, j)),
        ],
        out_specs=pl.BlockSpec((None, rows, tn), lambda l, j: (l, 0, j)),
        out_shape=jax.ShapeDtypeStruct((depth, rows, d3), jnp.float32),
        compiler_params=_cparams(("arbitrary", "arbitrary")),
        name="mods",
    )(cc, ada_w, ada_b.reshape(depth, 1, d3))


def _mod_spec(n_batch):
    return pl.BlockSpec((None, 1, 3 * D_MODEL), lambda b, i: (jnp.where(i == 0, n_batch, b), 0, 0))


def _norm_mod(h, g, mod):
    y = h * lax.rsqrt(jnp.mean(h * h, axis=-1, keepdims=True) + NORM_EPS) * g
    return y * (1.0 + mod[:, D_MODEL:2 * D_MODEL]) + mod[:, :D_MODEL]


def _rms(x, g):
    return x * lax.rsqrt(jnp.mean(x * x, axis=-1, keepdims=True) + NORM_EPS) * g


def _rope(x, c, s1, s2, half):
    return x * c + pltpu.roll(x, LANES - half, 1) * s1 + pltpu.roll(x, half, 1) * s2


def _proj_even_body(h_ref, *rest):
    _proj_even_core(h_ref[...], *rest)


def _proj_first_body(ctx_ref, x_ref, *rest):
    *rest, h_out_ref = rest
    h = jnp.where(pl.program_id(1) == 0, ctx_ref[...], x_ref[...])
    h_out_ref[...] = h
    _proj_even_core(h, *rest)


def _proj_even_core(h, mod_ref, g_ref, w_ref, wvat_ref, qg_ref, wuq_ref, kvg_ref, wukv_ref, wvt_ref,
                    ca_ref, sa1_ref, sa2_ref, cb_ref, sb1_ref, sb2_ref,
                    qa_ref, ka_ref, vat_ref, gate_ref, qb_ref, kb_ref, vbt_ref):
    u = _norm_mod(h, g_ref[...], mod_ref[...]).astype(jnp.bfloat16)
    ca, sa1, sa2 = ca_ref[...], sa1_ref[...], sa2_ref[...]
    cb, sb1, sb2 = cb_ref[...], sb1_ref[...], sb2_ref[...]

    qk = _dot(u, w_ref[:, _QA:_ZA])
    a_scale = HEAD_DIM ** -0.5 * LOG2E
    for j in range(4):
        blk = _rope(qk[:, j * LANES:(j + 1) * LANES], ca, sa1, sa2, HEAD_DIM // 2)
        qa_ref[:, j * LANES:(j + 1) * LANES] = (blk * a_scale).astype(jnp.bfloat16)
    ka_ref[...] = _rope(qk[:, _KA:_ZA], ca, sa1, sa2, HEAD_DIM // 2).astype(jnp.bfloat16)
    vat = _dot_t(wvat_ref[...], u).astype(jnp.bfloat16)
    for j in range(TILE // LANES):
        vat_ref[j] = vat[:, j * LANES:(j + 1) * LANES]
    gate_ref[...] = _silu(_dot(u, w_ref[:, _ZA:_CQ])).astype(jnp.bfloat16)

    lat = _dot(u, w_ref[:, _CQ:_AB_COLS])
    cq = _rms(lat[:, :B_Q_LORA], qg_ref[...]).astype(jnp.bfloat16)
    ckv = _rms(lat[:, B_Q_LORA:B_Q_LORA + B_KV_LORA], kvg_ref[...]).astype(jnp.bfloat16)
    kr = _rope(lat[:, B_Q_LORA + B_KV_LORA:], cb, sb1, sb2, B_ROPE // 2)

    qb = _dot(cq, wuq_ref[...])
    b_scale = (B_NOPE + B_ROPE) ** -0.5 * LOG2E
    kb = _dot(ckv, wukv_ref[...])
    for hd in range(B_HEADS):
        sl = slice(hd * LANES, (hd + 1) * LANES)
        qb_ref[:, sl] = (_rope(qb[:, sl], cb, sb1, sb2, B_ROPE // 2) * b_scale).astype(jnp.bfloat16)
        kb_ref[:, sl] = (kb[:, sl] + kr).astype(jnp.bfloat16)
    vbt_ref[...] = _dot_t(wvt_ref[...], ckv).astype(jnp.bfloat16)


def _proj_even(h, mods_l, g, w_in, wvat, qg, wuq, kvg, wukv, wvt, tabs_a, tabs_b, prev=None):
    first = isinstance(h, tuple)
    if first:
        ctx, x = h
        n_batch, seq_len, d = x.shape
        t_len = ctx.shape[1] + seq_len
    else:
        n_batch, t_len, d = h.shape
    nt = t_len // TILE
    row = lambda w: pl.BlockSpec((None, TILE, w), lambda b, i: (b, i, 0))
    full = lambda a: pl.BlockSpec(a.shape, lambda b, i: (0,) * a.ndim)
    tab = pl.BlockSpec((TILE, LANES), lambda b, i: (i, 0))
    bf = jnp.bfloat16
    sds = lambda *shape: jax.ShapeDtypeStruct((n_batch,) + shape, bf)
    n_vt = B_HEADS * B_V
    per_tile = TILE // LANES
    if first:
        h_specs = [pl.BlockSpec((None, TILE, d), lambda b, i: (b, 0, 0)),
                   pl.BlockSpec((None, TILE, d), lambda b, i: (b, jnp.maximum(i - 1, 0), 0))]
        h_args = [ctx, x]
    else:
        prev_specs, prev_args = _prev_operands(prev, n_batch) if prev else ([], [])
        h_specs, h_args = [row(d)] + prev_specs, [h] + prev_args
    out_specs = [row(512), row(LANES),
                 pl.BlockSpec((None, per_tile, LANES, LANES), lambda b, i: (b, i, 0, 0)),
                 row(d), row(d), row(d),
                 pl.BlockSpec((None, None, n_vt, TILE), lambda b, i: (b, i, 0, 0))]
    out_shape = [sds(t_len, 512), sds(t_len, LANES), sds(t_len // LANES, LANES, LANES),
                 sds(t_len, d), sds(t_len, d), sds(t_len, d), sds(nt, n_vt, TILE)]
    if first or prev:
        out_specs.append(row(d))
        out_shape.append(jax.ShapeDtypeStruct((n_batch, t_len, d), jnp.float32))
    return pl.pallas_call(
        _proj_first_body if first else
        functools.partial(_residual_then_proj_body, core=_proj_even_core) if prev else _proj_even_body,
        grid=(n_batch, nt),
        in_specs=h_specs + [_mod_spec(n_batch), full(g), full(w_in), full(wvat), full(qg), full(wuq), full(kvg),
                            full(wukv), full(wvt)] + [tab] * 6,
        out_specs=out_specs,
        out_shape=out_shape,
        input_output_aliases={0: len(out_specs) - 1} if prev else {},
        compiler_params=_cparams(("parallel", "arbitrary")),
        name="proj_even",
    )(*h_args, mods_l, g, w_in, wvat, qg, wuq, kvg, wukv, wvt, *tabs_a, *tabs_b)


def _proj_odd_body(h_ref, *rest):
    _proj_odd_core(h_ref[...], *rest)


def _proj_odd_core(h, mod_ref, g_ref, w_ref, wvt_ref, q_ref, k_ref, vt_ref, gate_ref):
    u = _norm_mod(h, g_ref[...], mod_ref[...]).astype(jnp.bfloat16)
    d = D_MODEL
    q_ref[...] = (_dot(u, w_ref[:, :d]) * (HEAD_DIM ** -0.5 * LOG2E)).astype(jnp.bfloat16)
    k_ref[...] = _dot(u, w_ref[:, d:2 * d]).astype(jnp.bfloat16)
    vt_ref[...] = _dot_t(wvt_ref[...], u).astype(jnp.bfloat16)
    gate_ref[...] = _silu(_dot(u, w_ref[:, 2 * d:])).astype(jnp.bfloat16)


def _proj_odd(h, mods_l, g, w_qkz, wvt, prev=None):
    n_batch, t_len, d = h.shape
    nt = t_len // TILE
    row = pl.BlockSpec((None, TILE, d), lambda b, i: (b, i, 0))
    row_t = pl.BlockSpec((None, None, d, TILE), lambda b, i: (b, i, 0, 0))
    full = lambda a: pl.BlockSpec(a.shape, lambda b, i: (0,) * a.ndim)
    out = jax.ShapeDtypeStruct((n_batch, t_len, d), jnp.bfloat16)
    prev_specs, prev_args = _prev_operands(prev, n_batch) if prev else ([], [])
    out_specs = [row, row, row_t, row]
    out_shape = [out, out, jax.ShapeDtypeStruct((n_batch, nt, d, TILE), jnp.bfloat16), out]
    if prev:
        out_specs.append(row)
        out_shape.append(jax.ShapeDtypeStruct(h.shape, jnp.float32))
    return pl.pallas_call(
        functools.partial(_residual_then_proj_body, core=_proj_odd_core) if prev else _proj_odd_body,
        grid=(n_batch, nt),
        in_specs=[row] + prev_specs + [_mod_spec(n_batch), full(g), full(w_qkz), full(wvt)],
        out_specs=out_specs,
        out_shape=out_shape,
        input_output_aliases={0: len(out_specs) - 1} if prev else {},
        compiler_params=_cparams(("parallel", "arbitrary")),
        name="proj_odd",
    )(h, *prev_args, mods_l, g, w_qkz, wvt)


def _lane_half_masks(rows, dtype):
    lane = lax.broadcasted_iota(jnp.int32, (rows, LANES), 1)
    lo = lane < HEAD_DIM
    return lo, jnp.logical_not(lo)


def _win_fill_mask(mask_ref):
    r = lax.broadcasted_iota(jnp.int32, (2 * TILE, TILE), 0)
    c = lax.broadcasted_iota(jnp.int32, (2 * TILE, TILE), 1)
    for case, (back, first) in enumerate(((A_WINDOW, A_WINDOW), (A_WINDOW, 0), (2 * A_WINDOW, 0))):
        ok = (jnp.abs(r - back - c) <= A_WINDOW) & (r >= first)
        mask_ref[case] = jnp.where(ok, 0.0, NEG)
    mask_ref[3] = jnp.full((2 * TILE, TILE), NEG, jnp.float32)


def _win_body(sink_ref, q_ref, k_ref, vt_ref, o_ref, mask_ref, sa_ref, sb_ref, p_ref, *, nt):
    assert TILE == 2 * A_WINDOW
    pair = pl.program_id(1)
    _win_fill_mask(mask_ref)
    n_keys = 3 * TILE
    n128 = nt * (TILE // LANES)
    ones = jnp.ones((16, n_keys), jnp.bfloat16)
    zero = jnp.zeros((), jnp.bfloat16)
    blk = LANES
    masks = _lane_half_masks(TILE, None)
    sinks = tuple(sink_ref[2 * pair + hh] * LOG2E for hh in range(2))

    def window(i):
        return jnp.clip(2 * i - 1, 0, n128 - 4)

    def scores(i, s_ref, heads):
        case = jnp.where(i == 0, 3, jnp.where(i == 1, 0, jnp.where(i == nt - 1, 2, 1)))
        k_ctx = k_ref[:TILE, :]
        k_loc = k_ref[pl.ds(pl.multiple_of(window(i) * LANES, LANES), 2 * TILE), :]
        q2 = q_ref[pl.ds(pl.multiple_of(i * TILE, TILE), TILE), :]
        m = []
        for hh in heads:
            qm = jnp.where(masks[hh], q2, zero)
            s_ctx = _dot_t(k_ctx, qm)
            s_ref[hh, :TILE, :] = s_ctx
            s_loc = _dot_t(k_loc, qm) + mask_ref[case]
            s_ref[hh, TILE:, :] = s_loc
            m.append(jnp.maximum(jnp.maximum(jnp.max(s_ctx, axis=0, keepdims=True),
                                             jnp.max(s_loc, axis=0, keepdims=True)), sinks[hh]))
        return m

    def update(i, s_ref, m, heads):
        w0 = window(i)
        vt = jnp.concatenate([vt_ref[0], vt_ref[1]] + [vt_ref[w0 + j] for j in range(4)], axis=1)
        outs = []
        for hh in heads:
            for c in range(n_keys // blk):
                p = jnp.exp2(s_ref[hh, c * blk:(c + 1) * blk, :] - m[hh])
                p_ref[hh, c * blk:(c + 1) * blk, :] = p.astype(jnp.bfloat16)
            vt_h = jnp.concatenate([vt[hh * HEAD_DIM:(hh + 1) * HEAD_DIM], ones], axis=0)
            acc = _dot(vt_h, p_ref[hh])
            denom = acc[HEAD_DIM:HEAD_DIM + 1] + jnp.exp2(sinks[hh] - m[hh])
            outs.append(acc[:HEAD_DIM] * (1.0 / denom))
        return outs

    def store(i, outs):
        o_ref[pl.ds(pl.multiple_of(i * TILE, TILE), TILE), :] = jnp.concatenate(outs, axis=0).T.astype(jnp.bfloat16)

    def step(i_next, s_next, i_cur, s_cur, m_cur):
        m_next, outs = [], []
        for hh in range(2):
            m_next += scores(i_next, s_next, (hh,))
            outs += update(i_cur, s_cur, m_cur, (hh,))
        store(i_cur, outs)
        return tuple(m_next)

    def body(t, m_a):
        for j in range(0, TILES_PER_TRIP, 2):
            i = TILES_PER_TRIP * t + j
            m_b = step(i + 1, sb_ref, i, sa_ref, m_a)
            m_a = step(i + 2, sa_ref, i + 1, sb_ref, m_b)
        return m_a

    m_a = lax.fori_loop(0, (nt - 1) // TILES_PER_TRIP, body, tuple(scores(0, sa_ref, (0, 1))))
    store(nt - 1, update(nt - 1, sa_ref, m_a, (0, 1)))


def _win_attn(qa, ka, vat, sink, ctx_len):
    n_batch, t_len, _ = qa.shape
    nt = t_len // TILE
    assert ctx_len == TILE and nt >= 4 and (nt - 1) % TILES_PER_TRIP == 0, "whole loop trips plus a last tile"
    return pl.pallas_call(
        functools.partial(_win_body, nt=nt),
        grid=(n_batch, A_HEADS // 2),
        in_specs=[pl.BlockSpec(memory_space=pltpu.SMEM),
                  pl.BlockSpec((None, t_len, LANES), lambda b, p: (b, 0, p)),
                  pl.BlockSpec((None, t_len, LANES), lambda b, p: (b, 0, 0)),
                  pl.BlockSpec((None,) + vat.shape[1:], lambda b, p: (b, 0, 0, 0))],
        out_specs=pl.BlockSpec((None, t_len, LANES), lambda b, p: (b, 0, p)),
        out_shape=jax.ShapeDtypeStruct((n_batch, t_len, A_HEADS * HEAD_DIM), jnp.bfloat16),
        scratch_shapes=[pltpu.VMEM((4, 2 * TILE, TILE), jnp.float32),
                        pltpu.VMEM((2, 3 * TILE, TILE), jnp.float32),
                        pltpu.VMEM((2, 3 * TILE, TILE), jnp.float32),
                        pltpu.VMEM((2, 3 * TILE, TILE), jnp.bfloat16)],
        compiler_params=_cparams(("parallel", "parallel")),
        name="win_attn",
    )(sink, qa, ka, vat)


def _mla_body(q_ref, k_ref, vt_ref, o_ref, sa_ref, sb_ref, sc_ref, p_ref, *, nt, n_chunks):
    sub = MLA_CHUNK // TILE
    blk = LANES
    ones_rows = 16
    n_items = 1 + n_chunks
    bufs = (sa_ref, sb_ref, sc_ref)
    assert n_items % len(bufs) == 0

    def tile_rows(t):
        return pl.ds(pl.multiple_of(t * TILE, TILE), TILE)

    def keys(n):
        return (0, 0, 1) if n == 0 else (TILE + (n - 1) * MLA_CHUNK, 1 + (n - 1) * sub, sub)

    def scores_h(qi, n, hh):
        start, _, n_tiles = keys(n)
        size = n_tiles * TILE
        cols = slice(hh * LANES, (hh + 1) * LANES)
        s = _dot_t(k_ref[start:start + size, cols], q_ref[tile_rows(qi), cols])
        bufs[n % len(bufs)][hh, :size, :] = s
        return jnp.max(s, axis=0, keepdims=True)

    def update_h(st, cm, n, hh):
        _, tile0, n_tiles = keys(n)
        size = n_tiles * TILE
        s_ref = bufs[n % len(bufs)]
        m_prev, acc = st
        m_new = jnp.maximum(m_prev, cm)
        alpha = jnp.exp2(m_prev - m_new)
        for c in range(size // blk):
            p = jnp.exp2(s_ref[hh, c * blk:(c + 1) * blk, :] - m_new)
            p_ref[hh, c * blk:(c + 1) * blk, :] = p.astype(jnp.bfloat16)
        vt = jnp.concatenate([vt_ref[tile0 + c, hh * B_V:(hh + 1) * B_V, :] for c in range(n_tiles)], axis=1)
        vt = jnp.concatenate([vt, jnp.ones((ones_rows, size), jnp.bfloat16)], axis=0)
        return m_new, acc * alpha + _dot(vt, p_ref[hh, :size, :])

    def scores(qi, n):
        return tuple(scores_h(qi, n, hh) for hh in range(2))

    def update(state, cm, n):
        return tuple(update_h(state[hh], cm[hh], n, hh) for hh in range(2))

    def finish(state, qi):
        o_t = jnp.concatenate([acc[:B_V] * (1.0 / acc[B_V:B_V + 1]) for _, acc in state], axis=0)
        o_ref[tile_rows(qi), :] = o_t.T.astype(jnp.bfloat16)

    init = tuple((jnp.full((1, TILE), NEG, jnp.float32), jnp.zeros((B_V + ones_rows, TILE), jnp.float32))
                 for _ in range(2))
    finish(update(init, scores(0, 0), 0), 0)

    def body(qi, cm):
        state = list(init)
        cm = list(cm)
        nxt = jnp.minimum(qi + 1, nt - 1)
        for n in range(n_items):
            for hh in range(2):
                cm_next = scores_h(qi, n + 1, hh) if n + 1 < n_items else scores_h(nxt, 0, hh)
                state[hh] = update_h(state[hh], cm[hh], n, hh)
                cm[hh] = cm_next
        finish(state, qi)
        return tuple(cm)

    lax.fori_loop(1, nt, body, scores(1, 0))


def _mla_attn(qb, kb, vbt, ctx_len):
    n_batch, t_len, _ = qb.shape
    nt = t_len // TILE
    seq = t_len - ctx_len
    assert ctx_len == TILE and seq % MLA_CHUNK == 0 and MLA_CHUNK % TILE == 0
    n_chunks = seq // MLA_CHUNK
    assert (1 + n_chunks) % 3 == 0, "context keys + latent chunks rotate over three score buffers"
    body = functools.partial(_mla_body, nt=nt, n_chunks=n_chunks)
    s_buf = pltpu.VMEM((2, MLA_CHUNK, TILE), jnp.float32)
    return pl.pallas_call(
        body,
        grid=(n_batch, B_HEADS // 2),
        in_specs=[pl.BlockSpec((None, t_len, 2 * LANES), lambda b, p: (b, 0, p)),
                  pl.BlockSpec((None, t_len, 2 * LANES), lambda b, p: (b, 0, p)),
                  pl.BlockSpec((None, nt, 2 * B_V, TILE), lambda b, p: (b, 0, p, 0))],
        out_specs=pl.BlockSpec((None, t_len, LANES), lambda b, p: (b, 0, p)),
        out_shape=jax.ShapeDtypeStruct((n_batch, t_len, B_HEADS * B_V), jnp.bfloat16),
        scratch_shapes=[s_buf, s_buf, s_buf, pltpu.VMEM((2, MLA_CHUNK, TILE), jnp.bfloat16)],
        compiler_params=_cparams(("parallel", "parallel")),
        name="mla_attn",
    )(qb, kb, vbt)


def _nbr_row_valid(case, jr, t):
    rpt = TILE // GRID_W
    lo = (rpt, jr + rpt - C_KH // 2, 2 * rpt - C_KH)[case]
    return lo <= t < lo + C_KH


def _nbr_fill_bias(tz_ref, bias_ref):
    rpt = TILE // GRID_W
    lane_lo = lax.broadcasted_iota(jnp.int32, (GRID_W, LANES), 1) < GRID_W
    neg = jnp.full((GRID_W, LANES), NEG, jnp.float32)
    for hh in range(2):
        bias_ref[3, hh] = jnp.full((3 * TILE, TILE), NEG, jnp.float32)
        for case in range(3):
            for t in range(3 * rpt):
                for jp in range(rpt // 2):
                    ok0 = _nbr_row_valid(case, 2 * jp, t)
                    ok1 = _nbr_row_valid(case, 2 * jp + 1, t)
                    if ok0 or ok1:
                        blk = tz_ref[hh, t - rpt - 2 * jp + C_KH - 2]
                        if not ok1:
                            blk = jnp.where(lane_lo, blk, NEG)
                        elif not ok0:
                            blk = jnp.where(lane_lo, NEG, blk)
                    else:
                        blk = neg
                    bias_ref[case, hh, t * GRID_W:(t + 1) * GRID_W, jp * LANES:(jp + 1) * LANES] = blk


def _nbr_body(q_ref, k_ref, vt_ref, tz_ref, o_ref, bias_ref, sa_ref, sb_ref, p_ref, *, nt):
    _nbr_fill_bias(tz_ref, bias_ref)
    n_keys = 4 * TILE
    ones = jnp.ones((16, n_keys), jnp.bfloat16)
    zero = jnp.zeros((), jnp.bfloat16)
    blk = LANES
    masks = _lane_half_masks(TILE, None)

    def key_tiles(i):
        return 0, jnp.clip(i - 1, 1, nt - 1), i, jnp.clip(i + 1, 1, nt - 1)

    def rows(t):
        return pl.ds(pl.multiple_of(t * TILE, TILE), TILE)

    def scores(i, s_ref, heads):
        case = jnp.where(i == 0, 3, jnp.where(i == 1, 0, jnp.where(i == nt - 1, 2, 1)))
        tc, t0, t1, t2 = key_tiles(i)
        k_ctx = k_ref[rows(tc), :]
        k_loc = jnp.concatenate([k_ref[rows(t0), :], k_ref[rows(t1), :], k_ref[rows(t2), :]], axis=0)
        q2 = q_ref[rows(i), :]
        m = []
        for hh in heads:
            qm = jnp.where(masks[hh], q2, zero)
            s_ctx = _dot_t(k_ctx, qm)
            s_ref[hh, :TILE, :] = s_ctx
            s_loc = _dot_t(k_loc, qm) + bias_ref[case, hh]
            s_ref[hh, TILE:, :] = s_loc
            m.append(jnp.maximum(jnp.max(s_ctx, axis=0, keepdims=True), jnp.max(s_loc, axis=0, keepdims=True)))
        return m

    def update(i, s_ref, m, heads):
        vt = jnp.concatenate([vt_ref[t] for t in key_tiles(i)], axis=1)
        outs = []
        for hh in heads:
            for c in range(n_keys // blk):
                p = jnp.exp2(s_ref[hh, c * blk:(c + 1) * blk, :] - m[hh])
                p_ref[hh, c * blk:(c + 1) * blk, :] = p.astype(jnp.bfloat16)
            vt_h = jnp.concatenate([vt[hh * HEAD_DIM:(hh + 1) * HEAD_DIM], ones], axis=0)
            acc = _dot(vt_h, p_ref[hh])
            outs.append(acc[:HEAD_DIM] * (1.0 / acc[HEAD_DIM:HEAD_DIM + 1]))
        return outs

    def store(i, outs):
        o_ref[rows(i), :] = jnp.concatenate(outs, axis=0).T.astype(jnp.bfloat16)

    def step(i_next, s_next, i_cur, s_cur, m_cur):
        m_next, outs = [], []
        for hh in range(2):
            m_next += scores(i_next, s_next, (hh,))
            outs += update(i_cur, s_cur, m_cur, (hh,))
        store(i_cur, outs)
        return tuple(m_next)

    def body(t, m_a):
        for j in range(0, TILES_PER_TRIP, 2):
            i = TILES_PER_TRIP * t + j
            m_b = step(i + 1, sb_ref, i, sa_ref, m_a)
            m_a = step(i + 2, sa_ref, i + 1, sb_ref, m_b)
        return m_a

    m_a = lax.fori_loop(0, (nt - 1) // TILES_PER_TRIP, body, tuple(scores(0, sa_ref, (0, 1))))
    store(nt - 1, update(nt - 1, sa_ref, m_a, (0, 1)))


def _nbr_attn(q, k, vt, tz, ctx_len):
    n_batch, t_len, d = q.shape
    nt = t_len // TILE
    assert ctx_len == TILE and TILE % GRID_W == 0 and C_KH == 2 * (TILE // GRID_W)
    assert nt >= 4, "needs distinct first / interior / last latent tiles"
    assert (nt - 1) % TILES_PER_TRIP == 0, "whole loop trips plus a last tile"
    seq = pl.BlockSpec((None, t_len, LANES), lambda b, p: (b, 0, p))
    return pl.pallas_call(
        functools.partial(_nbr_body, nt=nt),
        grid=(n_batch, C_HEADS // 2),
        in_specs=[seq, seq, pl.BlockSpec((None, nt, LANES, TILE), lambda b, p: (b, 0, p, 0)),
                  pl.BlockSpec((2,) + tz.shape[1:], lambda b, p: (p, 0, 0, 0))],
        out_specs=seq,
        out_shape=jax.ShapeDtypeStruct((n_batch, t_len, d), jnp.bfloat16),
        scratch_shapes=[pltpu.VMEM((4, 2, 3 * TILE, TILE), jnp.float32),
                        pltpu.VMEM((2, 4 * TILE, TILE), jnp.float32),
                        pltpu.VMEM((2, 4 * TILE, TILE), jnp.float32),
                        pltpu.VMEM((2, 4 * TILE, TILE), jnp.bfloat16)],
        compiler_params=_cparams(("parallel", "parallel")),
        name="nbr_attn",
    )(q, k, vt, tz)


def _nbr_toeplitz(rpb):
    n_off = 2 * C_KW - 1
    pad = GRID_W - C_KW
    ext = jnp.pad(rpb, ((0, 0), (0, 0), (pad, pad)), mode="edge")[:, :, ::-1]
    assert ext.shape[-1] == n_off + 2 * pad == 2 * GRID_W - 1
    n_h, n_r, length = ext.shape
    flat = jnp.tile(jnp.pad(ext, ((0, 0), (0, 0), (0, 1))), (1, 1, GRID_W))
    tz = flat[:, :, GRID_W - 1:GRID_W - 1 + GRID_W * length].reshape(n_h, n_r, GRID_W, length)[..., :GRID_W]
    qc = np.arange(GRID_W)[None, :]
    kc = np.arange(GRID_W)[:, None]
    cs = np.clip(qc - C_KW // 2, 0, GRID_W - C_KW)
    col_ok = jnp.asarray((kc >= cs) & (kc < cs + C_KW))
    tz = jnp.where(col_ok, tz * LOG2E, NEG)
    return jnp.concatenate([tz[:, 1:], tz[:, :-1]], axis=-1)


def _residual(h_ref, mod_ref, o1_ref, o2_ref, gate_ref, w_ref):
    half = D_MODEL // 2
    g = gate_ref[...].astype(jnp.float32)
    og1 = (o1_ref[...].astype(jnp.float32) * g[:, :half]).astype(jnp.bfloat16)
    og2 = (o2_ref[...].astype(jnp.float32) * g[:, half:]).astype(jnp.bfloat16)
    y = _dot(og1, w_ref[:half, :]) + _dot(og2, w_ref[half:, :])
    return h_ref[...] + mod_ref[:, 2 * D_MODEL:] * y


def _residual_then_proj_body(h_ref, mod_ref, o1_ref, o2_ref, gate_ref, w_ref, *rest, core):
    *proj_refs, h_out_ref = rest
    h_new = _residual(h_ref, mod_ref, o1_ref, o2_ref, gate_ref, w_ref)
    h_out_ref[...] = h_new
    core(h_new, *proj_refs)


def _prev_operands(prev, n_batch):
    mods_p, o1, o2, gate, w, o2_block = prev
    d = D_MODEL
    row = lambda wd, blk=0: pl.BlockSpec((None, TILE, wd), lambda b, i: (b, i, blk))
    specs = [_mod_spec(n_batch), row(d // 2), row(d // 2, o2_block), row(d), pl.BlockSpec(w.shape, lambda b, i: (0, 0))]
    return specs, [mods_p, o1, o2, gate, w]


def _out_body(h_ref, mod_ref, o1_ref, o2_ref, gate_ref, w_ref, *rest, final):
    h_new = _residual(h_ref, mod_ref, o1_ref, o2_ref, gate_ref, w_ref)
    if final:
        fg_ref, out_ref = rest
        out_ref[...] = _rms(h_new, fg_ref[...])
    else:
        rest[0][...] = h_new


def _out_proj(h, mods_l, o1, o2, gate, w, o2_block, final_g=None):
    n_batch, t_len, d = h.shape
    nt = t_len // TILE
    final = final_g is not None
    off = 1 if final else 0
    row = lambda wd, blk=0: pl.BlockSpec((None, TILE, wd), lambda b, i: (b, i + off, blk))
    mod = pl.BlockSpec((None, 1, 3 * D_MODEL), lambda b, i: (jnp.where(i + off == 0, n_batch, b), 0, 0))
    full = lambda a: pl.BlockSpec(a.shape, lambda b, i: (0,) * a.ndim)
    in_specs = [row(d), mod, row(d // 2), row(d // 2, o2_block), row(d), full(w)]
    args = [h, mods_l, o1, o2, gate, w]
    if final:
        in_specs.append(full(final_g))
        args.append(final_g)
        out_spec = pl.BlockSpec((None, TILE, d), lambda b, i: (b, i, 0))
        out_shape = jax.ShapeDtypeStruct((n_batch, t_len - TILE, d), jnp.float32)
        aliases = {}
    else:
        out_spec = row(d)
        out_shape = jax.ShapeDtypeStruct(h.shape, jnp.float32)
        aliases = {0: 0}
    return pl.pallas_call(
        functools.partial(_out_body, final=final),
        grid=(n_batch, nt - off),
        in_specs=in_specs,
        out_specs=out_spec,
        out_shape=out_shape,
        input_output_aliases=aliases,
        compiler_params=_cparams(("parallel", "arbitrary")),
        name="out_proj_final" if final else "out_proj",
    )(*args)


def _deinterleave(n):
    return np.concatenate([np.arange(0, n, 2), np.arange(1, n, 2)])


def _take_cols(w, cols):
    cols = np.asarray(cols)
    picked = jnp.take(w, jnp.asarray(np.maximum(cols, 0)), axis=1)
    return jnp.where(jnp.asarray(cols >= 0)[None, :], picked, 0.0)


def _even_layout():
    perm = _deinterleave(HEAD_DIM)
    qa0, ka0 = 0, A_HEADS * HEAD_DIM
    va0 = ka0 + A_KV_HEADS * HEAD_DIM
    za0 = va0 + A_KV_HEADS * HEAD_DIM
    cq0 = za0 + A_HEADS * HEAD_DIM
    ckv0 = cq0 + B_Q_LORA
    kr0 = ckv0 + B_KV_LORA
    zb0 = kr0 + B_ROPE
    cols = []
    cols += [qa0 + hd * HEAD_DIM + perm for hd in _A_SLOTS]
    cols += [ka0 + kv * HEAD_DIM + perm for kv in range(A_KV_HEADS)]
    va_cols = va0 + np.arange(A_KV_HEADS * HEAD_DIM)
    cols += [za0 + hd * HEAD_DIM + np.arange(HEAD_DIM) for hd in _A_SLOTS]
    cols += [zb0 + np.arange(B_HEADS * B_V)]
    cols += [cq0 + np.arange(B_Q_LORA), ckv0 + np.arange(B_KV_LORA)]
    cols += [np.full(B_NOPE, -1), kr0 + _deinterleave(B_ROPE), np.full(LANES - B_NOPE - B_ROPE, -1)]
    in_cols = np.concatenate(cols)
    assert in_cols.shape[0] == _AB_COLS

    uq = []
    for hd in range(B_HEADS):
        base = hd * (B_NOPE + B_ROPE)
        uq += [base + np.arange(B_NOPE), base + B_NOPE + _deinterleave(B_ROPE),
               np.full(LANES - B_NOPE - B_ROPE, -1)]
    ukv_k, ukv_v = [], []
    pad = np.full(LANES - B_NOPE, -1)
    for hd in range(B_HEADS):
        base = hd * (B_NOPE + B_V)
        ukv_k += [base + np.arange(B_NOPE), pad]
        ukv_v += [base + B_NOPE + np.arange(B_V)]
    out_rows = np.concatenate([hd * HEAD_DIM + np.arange(HEAD_DIM) for hd in _A_SLOTS]
                              + [A_HEADS * HEAD_DIM + np.arange(B_HEADS * B_V)])
    return in_cols, va_cols, np.concatenate(uq), np.concatenate(ukv_k), np.concatenate(ukv_v), out_rows


def _rope_tables(seq_len, ctx_len):
    t = jnp.arange(seq_len, dtype=jnp.int32)
    row = (t // GRID_W).astype(jnp.float32)
    col = (t % GRID_W).astype(jnp.float32)

    def angles(rot_dim):
        ppa = rot_dim // 4
        inv = ROPE_THETA ** (-jnp.arange(ppa, dtype=jnp.float32) / ppa)
        ang = jnp.concatenate([row[:, None] * inv, col[:, None] * inv], axis=-1)
        return jnp.cos(ang), jnp.sin(ang)

    def with_ctx(c, s1, s2):
        pad = lambda a, v: jnp.concatenate([jnp.full((ctx_len, LANES), v, jnp.float32), a], axis=0)
        return pad(c, 1.0), pad(s1, 0.0), pad(s2, 0.0)

    cos_a, sin_a = angles(HEAD_DIM)
    z32 = jnp.zeros_like(sin_a)
    tabs_a = with_ctx(jnp.tile(cos_a, (1, 4)), jnp.tile(jnp.concatenate([-sin_a, z32], 1), (1, 2)),
                      jnp.tile(jnp.concatenate([z32, sin_a], 1), (1, 2)))
    cos_b, sin_b = angles(B_ROPE)
    one64 = jnp.ones((seq_len, B_NOPE), jnp.float32)
    z64 = jnp.zeros((seq_len, B_NOPE), jnp.float32)
    z16 = jnp.zeros_like(sin_b)
    one32 = jnp.ones((seq_len, LANES - B_NOPE - B_ROPE), jnp.float32)
    z32b = jnp.zeros((seq_len, LANES - B_NOPE - B_ROPE), jnp.float32)
    tabs_b = with_ctx(jnp.concatenate([one64, cos_b, cos_b, one32], 1),
                      jnp.concatenate([z64, -sin_b, z16, z32b], 1),
                      jnp.concatenate([z64, z16, sin_b, z32b], 1))
    return tabs_a, tabs_b


def kernel(x, c, ctx, c_ctx, ada_w, ada_b, norm_g, ab_in_w, ab_out_w, a_sink, b_q_norm_g, b_w_uq,
           b_kv_norm_g, b_w_ukv, c_in_w, c_out_w, c_rpb, final_g):
    n_batch, seq_len, d = x.shape
    ctx_len = ctx.shape[1]
    depth = ada_w.shape[0]
    assert d == D_MODEL and ctx_len == TILE and seq_len % TILE == 0
    assert depth % 2 == 0, "the final norm is fused into the last (odd) layer's output projection"
    bf = jnp.bfloat16

    mod_rows = -(-(n_batch + 1) // 8) * 8
    cc = jnp.concatenate([c, c_ctx[None, :], jnp.zeros((mod_rows - n_batch - 1, d), jnp.float32)], axis=0)
    mods = _mods(cc, ada_w, ada_b).reshape(depth, mod_rows, 1, 3 * d)

    in_cols, va_cols, uq_cols, uk_cols, uv_cols, out_rows = _even_layout()
    tabs_a, tabs_b = _rope_tables(seq_len, ctx_len)
    h = (ctx, x)
    row2 = lambda v: v.reshape(1, -1)

    prev = None
    for layer in range(depth):
        i = layer // 2
        g = row2(norm_g[layer])
        if layer % 2 == 0:
            in_w, ukv_w = ab_in_w[i].astype(bf), b_w_ukv[i].astype(bf)
            w_in = _take_cols(in_w, in_cols)
            wuq = _take_cols(b_w_uq[i].astype(bf), uq_cols)
            wuk = _take_cols(ukv_w, uk_cols)
            wvt = _take_cols(ukv_w, uv_cols).T
            w_out = jnp.take(ab_out_w[i].astype(bf), jnp.asarray(out_rows), axis=0)
            sink = jnp.take(a_sink[i], jnp.asarray(_A_SLOTS))
            wvat = _take_cols(in_w, va_cols).T
            outs = _proj_even(h, mods[layer], g, w_in, wvat, row2(b_q_norm_g[i]), wuq,
                              row2(b_kv_norm_g[i]), wuk, wvt, tabs_a, tabs_b, prev=prev)
            if layer == 0 or prev:
                h = outs[-1]
            qa, ka, vat, gate, qb, kb, vbt = outs[:7]
            o1 = _win_attn(qa, ka, vat, sink, ctx_len)
            o2 = _mla_attn(qb, kb, vbt, ctx_len)
            o2_block = 0
        else:
            w = c_in_w[i].astype(bf)
            w_qkz = jnp.concatenate([w[:, :2 * d], w[:, 3 * d:]], axis=1)
            outs = _proj_odd(h, mods[layer], g, w_qkz, w[:, 2 * d:3 * d].T, prev=prev)
            if prev:
                h = outs[-1]
            q, k, vt, gate = outs[:4]
            o1 = o2 = _nbr_attn(q, k, vt, _nbr_toeplitz(c_rpb[i]), ctx_len)
            o2_block = 1
            w_out = c_out_w[i].astype(bf)
        prev = (mods[layer], o1, o2, gate, w_out, o2_block)
    return _out_proj(h, *prev, final_g=row2(final_g))
```

```python
import functools
import math

import numpy as np
import jax
import jax.numpy as jnp
from jax import lax
from jax.experimental import pallas as pl
from jax.experimental.pallas import tpu as pltpu

D_MODEL = 1024
HEAD_DIM = 64
GRID_W = 64
ROPE_THETA = 10000.0
NORM_EPS = 1e-6
A_HEADS = 8
A_KV_HEADS = 2
A_WINDOW = 128
B_HEADS = 8
B_Q_LORA = 384
B_KV_LORA = 256
B_NOPE = 64
B_ROPE = 32
B_V = 64
C_HEADS = 16
C_KH = 8
C_KW = 16

LANES = 128
TILE = 256
MLA_CHUNK = 1024
TILES_PER_TRIP = 4
LOG2E = math.log2(math.e)
NEG = -1e30
VMEM_LIMIT = 48 * 1024 * 1024

_QA, _KA, _ZA, _ZB, _CQ, _CKV, _KR, _AB_COLS = 0, 512, 640, 1152, 1664, 2048, 2304, 2432
_A_SLOTS = (0, 4, 1, 5, 2, 6, 3, 7)


def _cparams(sem):
    return pltpu.CompilerParams(dimension_semantics=sem, vmem_limit_bytes=VMEM_LIMIT)


def _silu(x):
    return x * (1.0 / (1.0 + jnp.exp(-x)))


def _dot(a, b):
    return jnp.dot(a, b, preferred_element_type=jnp.float32)


def _dot_t(a, b):
    return lax.dot_general(a, b, (((1,), (1,)), ((), ())), preferred_element_type=jnp.float32)


def _mods_body(c_ref, w_ref, b_ref, o_ref):
    sc = _silu(c_ref[...])
    o_ref[...] = jnp.dot(sc, w_ref[...], preferred_element_type=jnp.float32,
                         precision=lax.Precision.HIGHEST) + b_ref[...]


def _mods(cc, ada_w, ada_b):
    depth, d, d3 = ada_w.shape
    rows = cc.shape[0]
    tn = 512
    return pl.pallas_call(
        _mods_body,
        grid=(depth, d3 // tn),
        in_specs=[
            pl.BlockSpec((rows, d), lambda l, j: (0, 0)),
            pl.BlockSpec((None, d, tn), lambda l, j: (l, 0, j)),
            pl.BlockSpec((None, 1, tn), lambda l, j: (l, 0, j)),
        ],
        out_specs=pl.BlockSpec((None, rows, tn), lambda l, j: (l, 0, j)),
        out_shape=jax.ShapeDtypeStruct((depth, rows, d3), jnp.float32),
        compiler_params=_cparams(("arbitrary", "arbitrary")),
        name="mods",
    )(cc, ada_w, ada_b.reshape(depth, 1, d3))


def _mod_spec(n_batch):
    return pl.BlockSpec((None, 1, 3 * D_MODEL), lambda b, i: (jnp.where(i == 0, n_batch, b), 0, 0))


def _norm_mod(h, g, mod):
    y = h * lax.rsqrt(jnp.mean(h * h, axis=-1, keepdims=True) + NORM_EPS) * g
    return y * (1.0 + mod[:, D_MODEL:2 * D_MODEL]) + mod[:, :D_MODEL]


def _rms(x, g):
    return x * lax.rsqrt(jnp.mean(x * x, axis=-1, keepdims=True) + NORM_EPS) * g


def _rope(x, c, s1, s2, half):
    return x * c + pltpu.roll(x, LANES - half, 1) * s1 + pltpu.roll(x, half, 1) * s2


def _proj_even_body(h_ref, *rest):
    _proj_even_core(h_ref[...], *rest)


def _proj_first_body(ctx_ref, x_ref, *rest):
    *rest, h_out_ref = rest
    h = jnp.where(pl.program_id(1) == 0, ctx_ref[...], x_ref[...])
    h_out_ref[...] = h
    _proj_even_core(h, *rest)


def _proj_even_core(h, mod_ref, g_ref, w_ref, wvat_ref, qg_ref, wuq_ref, kvg_ref, wukv_ref, wvt_ref,
                    ca_ref, sa1_ref, sa2_ref, cb_ref, sb1_ref, sb2_ref,
                    qa_ref, ka_ref, vat_ref, gate_ref, qb_ref, kb_ref, vbt_ref):
    u = _norm_mod(h, g_ref[...], mod_ref[...]).astype(jnp.bfloat16)
    ca, sa1, sa2 = ca_ref[...], sa1_ref[...], sa2_ref[...]
    cb, sb1, sb2 = cb_ref[...], sb1_ref[...], sb2_ref[...]

    qk = _dot(u, w_ref[:, _QA:_ZA])
    a_scale = HEAD_DIM ** -0.5 * LOG2E
    for j in range(4):
        blk = _rope(qk[:, j * LANES:(j + 1) * LANES], ca, sa1, sa2, HEAD_DIM // 2)
        qa_ref[:, j * LANES:(j + 1) * LANES] = (blk * a_scale).astype(jnp.bfloat16)
    ka_ref[...] = _rope(qk[:, _KA:_ZA], ca, sa1, sa2, HEAD_DIM // 2).astype(jnp.bfloat16)
    vat = _dot_t(wvat_ref[...], u).astype(jnp.bfloat16)
    for j in range(TILE // LANES):
        vat_ref[j] = vat[:, j * LANES:(j + 1) * LANES]
    gate_ref[...] = _silu(_dot(u, w_ref[:, _ZA:_CQ])).astype(jnp.bfloat16)

    lat = _dot(u, w_ref[:, _CQ:_AB_COLS])
    cq = _rms(lat[:, :B_Q_LORA], qg_ref[...]).astype(jnp.bfloat16)
    ckv = _rms(lat[:, B_Q_LORA:B_Q_LORA + B_KV_LORA], kvg_ref[...]).astype(jnp.bfloat16)
    kr = _rope(lat[:, B_Q_LORA + B_KV_LORA:], cb, sb1, sb2, B_ROPE // 2)

    qb = _dot(cq, wuq_ref[...])
    b_scale = (B_NOPE + B_ROPE) ** -0.5 * LOG2E
    kb = _dot(ckv, wukv_ref[...])
    for hd in range(B_HEADS):
        sl = slice(hd * LANES, (hd + 1) * LANES)
        qb_ref[:, sl] = (_rope(qb[:, sl], cb, sb1, sb2, B_ROPE // 2) * b_scale).astype(jnp.bfloat16)
        kb_ref[:, sl] = (kb[:, sl] + kr).astype(jnp.bfloat16)
    vbt_ref[...] = _dot_t(wvt_ref[...], ckv).astype(jnp.bfloat16)


def _proj_even(h, mods_l, g, w_in, wvat, qg, wuq, kvg, wukv, wvt, tabs_a, tabs_b, prev=None):
    first = isinstance(h, tuple)
    if first:
        ctx, x = h
        n_batch, seq_len, d = x.shape
        t_len = ctx.shape[1] + seq_len
    else:
        n_batch, t_len, d = h.shape
    nt = t_len // TILE
    row = lambda w: pl.BlockSpec((None, TILE, w), lambda b, i: (b, i, 0))
    full = lambda a: pl.BlockSpec(a.shape, lambda b, i: (0,) * a.ndim)
    tab = pl.BlockSpec((TILE, LANES), lambda b, i: (i, 0))
    bf = jnp.bfloat16
    sds = lambda *shape: jax.ShapeDtypeStruct((n_batch,) + shape, bf)
    n_vt = B_HEADS * B_V
    per_tile = TILE // LANES
    if first:
        h_specs = [pl.BlockSpec((None, TILE, d), lambda b, i: (b, 0, 0)),
                   pl.BlockSpec((None, TILE, d), lambda b, i: (b, jnp.maximum(i - 1, 0), 0))]
        h_args = [ctx, x]
    else:
        prev_specs, prev_args = _prev_operands(prev, n_batch) if prev else ([], [])
        h_specs, h_args = [row(d)] + prev_specs, [h] + prev_args
    out_specs = [row(512), row(LANES),
                 pl.BlockSpec((None, per_tile, LANES, LANES), lambda b, i: (b, i, 0, 0)),
                 row(d), row(d), row(d),
                 pl.BlockSpec((None, None, n_vt, TILE), lambda b, i: (b, i, 0, 0))]
    out_shape = [sds(t_len, 512), sds(t_len, LANES), sds(t_len // LANES, LANES, LANES),
                 sds(t_len, d), sds(t_len, d), sds(t_len, d), sds(nt, n_vt, TILE)]
    if first or prev:
        out_specs.append(row(d))
        out_shape.append(jax.ShapeDtypeStruct((n_batch, t_len, d), jnp.float32))
    return pl.pallas_call(
        _proj_first_body if first else
        functools.partial(_residual_then_proj_body, core=_proj_even_core) if prev else _proj_even_body,
        grid=(n_batch, nt),
        in_specs=h_specs + [_mod_spec(n_batch), full(g), full(w_in), full(wvat), full(qg), full(wuq), full(kvg),
                            full(wukv), full(wvt)] + [tab] * 6,
        out_specs=out_specs,
        out_shape=out_shape,
        input_output_aliases={0: len(out_specs) - 1} if prev else {},
        compiler_params=_cparams(("parallel", "arbitrary")),
        name="proj_even",
    )(*h_args, mods_l, g, w_in, wvat, qg, wuq, kvg, wukv, wvt, *tabs_a, *tabs_b)


def _proj_odd_body(h_ref, *rest):
    _proj_odd_core(h_ref[...], *rest)


def _proj_odd_core(h, mod_ref, g_ref, w_ref, wvt_ref, q_ref, k_ref, vt_ref, gate_ref):
    u = _norm_mod(h, g_ref[...], mod_ref[...]).astype(jnp.bfloat16)
    d = D_MODEL
    q_ref[...] = (_dot(u, w_ref[:, :d]) * (HEAD_DIM ** -0.5 * LOG2E)).astype(jnp.bfloat16)
    k_ref[...] = _dot(u, w_ref[:, d:2 * d]).astype(jnp.bfloat16)
    vt_ref[...] = _dot_t(wvt_ref[...], u).astype(jnp.bfloat16)
    gate_ref[...] = _silu(_dot(u, w_ref[:, 2 * d:])).astype(jnp.bfloat16)


def _proj_odd(h, mods_l, g, w_qkz, wvt, prev=None):
    n_batch, t_len, d = h.shape
    nt = t_len // TILE
    row = pl.BlockSpec((None, TILE, d), lambda b, i: (b, i, 0))
    row_t = pl.BlockSpec((None, None, d, TILE), lambda b, i: (b, i, 0, 0))
    full = lambda a: pl.BlockSpec(a.shape, lambda b, i: (0,) * a.ndim)
    out = jax.ShapeDtypeStruct((n_batch, t_len, d), jnp.bfloat16)
    prev_specs, prev_args = _prev_operands(prev, n_batch) if prev else ([], [])
    out_specs = [row, row, row_t, row]
    out_shape = [out, out, jax.ShapeDtypeStruct((n_batch, nt, d, TILE), jnp.bfloat16), out]
    if prev:
        out_specs.append(row)
        out_shape.append(jax.ShapeDtypeStruct(h.shape, jnp.float32))
    return pl.pallas_call(
        functools.partial(_residual_then_proj_body, core=_proj_odd_core) if prev else _proj_odd_body,
        grid=(n_batch, nt),
        in_specs=[row] + prev_specs + [_mod_spec(n_batch), full(g), full(w_qkz), full(wvt)],
        out_specs=out_specs,
        out_shape=out_shape,
        input_output_aliases={0: len(out_specs) - 1} if prev else {},
        compiler_params=_cparams(("parallel", "arbitrary")),
        name="proj_odd",
    )(h, *prev_args, mods_l, g, w_qkz, wvt)


def _lane_half_masks(rows):
    lane = lax.broadcasted_iota(jnp.int32, (rows, LANES), 1)
    lo = lane < HEAD_DIM
    return lo, jnp.logical_not(lo)


def _win_fill_mask(mask_ref):
    r = lax.broadcasted_iota(jnp.int32, (2 * TILE, TILE), 0)
    c = lax.broadcasted_iota(jnp.int32, (2 * TILE, TILE), 1)
    for case, (back, first) in enumerate(((A_WINDOW, A_WINDOW), (A_WINDOW, 0), (2 * A_WINDOW, 0))):
        ok = (jnp.abs(r - back - c) <= A_WINDOW) & (r >= first)
        mask_ref[case] = jnp.where(ok, 0.0, NEG)
    mask_ref[3] = jnp.full((2 * TILE, TILE), NEG, jnp.float32)


def _win_body(sink_ref, q_ref, k_ref, vt_ref, o_ref, mask_ref, sa_ref, sb_ref, p_ref, *, nt):
    assert TILE == 2 * A_WINDOW
    pair = pl.program_id(1)
    _win_fill_mask(mask_ref)
    n_keys = 3 * TILE
    n128 = nt * (TILE // LANES)
    ones = jnp.ones((16, n_keys), jnp.bfloat16)
    zero = jnp.zeros((), jnp.bfloat16)
    blk = LANES
    masks = _lane_half_masks(TILE)
    sinks = tuple(sink_ref[2 * pair + hh] * LOG2E for hh in range(2))

    def window(i):
        return jnp.clip(2 * i - 1, 0, n128 - 4)

    def scores(i, s_ref, heads):
        case = jnp.where(i == 0, 3, jnp.where(i == 1, 0, jnp.where(i == nt - 1, 2, 1)))
        k_ctx = k_ref[:TILE, :]
        k_loc = k_ref[pl.ds(pl.multiple_of(window(i) * LANES, LANES), 2 * TILE), :]
        q2 = q_ref[pl.ds(pl.multiple_of(i * TILE, TILE), TILE), :]
        m = []
        for hh in heads:
            qm = jnp.where(masks[hh], q2, zero)
            s_ctx = _dot_t(k_ctx, qm)
            s_ref[hh, :TILE, :] = s_ctx
            s_loc = _dot_t(k_loc, qm) + mask_ref[case]
            s_ref[hh, TILE:, :] = s_loc
            m.append(jnp.maximum(jnp.maximum(jnp.max(s_ctx, axis=0, keepdims=True),
                                             jnp.max(s_loc, axis=0, keepdims=True)), sinks[hh]))
        return m

    def update(i, s_ref, m, heads):
        w0 = window(i)
        vt = jnp.concatenate([vt_ref[0], vt_ref[1]] + [vt_ref[w0 + j] for j in range(4)], axis=1)
        outs = []
        for hh in heads:
            for c in range(n_keys // blk):
                p = jnp.exp2(s_ref[hh, c * blk:(c + 1) * blk, :] - m[hh])
                p_ref[hh, c * blk:(c + 1) * blk, :] = p.astype(jnp.bfloat16)
            vt_h = jnp.concatenate([vt[hh * HEAD_DIM:(hh + 1) * HEAD_DIM], ones], axis=0)
            acc = _dot(vt_h, p_ref[hh])
            denom = acc[HEAD_DIM:HEAD_DIM + 1] + jnp.exp2(sinks[hh] - m[hh])
            outs.append(acc[:HEAD_DIM] * (1.0 / denom))
        return outs

    def store(i, outs):
        o_ref[pl.ds(pl.multiple_of(i * TILE, TILE), TILE), :] = jnp.concatenate(outs, axis=0).T.astype(jnp.bfloat16)

    def step(i_next, s_next, i_cur, s_cur, m_cur):
        m_next, outs = [], []
        for hh in range(2):
            m_next += scores(i_next, s_next, (hh,))
            outs += update(i_cur, s_cur, m_cur, (hh,))
        store(i_cur, outs)
        return tuple(m_next)

    def body(t, m_a):
        for j in range(0, TILES_PER_TRIP, 2):
            i = TILES_PER_TRIP * t + j
            m_b = step(i + 1, sb_ref, i, sa_ref, m_a)
            m_a = step(i + 2, sa_ref, i + 1, sb_ref, m_b)
        return m_a

    m_a = lax.fori_loop(0, (nt - 1) // TILES_PER_TRIP, body, tuple(scores(0, sa_ref, (0, 1))))
    store(nt - 1, update(nt - 1, sa_ref, m_a, (0, 1)))


def _win_attn(qa, ka, vat, sink, ctx_len):
    n_batch, t_len, _ = qa.shape
    nt = t_len // TILE
    assert ctx_len == TILE and nt >= 4 and (nt - 1) % TILES_PER_TRIP == 0, "whole loop trips plus a last tile"
    return pl.pallas_call(
        functools.partial(_win_body, nt=nt),
        grid=(n_batch, A_HEADS // 2),
        in_specs=[pl.BlockSpec(memory_space=pltpu.SMEM),
                  pl.BlockSpec((None, t_len, LANES), lambda b, p: (b, 0, p)),
                  pl.BlockSpec((None, t_len, LANES), lambda b, p: (b, 0, 0)),
                  pl.BlockSpec((None,) + vat.shape[1:], lambda b, p: (b, 0, 0, 0))],
        out_specs=pl.BlockSpec((None, t_len, LANES), lambda b, p: (b, 0, p)),
        out_shape=jax.ShapeDtypeStruct((n_batch, t_len, A_HEADS * HEAD_DIM), jnp.bfloat16),
        scratch_shapes=[pltpu.VMEM((4, 2 * TILE, TILE), jnp.float32),
                        pltpu.VMEM((2, 3 * TILE, TILE), jnp.float32),
                        pltpu.VMEM((2, 3 * TILE, TILE), jnp.float32),
                        pltpu.VMEM((2, 3 * TILE, TILE), jnp.bfloat16)],
        compiler_params=_cparams(("parallel", "parallel")),
        name="win_attn",
    )(sink, qa, ka, vat)


def _mla_body(q_ref, k_ref, vt_ref, o_ref, sa_ref, sb_ref, sc_ref, p_ref, *, nt, n_chunks):
    sub = MLA_CHUNK // TILE
    blk = LANES
    ones_rows = 16
    n_items = 1 + n_chunks
    bufs = (sa_ref, sb_ref, sc_ref)
    assert n_items % len(bufs) == 0

    def tile_rows(t):
        return pl.ds(pl.multiple_of(t * TILE, TILE), TILE)

    def keys(n):
        return (0, 0, 1) if n == 0 else (TILE + (n - 1) * MLA_CHUNK, 1 + (n - 1) * sub, sub)

    def scores_h(qi, n, hh):
        start, _, n_tiles = keys(n)
        size = n_tiles * TILE
        cols = slice(hh * LANES, (hh + 1) * LANES)
        s = _dot_t(k_ref[start:start + size, cols], q_ref[tile_rows(qi), cols])
        bufs[n % len(bufs)][hh, :size, :] = s
        return jnp.max(s, axis=0, keepdims=True)

    def update_h(st, cm, n, hh):
        _, tile0, n_tiles = keys(n)
        size = n_tiles * TILE
        s_ref = bufs[n % len(bufs)]
        m_prev, acc = st
        m_new = jnp.maximum(m_prev, cm)
        alpha = jnp.exp2(m_prev - m_new)
        for c in range(size // blk):
            p = jnp.exp2(s_ref[hh, c * blk:(c + 1) * blk, :] - m_new)
            p_ref[hh, c * blk:(c + 1) * blk, :] = p.astype(jnp.bfloat16)
        vt = jnp.concatenate([vt_ref[tile0 + c, hh * B_V:(hh + 1) * B_V, :] for c in range(n_tiles)], axis=1)
        vt = jnp.concatenate([vt, jnp.ones((ones_rows, size), jnp.bfloat16)], axis=0)
        return m_new, acc * alpha + _dot(vt, p_ref[hh, :size, :])

    def scores(qi, n):
        return tuple(scores_h(qi, n, hh) for hh in range(2))

    def update(state, cm, n):
        return tuple(update_h(state[hh], cm[hh], n, hh) for hh in range(2))

    def finish(state, qi):
        o_t = jnp.concatenate([acc[:B_V] * (1.0 / acc[B_V:B_V + 1]) for _, acc in state], axis=0)
        o_ref[tile_rows(qi), :] = o_t.T.astype(jnp.bfloat16)

    init = tuple((jnp.full((1, TILE), NEG, jnp.float32), jnp.zeros((B_V + ones_rows, TILE), jnp.float32))
                 for _ in range(2))
    finish(update(init, scores(0, 0), 0), 0)

    def body(qi, cm):
        state = list(init)
        cm = list(cm)
        nxt = jnp.minimum(qi + 1, nt - 1)
        for n in range(n_items):
            for hh in range(2):
                cm_next = scores_h(qi, n + 1, hh) if n + 1 < n_items else scores_h(nxt, 0, hh)
                state[hh] = update_h(state[hh], cm[hh], n, hh)
                cm[hh] = cm_next
        finish(state, qi)
        return tuple(cm)

    lax.fori_loop(1, nt, body, scores(1, 0))


def _mla_attn(qb, kb, vbt, ctx_len):
    n_batch, t_len, _ = qb.shape
    nt = t_len // TILE
    seq = t_len - ctx_len
    assert ctx_len == TILE and seq % MLA_CHUNK == 0 and MLA_CHUNK % TILE == 0
    n_chunks = seq // MLA_CHUNK
    assert (1 + n_chunks) % 3 == 0, "context keys + latent chunks rotate over three score buffers"
    body = functools.partial(_mla_body, nt=nt, n_chunks=n_chunks)
    s_buf = pltpu.VMEM((2, MLA_CHUNK, TILE), jnp.float32)
    return pl.pallas_call(
        body,
        grid=(n_batch, B_HEADS // 2),
        in_specs=[pl.BlockSpec((None, t_len, 2 * LANES), lambda b, p: (b, 0, p)),
                  pl.BlockSpec((None, t_len, 2 * LANES), lambda b, p: (b, 0, p)),
                  pl.BlockSpec((None, nt, 2 * B_V, TILE), lambda b, p: (b, 0, p, 0))],
        out_specs=pl.BlockSpec((None, t_len, LANES), lambda b, p: (b, 0, p)),
        out_shape=jax.ShapeDtypeStruct((n_batch, t_len, B_HEADS * B_V), jnp.bfloat16),
        scratch_shapes=[s_buf, s_buf, s_buf, pltpu.VMEM((2, MLA_CHUNK, TILE), jnp.bfloat16)],
        compiler_params=_cparams(("parallel", "parallel")),
        name="mla_attn",
    )(qb, kb, vbt)


def _nbr_row_valid(case, jr, t):
    rpt = TILE // GRID_W
    lo = (rpt, jr + rpt - C_KH // 2, 2 * rpt - C_KH)[case]
    return lo <= t < lo + C_KH


def _nbr_fill_bias(tz_ref, bias_ref):
    rpt = TILE // GRID_W
    lane_lo = lax.broadcasted_iota(jnp.int32, (GRID_W, LANES), 1) < GRID_W
    neg = jnp.full((GRID_W, LANES), NEG, jnp.float32)
    for hh in range(2):
        bias_ref[3, hh] = jnp.full((3 * TILE, TILE), NEG, jnp.float32)
        for case in range(3):
            for t in range(3 * rpt):
                for jp in range(rpt // 2):
                    ok0 = _nbr_row_valid(case, 2 * jp, t)
                    ok1 = _nbr_row_valid(case, 2 * jp + 1, t)
                    if ok0 or ok1:
                        blk = tz_ref[hh, t - rpt - 2 * jp + C_KH - 2]
                        if not ok1:
                            blk = jnp.where(lane_lo, blk, NEG)
                        elif not ok0:
                            blk = jnp.where(lane_lo, NEG, blk)
                    else:
                        blk = neg
                    bias_ref[case, hh, t * GRID_W:(t + 1) * GRID_W, jp * LANES:(jp + 1) * LANES] = blk


def _nbr_body(q_ref, k_ref, vt_ref, tz_ref, o_ref, bias_ref, sa_ref, sb_ref, p_ref, *, nt):
    _nbr_fill_bias(tz_ref, bias_ref)
    n_keys = 4 * TILE
    ones = jnp.ones((16, n_keys), jnp.bfloat16)
    zero = jnp.zeros((), jnp.bfloat16)
    blk = LANES
    masks = _lane_half_masks(TILE)

    def key_tiles(i):
        return 0, jnp.clip(i - 1, 1, nt - 1), i, jnp.clip(i + 1, 1, nt - 1)

    def rows(t):
        return pl.ds(pl.multiple_of(t * TILE, TILE), TILE)

    def scores(i, s_ref, heads):
        case = jnp.where(i == 0, 3, jnp.where(i == 1, 0, jnp.where(i == nt - 1, 2, 1)))
        tc, t0, t1, t2 = key_tiles(i)
        k_ctx = k_ref[rows(tc), :]
        k_loc = jnp.concatenate([k_ref[rows(t0), :], k_ref[rows(t1), :], k_ref[rows(t2), :]], axis=0)
        q2 = q_ref[rows(i), :]
        m = []
        for hh in heads:
            qm = jnp.where(masks[hh], q2, zero)
            s_ctx = _dot_t(k_ctx, qm)
            s_ref[hh, :TILE, :] = s_ctx
            s_loc = _dot_t(k_loc, qm) + bias_ref[case, hh]
            s_ref[hh, TILE:, :] = s_loc
            m.append(jnp.maximum(jnp.max(s_ctx, axis=0, keepdims=True), jnp.max(s_loc, axis=0, keepdims=True)))
        return m

    def update(i, s_ref, m, heads):
        vt = jnp.concatenate([vt_ref[t] for t in key_tiles(i)], axis=1)
        outs = []
        for hh in heads:
            for c in range(n_keys // blk):
                p = jnp.exp2(s_ref[hh, c * blk:(c + 1) * blk, :] - m[hh])
                p_ref[hh, c * blk:(c + 1) * blk, :] = p.astype(jnp.bfloat16)
            vt_h = jnp.concatenate([vt[hh * HEAD_DIM:(hh + 1) * HEAD_DIM], ones], axis=0)
            acc = _dot(vt_h, p_ref[hh])
            outs.append(acc[:HEAD_DIM] * (1.0 / acc[HEAD_DIM:HEAD_DIM + 1]))
        return outs

    def store(i, outs):
        o_ref[rows(i), :] = jnp.concatenate(outs, axis=0).T.astype(jnp.bfloat16)

    def step(i_next, s_next, i_cur, s_cur, m_cur):
        m_next, outs = [], []
        for hh in range(2):
            m_next += scores(i_next, s_next, (hh,))
            outs += update(i_cur, s_cur, m_cur, (hh,))
        store(i_cur, outs)
        return tuple(m_next)

    def body(t, m_a):
        for j in range(0, TILES_PER_TRIP, 2):
            i = TILES_PER_TRIP * t + j
            m_b = step(i + 1, sb_ref, i, sa_ref, m_a)
            m_a = step(i + 2, sa_ref, i + 1, sb_ref, m_b)
        return m_a

    m_a = lax.fori_loop(0, (nt - 1) // TILES_PER_TRIP, body, tuple(scores(0, sa_ref, (0, 1))))
    store(nt - 1, update(nt - 1, sa_ref, m_a, (0, 1)))


def _nbr_attn(q, k, vt, tz, ctx_len):
    n_batch, t_len, d = q.shape
    nt = t_len // TILE
    assert ctx_len == TILE and TILE % GRID_W == 0 and C_KH == 2 * (TILE // GRID_W)
    assert nt >= 4, "needs distinct first / interior / last latent tiles"
    assert (nt - 1) % TILES_PER_TRIP == 0, "whole loop trips plus a last tile"
    seq = pl.BlockSpec((None, t_len, LANES), lambda b, p: (b, 0, p))
    return pl.pallas_call(
        functools.partial(_nbr_body, nt=nt),
        grid=(n_batch, C_HEADS // 2),
        in_specs=[seq, seq, pl.BlockSpec((None, nt, LANES, TILE), lambda b, p: (b, 0, p, 0)),
                  pl.BlockSpec((2,) + tz.shape[1:], lambda b, p: (p, 0, 0, 0))],
        out_specs=seq,
        out_shape=jax.ShapeDtypeStruct((n_batch, t_len, d), jnp.bfloat16),
        scratch_shapes=[pltpu.VMEM((4, 2, 3 * TILE, TILE), jnp.float32),
                        pltpu.VMEM((2, 4 * TILE, TILE), jnp.float32),
                        pltpu.VMEM((2, 4 * TILE, TILE), jnp.float32),
                        pltpu.VMEM((2, 4 * TILE, TILE), jnp.bfloat16)],
        compiler_params=_cparams(("parallel", "parallel")),
        name="nbr_attn",
    )(q, k, vt, tz)


def _nbr_toeplitz(rpb):
    n_off = 2 * C_KW - 1
    pad = GRID_W - C_KW
    ext = jnp.pad(rpb, ((0, 0), (0, 0), (pad, pad)), mode="edge")[:, :, ::-1]
    assert ext.shape[-1] == n_off + 2 * pad == 2 * GRID_W - 1
    n_h, n_r, length = ext.shape
    flat = jnp.tile(jnp.pad(ext, ((0, 0), (0, 0), (0, 1))), (1, 1, GRID_W))
    tz = flat[:, :, GRID_W - 1:GRID_W - 1 + GRID_W * length].reshape(n_h, n_r, GRID_W, length)[..., :GRID_W]
    qc = np.arange(GRID_W)[None, :]
    kc = np.arange(GRID_W)[:, None]
    cs = np.clip(qc - C_KW // 2, 0, GRID_W - C_KW)
    col_ok = jnp.asarray((kc >= cs) & (kc < cs + C_KW))
    tz = jnp.where(col_ok, tz * LOG2E, NEG)
    return jnp.concatenate([tz[:, 1:], tz[:, :-1]], axis=-1)


def _residual(h_ref, mod_ref, o1_ref, o2_ref, gate_ref, w_ref):
    half = D_MODEL // 2
    g = gate_ref[...].astype(jnp.float32)
    og1 = (o1_ref[...].astype(jnp.float32) * g[:, :half]).astype(jnp.bfloat16)
    og2 = (o2_ref[...].astype(jnp.float32) * g[:, half:]).astype(jnp.bfloat16)
    y = _dot(og1, w_ref[:half, :]) + _dot(og2, w_ref[half:, :])
    return h_ref[...] + mod_ref[:, 2 * D_MODEL:] * y


def _residual_then_proj_body(h_ref, mod_ref, o1_ref, o2_ref, gate_ref, w_ref, *rest, core):
    *proj_refs, h_out_ref = rest
    h_new = _residual(h_ref, mod_ref, o1_ref, o2_ref, gate_ref, w_ref)
    h_out_ref[...] = h_new
    core(h_new, *proj_refs)


def _prev_operands(prev, n_batch):
    mods_p, o1, o2, gate, w, o2_block = prev
    d = D_MODEL
    row = lambda wd, blk=0: pl.BlockSpec((None, TILE, wd), lambda b, i: (b, i, blk))
    specs = [_mod_spec(n_batch), row(d // 2), row(d // 2, o2_block), row(d), pl.BlockSpec(w.shape, lambda b, i: (0, 0))]
    return specs, [mods_p, o1, o2, gate, w]


def _out_final_body(h_ref, mod_ref, o1_ref, o2_ref, gate_ref, w_ref, fg_ref, out_ref):
    out_ref[...] = _rms(_residual(h_ref, mod_ref, o1_ref, o2_ref, gate_ref, w_ref), fg_ref[...])


def _out_proj_final(h, mods_l, o1, o2, gate, w, o2_block, final_g):
    n_batch, t_len, d = h.shape
    nt = t_len // TILE
    row = lambda wd, blk=0: pl.BlockSpec((None, TILE, wd), lambda b, i: (b, i + 1, blk))
    mod = pl.BlockSpec((None, 1, 3 * D_MODEL), lambda b, i: (b, 0, 0))
    full = lambda a: pl.BlockSpec(a.shape, lambda b, i: (0,) * a.ndim)
    return pl.pallas_call(
        _out_final_body,
        grid=(n_batch, nt - 1),
        in_specs=[row(d), mod, row(d // 2), row(d // 2, o2_block), row(d), full(w), full(final_g)],
        out_specs=pl.BlockSpec((None, TILE, d), lambda b, i: (b, i, 0)),
        out_shape=jax.ShapeDtypeStruct((n_batch, t_len - TILE, d), jnp.float32),
        compiler_params=_cparams(("parallel", "arbitrary")),
        name="out_proj_final",
    )(h, mods_l, o1, o2, gate, w, final_g)


def _deinterleave(n):
    return np.concatenate([np.arange(0, n, 2), np.arange(1, n, 2)])


def _take_cols(w, cols):
    cols = np.asarray(cols)
    picked = jnp.take(w, jnp.asarray(np.maximum(cols, 0)), axis=1)
    return jnp.where(jnp.asarray(cols >= 0)[None, :], picked, 0.0)


def _even_layout():
    perm = _deinterleave(HEAD_DIM)
    qa0, ka0 = 0, A_HEADS * HEAD_DIM
    va0 = ka0 + A_KV_HEADS * HEAD_DIM
    za0 = va0 + A_KV_HEADS * HEAD_DIM
    cq0 = za0 + A_HEADS * HEAD_DIM
    ckv0 = cq0 + B_Q_LORA
    kr0 = ckv0 + B_KV_LORA
    zb0 = kr0 + B_ROPE
    cols = []
    cols += [qa0 + hd * HEAD_DIM + perm for hd in _A_SLOTS]
    cols += [ka0 + kv * HEAD_DIM + perm for kv in range(A_KV_HEADS)]
    va_cols = va0 + np.arange(A_KV_HEADS * HEAD_DIM)
    cols += [za0 + hd * HEAD_DIM + np.arange(HEAD_DIM) for hd in _A_SLOTS]
    cols += [zb0 + np.arange(B_HEADS * B_V)]
    cols += [cq0 + np.arange(B_Q_LORA), ckv0 + np.arange(B_KV_LORA)]
    cols += [np.full(B_NOPE, -1), kr0 + _deinterleave(B_ROPE), np.full(LANES - B_NOPE - B_ROPE, -1)]
    in_cols = np.concatenate(cols)
    assert in_cols.shape[0] == _AB_COLS

    uq = []
    for hd in range(B_HEADS):
        base = hd * (B_NOPE + B_ROPE)
        uq += [base + np.arange(B_NOPE), base + B_NOPE + _deinterleave(B_ROPE),
               np.full(LANES - B_NOPE - B_ROPE, -1)]
    ukv_k, ukv_v = [], []
    pad = np.full(LANES - B_NOPE, -1)
    for hd in range(B_HEADS):
        base = hd * (B_NOPE + B_V)
        ukv_k += [base + np.arange(B_NOPE), pad]
        ukv_v += [base + B_NOPE + np.arange(B_V)]
    out_rows = np.concatenate([hd * HEAD_DIM + np.arange(HEAD_DIM) for hd in _A_SLOTS]
                              + [A_HEADS * HEAD_DIM + np.arange(B_HEADS * B_V)])
    return in_cols, va_cols, np.concatenate(uq), np.concatenate(ukv_k), np.concatenate(ukv_v), out_rows


def _rope_tables(seq_len, ctx_len):
    t = jnp.arange(seq_len, dtype=jnp.int32)
    row = (t // GRID_W).astype(jnp.float32)
    col = (t % GRID_W).astype(jnp.float32)

    def angles(rot_dim):
        ppa = rot_dim // 4
        inv = ROPE_THETA ** (-jnp.arange(ppa, dtype=jnp.float32) / ppa)
        ang = jnp.concatenate([row[:, None] * inv, col[:, None] * inv], axis=-1)
        return jnp.cos(ang), jnp.sin(ang)

    def with_ctx(c, s1, s2):
        pad = lambda a, v: jnp.concatenate([jnp.full((ctx_len, LANES), v, jnp.float32), a], axis=0)
        return pad(c, 1.0), pad(s1, 0.0), pad(s2, 0.0)

    cos_a, sin_a = angles(HEAD_DIM)
    z32 = jnp.zeros_like(sin_a)
    tabs_a = with_ctx(jnp.tile(cos_a, (1, 4)), jnp.tile(jnp.concatenate([-sin_a, z32], 1), (1, 2)),
                      jnp.tile(jnp.concatenate([z32, sin_a], 1), (1, 2)))
    cos_b, sin_b = angles(B_ROPE)
    one64 = jnp.ones((seq_len, B_NOPE), jnp.float32)
    z64 = jnp.zeros((seq_len, B_NOPE), jnp.float32)
    z16 = jnp.zeros_like(sin_b)
    one32 = jnp.ones((seq_len, LANES - B_NOPE - B_ROPE), jnp.float32)
    z32b = jnp.zeros((seq_len, LANES - B_NOPE - B_ROPE), jnp.float32)
    tabs_b = with_ctx(jnp.concatenate([one64, cos_b, cos_b, one32], 1),
                      jnp.concatenate([z64, -sin_b, z16, z32b], 1),
                      jnp.concatenate([z64, z16, sin_b, z32b], 1))
    return tabs_a, tabs_b


def kernel(x, c, ctx, c_ctx, ada_w, ada_b, norm_g, ab_in_w, ab_out_w, a_sink, b_q_norm_g, b_w_uq,
           b_kv_norm_g, b_w_ukv, c_in_w, c_out_w, c_rpb, final_g):
    n_batch, seq_len, d = x.shape
    ctx_len = ctx.shape[1]
    depth = ada_w.shape[0]
    assert d == D_MODEL and ctx_len == TILE and seq_len % TILE == 0
    bf = jnp.bfloat16

    mod_rows = -(-(n_batch + 1) // 8) * 8
    cc = jnp.concatenate([c, c_ctx[None, :], jnp.zeros((mod_rows - n_batch - 1, d), jnp.float32)], axis=0)
    mods = _mods(cc, ada_w, ada_b).reshape(depth, mod_rows, 1, 3 * d)

    in_cols, va_cols, uq_cols, uk_cols, uv_cols, out_rows = _even_layout()
    tabs_a, tabs_b = _rope_tables(seq_len, ctx_len)
    h = (ctx, x)
    row2 = lambda v: v.reshape(1, -1)

    prev = None
    for layer in range(depth):
        i = layer // 2
        g = row2(norm_g[layer])
        if layer % 2 == 0:
            in_w, ukv_w = ab_in_w[i].astype(bf), b_w_ukv[i].astype(bf)
            w_in = _take_cols(in_w, in_cols)
            wuq = _take_cols(b_w_uq[i].astype(bf), uq_cols)
            wuk = _take_cols(ukv_w, uk_cols)
            wvt = _take_cols(ukv_w, uv_cols).T
            w_out = jnp.take(ab_out_w[i].astype(bf), jnp.asarray(out_rows), axis=0)
            sink = jnp.take(a_sink[i], jnp.asarray(_A_SLOTS))
            wvat = _take_cols(in_w, va_cols).T
            outs = _proj_even(h, mods[layer], g, w_in, wvat, row2(b_q_norm_g[i]), wuq,
                              row2(b_kv_norm_g[i]), wuk, wvt, tabs_a, tabs_b, prev=prev)
            if layer == 0 or prev:
                h = outs[-1]
            qa, ka, vat, gate, qb, kb, vbt = outs[:7]
            o1 = _win_attn(qa, ka, vat, sink, ctx_len)
            o2 = _mla_attn(qb, kb, vbt, ctx_len)
            o2_block = 0
        else:
            w = c_in_w[i].astype(bf)
            w_qkz = jnp.concatenate([w[:, :2 * d], w[:, 3 * d:]], axis=1)
            outs = _proj_odd(h, mods[layer], g, w_qkz, w[:, 2 * d:3 * d].T, prev=prev)
            if prev:
                h = outs[-1]
            q, k, vt, gate = outs[:4]
            o1 = o2 = _nbr_attn(q, k, vt, _nbr_toeplitz(c_rpb[i]), ctx_len)
            o2_block = 1
            w_out = c_out_w[i].astype(bf)
        prev = (mods[layer], o1, o2, gate, w_out, o2_block)
    return _out_proj_final(h, *prev, row2(final_g))
```

```python
import functools
import math

import numpy as np
import jax
import jax.numpy as jnp
from jax import lax
from jax.experimental import pallas as pl
from jax.experimental.pallas import tpu as pltpu

D_MODEL = 1024
HEAD_DIM = 64
GRID_W = 64
ROPE_THETA = 10000.0
NORM_EPS = 1e-6
A_HEADS = 8
A_KV_HEADS = 2
A_WINDOW = 128
B_HEADS = 8
B_Q_LORA = 384
B_KV_LORA = 256
B_NOPE = 64
B_ROPE = 32
B_V = 64
C_HEADS = 16
C_KH = 8
C_KW = 16

LANES = 128
TILE = 256
MLA_CHUNK = 1024
TILES_PER_TRIP = 4
LOG2E = math.log2(math.e)
NEG = -1e30
VMEM_LIMIT = 48 * 1024 * 1024

_QA, _KA, _ZA, _ZB, _CQ, _CKV, _KR, _AB_COLS = 0, 512, 640, 1152, 1664, 2048, 2304, 2432
_A_SLOTS = (0, 4, 1, 5, 2, 6, 3, 7)


def _cparams(sem):
    return pltpu.CompilerParams(dimension_semantics=sem, vmem_limit_bytes=VMEM_LIMIT)


def _silu(x):
    return x * (1.0 / (1.0 + jnp.exp(-x)))


def _dot(a, b):
    return jnp.dot(a, b, preferred_element_type=jnp.float32)


def _dot_t(a, b):
    return lax.dot_general(a, b, (((1,), (1,)), ((), ())), preferred_element_type=jnp.float32)


def _mods_body(c_ref, w_ref, b_ref, o_ref):
    sc = _silu(c_ref[...])
    o_ref[...] = jnp.dot(sc, w_ref[...], preferred_element_type=jnp.float32,
                         precision=lax.Precision.HIGHEST) + b_ref[...]


def _mods(cc, ada_w, ada_b):
    depth, d, d3 = ada_w.shape
    rows = cc.shape[0]
    tn = 512
    return pl.pallas_call(
        _mods_body,
        grid=(depth, d3 // tn),
        in_specs=[
            pl.BlockSpec((rows, d), lambda l, j: (0, 0)),
            pl.BlockSpec((None, d, tn), lambda l, j: (l, 0, j)),
            pl.BlockSpec((None, 1, tn), lambda l, j: (l, 0, j)),
        ],
        out_specs=pl.BlockSpec((None, rows, tn), lambda l, j: (l, 0, j)),
        out_shape=jax.ShapeDtypeStruct((depth, rows, d3), jnp.float32),
        compiler_params=_cparams(("arbitrary", "arbitrary")),
        name="mods",
    )(cc, ada_w, ada_b.reshape(depth, 1, d3))


def _mod_spec(n_batch):
    return pl.BlockSpec((None, 1, 3 * D_MODEL), lambda b, i: (jnp.where(i == 0, n_batch, b), 0, 0))


def _norm_mod(h, g, mod):
    y = h * lax.rsqrt(jnp.mean(h * h, axis=-1, keepdims=True) + NORM_EPS) * g
    return y * (1.0 + mod[:, D_MODEL:2 * D_MODEL]) + mod[:, :D_MODEL]


def _rms(x, g):
    return x * lax.rsqrt(jnp.mean(x * x, axis=-1, keepdims=True) + NORM_EPS) * g


def _rope(x, c, s1, s2, half):
    return x * c + pltpu.roll(x, LANES - half, 1) * s1 + pltpu.roll(x, half, 1) * s2


def _proj_even_body(h_ref, *rest):
    _proj_even_core(h_ref[...], *rest)


def _proj_first_body(ctx_ref, x_ref, *rest):
    *rest, h_out_ref = rest
    h = jnp.where(pl.program_id(1) == 0, ctx_ref[...], x_ref[...])
    h_out_ref[...] = h
    _proj_even_core(h, *rest)


def _proj_even_core(h, mod_ref, g_ref, w_ref, wvat_ref, qg_ref, wuq_ref, kvg_ref, wukv_ref, wvt_ref,
                    ca_ref, sa1_ref, sa2_ref, cb_ref, sb1_ref, sb2_ref,
                    qa_ref, ka_ref, vat_ref, gate_ref, qb_ref, kb_ref, vbt_ref):
    u = _norm_mod(h, g_ref[...], mod_ref[...]).astype(jnp.bfloat16)
    ca, sa1, sa2 = ca_ref[...], sa1_ref[...], sa2_ref[...]
    cb, sb1, sb2 = cb_ref[...], sb1_ref[...], sb2_ref[...]

    qk = _dot(u, w_ref[:, _QA:_ZA])
    a_scale = HEAD_DIM ** -0.5 * LOG2E
    for j in range(4):
        blk = _rope(qk[:, j * LANES:(j + 1) * LANES], ca, sa1, sa2, HEAD_DIM // 2)
        qa_ref[:, j * LANES:(j + 1) * LANES] = (blk * a_scale).astype(jnp.bfloat16)
    ka_ref[...] = _rope(qk[:, _KA:_ZA], ca, sa1, sa2, HEAD_DIM // 2).astype(jnp.bfloat16)
    vat = _dot_t(wvat_ref[...], u).astype(jnp.bfloat16)
    for j in range(TILE // LANES):
        vat_ref[j] = vat[:, j * LANES:(j + 1) * LANES]
    gate_ref[...] = _silu(_dot(u, w_ref[:, _ZA:_CQ])).astype(jnp.bfloat16)

    lat = _dot(u, w_ref[:, _CQ:_AB_COLS])
    cq = _rms(lat[:, :B_Q_LORA], qg_ref[...]).astype(jnp.bfloat16)
    ckv = _rms(lat[:, B_Q_LORA:B_Q_LORA + B_KV_LORA], kvg_ref[...]).astype(jnp.bfloat16)
    kr = _rope(lat[:, B_Q_LORA + B_KV_LORA:], cb, sb1, sb2, B_ROPE // 2)

    qb = _dot(cq, wuq_ref[...])
    b_scale = (B_NOPE + B_ROPE) ** -0.5 * LOG2E
    kb = _dot(ckv, wukv_ref[...])
    for hd in range(B_HEADS):
        sl = slice(hd * LANES, (hd + 1) * LANES)
        qb_ref[:, sl] = (_rope(qb[:, sl], cb, sb1, sb2, B_ROPE // 2) * b_scale).astype(jnp.bfloat16)
        kb_ref[:, sl] = (kb[:, sl] + kr).astype(jnp.bfloat16)
    vbt_ref[...] = _dot_t(wvt_ref[...], ckv).astype(jnp.bfloat16)


def _proj_even(h, mods_l, g, w_in, wvat, qg, wuq, kvg, wukv, wvt, tabs_a, tabs_b, prev=None):
    first = isinstance(h, tuple)
    if first:
        ctx, x = h
        n_batch, seq_len, d = x.shape
        t_len = ctx.shape[1] + seq_len
    else:
        n_batch, t_len, d = h.shape
    nt = t_len // TILE
    row = lambda w: pl.BlockSpec((None, TILE, w), lambda b, i: (b, i, 0))
    full = lambda a: pl.BlockSpec(a.shape, lambda b, i: (0,) * a.ndim)
    tab = pl.BlockSpec((TILE, LANES), lambda b, i: (i, 0))
    bf = jnp.bfloat16
    sds = lambda *shape: jax.ShapeDtypeStruct((n_batch,) + shape, bf)
    n_vt = B_HEADS * B_V
    per_tile = TILE // LANES
    if first:
        h_specs = [pl.BlockSpec((None, TILE, d), lambda b, i: (b, 0, 0)),
                   pl.BlockSpec((None, TILE, d), lambda b, i: (b, jnp.maximum(i - 1, 0), 0))]
        h_args = [ctx, x]
    else:
        prev_specs, prev_args = _prev_operands(prev, n_batch) if prev else ([], [])
        h_specs, h_args = [row(d)] + prev_specs, [h] + prev_args
    out_specs = [row(512), row(LANES),
                 pl.BlockSpec((None, per_tile, LANES, LANES), lambda b, i: (b, i, 0, 0)),
                 row(d), row(d), row(d),
                 pl.BlockSpec((None, None, n_vt, TILE), lambda b, i: (b, i, 0, 0))]
    out_shape = [sds(t_len, 512), sds(t_len, LANES), sds(t_len // LANES, LANES, LANES),
                 sds(t_len, d), sds(t_len, d), sds(t_len, d), sds(nt, n_vt, TILE)]
    if first or prev:
        out_specs.append(row(d))
        out_shape.append(jax.ShapeDtypeStruct((n_batch, t_len, d), jnp.float32))
    return pl.pallas_call(
        _proj_first_body if first else
        functools.partial(_residual_then_proj_body, core=_proj_even_core) if prev else _proj_even_body,
        grid=(n_batch, nt),
        in_specs=h_specs + [_mod_spec(n_batch), full(g), full(w_in), full(wvat), full(qg), full(wuq), full(kvg),
                            full(wukv), full(wvt)] + [tab] * 6,
        out_specs=out_specs,
        out_shape=out_shape,
        input_output_aliases={0: len(out_specs) - 1} if prev else {},
        compiler_params=_cparams(("parallel", "arbitrary")),
        name="proj_even",
    )(*h_args, mods_l, g, w_in, wvat, qg, wuq, kvg, wukv, wvt, *tabs_a, *tabs_b)


def _proj_odd_body(h_ref, *rest):
    _proj_odd_core(h_ref[...], *rest)


def _proj_odd_core(h, mod_ref, g_ref, w_ref, wvt_ref, q_ref, k_ref, vt_ref, gate_ref):
    u = _norm_mod(h, g_ref[...], mod_ref[...]).astype(jnp.bfloat16)
    d = D_MODEL
    q_ref[...] = (_dot(u, w_ref[:, :d]) * (HEAD_DIM ** -0.5 * LOG2E)).astype(jnp.bfloat16)
    k_ref[...] = _dot(u, w_ref[:, d:2 * d]).astype(jnp.bfloat16)
    vt_ref[...] = _dot_t(wvt_ref[...], u).astype(jnp.bfloat16)
    gate_ref[...] = _silu(_dot(u, w_ref[:, 2 * d:])).astype(jnp.bfloat16)


def _proj_odd(h, mods_l, g, w_qkz, wvt, prev=None):
    n_batch, t_len, d = h.shape
    nt = t_len // TILE
    row = pl.BlockSpec((None, TILE, d), lambda b, i: (b, i, 0))
    row_t = pl.BlockSpec((None, None, d, TILE), lambda b, i: (b, i, 0, 0))
    full = lambda a: pl.BlockSpec(a.shape, lambda b, i: (0,) * a.ndim)
    out = jax.ShapeDtypeStruct((n_batch, t_len, d), jnp.bfloat16)
    prev_specs, prev_args = _prev_operands(prev, n_batch) if prev else ([], [])
    out_specs = [row, row, row_t, row]
    out_shape = [out, out, jax.ShapeDtypeStruct((n_batch, nt, d, TILE), jnp.bfloat16), out]
    if prev:
        out_specs.append(row)
        out_shape.append(jax.ShapeDtypeStruct(h.shape, jnp.float32))
    return pl.pallas_call(
        functools.partial(_residual_then_proj_body, core=_proj_odd_core) if prev else _proj_odd_body,
        grid=(n_batch, nt),
        in_specs=[row] + prev_specs + [_mod_spec(n_batch), full(g), full(w_qkz), full(wvt)],
        out_specs=out_specs,
        out_shape=out_shape,
        input_output_aliases={0: len(out_specs) - 1} if prev else {},
        compiler_params=_cparams(("parallel", "arbitrary")),
        name="proj_odd",
    )(h, *prev_args, mods_l, g, w_qkz, wvt)


def _lane_half_masks(rows):
    lane = lax.broadcasted_iota(jnp.int32, (rows, LANES), 1)
    lo = lane < HEAD_DIM
    return lo, jnp.logical_not(lo)


def _win_fill_mask(mask_ref):
    r = lax.broadcasted_iota(jnp.int32, (2 * TILE, TILE), 0)
    c = lax.broadcasted_iota(jnp.int32, (2 * TILE, TILE), 1)
    for case, (back, first) in enumerate(((A_WINDOW, A_WINDOW), (A_WINDOW, 0), (2 * A_WINDOW, 0))):
        ok = (jnp.abs(r - back - c) <= A_WINDOW) & (r >= first)
        mask_ref[case] = jnp.where(ok, 0.0, NEG)
    mask_ref[3] = jnp.full((2 * TILE, TILE), NEG, jnp.float32)


def _win_body(sink_ref, q_ref, k_ref, vt_ref, o_ref, mask_ref, sa_ref, sb_ref, p_ref, *, nt):
    assert TILE == 2 * A_WINDOW
    pair = pl.program_id(1)
    _win_fill_mask(mask_ref)
    n_keys = 3 * TILE
    n128 = nt * (TILE // LANES)
    ones = jnp.ones((16, n_keys), jnp.bfloat16)
    zero = jnp.zeros((), jnp.bfloat16)
    blk = LANES
    masks = _lane_half_masks(TILE)
    sinks = tuple(sink_ref[2 * pair + hh] * LOG2E for hh in range(2))

    def window(i):
        return jnp.clip(2 * i - 1, 0, n128 - 4)

    def scores(i, s_ref, heads):
        case = jnp.where(i == 0, 3, jnp.where(i == 1, 0, jnp.where(i == nt - 1, 2, 1)))
        k_ctx = k_ref[:TILE, :]
        k_loc = k_ref[pl.ds(pl.multiple_of(window(i) * LANES, LANES), 2 * TILE), :]
        q2 = q_ref[pl.ds(pl.multiple_of(i * TILE, TILE), TILE), :]
        m = []
        for hh in heads:
            qm = jnp.where(masks[hh], q2, zero)
            s_ctx = _dot_t(k_ctx, qm)
            s_ref[hh, :TILE, :] = s_ctx
            s_loc = _dot_t(k_loc, qm) + mask_ref[case]
            s_ref[hh, TILE:, :] = s_loc
            m.append(jnp.maximum(jnp.maximum(jnp.max(s_ctx, axis=0, keepdims=True),
                                             jnp.max(s_loc, axis=0, keepdims=True)), sinks[hh]))
        return m

    def update(i, s_ref, m, heads):
        w0 = window(i)
        vt = jnp.concatenate([vt_ref[0], vt_ref[1]] + [vt_ref[w0 + j] for j in range(4)], axis=1)
        outs = []
        for hh in heads:
            for c in range(n_keys // blk):
                p = jnp.exp2(s_ref[hh, c * blk:(c + 1) * blk, :] - m[hh])
                p_ref[hh, c * blk:(c + 1) * blk, :] = p.astype(jnp.bfloat16)
            vt_h = jnp.concatenate([vt[hh * HEAD_DIM:(hh + 1) * HEAD_DIM], ones], axis=0)
            acc = _dot(vt_h, p_ref[hh])
            denom = acc[HEAD_DIM:HEAD_DIM + 1] + jnp.exp2(sinks[hh] - m[hh])
            outs.append(acc[:HEAD_DIM] * (1.0 / denom))
        return outs

    def store(i, outs):
        o_ref[pl.ds(pl.multiple_of(i * TILE, TILE), TILE), :] = jnp.concatenate(outs, axis=0).T.astype(jnp.bfloat16)

    def step(i_next, s_next, i_cur, s_cur, m_cur):
        m_next, outs = [], []
        for hh in range(2):
            m_next += scores(i_next, s_next, (hh,))
            outs += update(i_cur, s_cur, m_cur, (hh,))
        store(i_cur, outs)
        return tuple(m_next)

    def body(t, m_a):
        for j in range(0, TILES_PER_TRIP, 2):
            i = TILES_PER_TRIP * t + j
            m_b = step(i + 1, sb_ref, i, sa_ref, m_a)
            m_a = step(i + 2, sa_ref, i + 1, sb_ref, m_b)
        return m_a

    m_a = lax.fori_loop(0, (nt - 1) // TILES_PER_TRIP, body, tuple(scores(0, sa_ref, (0, 1))))
    store(nt - 1, update(nt - 1, sa_ref, m_a, (0, 1)))


def _win_attn(qa, ka, vat, sink, ctx_len):
    n_batch, t_len, _ = qa.shape
    nt = t_len // TILE
    assert ctx_len == TILE and nt >= 4 and (nt - 1) % TILES_PER_TRIP == 0, "whole loop trips plus a last tile"
    return pl.pallas_call(
        functools.partial(_win_body, nt=nt),
        grid=(n_batch, A_HEADS // 2),
        in_specs=[pl.BlockSpec(memory_space=pltpu.SMEM),
                  pl.BlockSpec((None, t_len, LANES), lambda b, p: (b, 0, p)),
                  pl.BlockSpec((None, t_len, LANES), lambda b, p: (b, 0, 0)),
                  pl.BlockSpec((None,) + vat.shape[1:], lambda b, p: (b, 0, 0, 0))],
        out_specs=pl.BlockSpec((None, t_len, LANES), lambda b, p: (b, 0, p)),
        out_shape=jax.ShapeDtypeStruct((n_batch, t_len, A_HEADS * HEAD_DIM), jnp.bfloat16),
        scratch_shapes=[pltpu.VMEM((4, 2 * TILE, TILE), jnp.float32),
                        pltpu.VMEM((2, 3 * TILE, TILE), jnp.float32),
                        pltpu.VMEM((2, 3 * TILE, TILE), jnp.float32),
                        pltpu.VMEM((2, 3 * TILE, TILE), jnp.bfloat16)],
        compiler_params=_cparams(("parallel", "parallel")),
        name="win_attn",
    )(sink, qa, ka, vat)


def _mla_body(q_ref, k_ref, vt_ref, o_ref, sa_ref, sb_ref, p_ref, *, nt, n_chunks):
    sub = MLA_CHUNK // TILE
    blk = LANES
    ones_rows = 16
    n_items = n_chunks
    bufs = (sa_ref, sb_ref)
    assert n_items % len(bufs) == 0

    def buf(n):
        return bufs[0 if n is None else n % len(bufs)]

    def tile_rows(t):
        return pl.ds(pl.multiple_of(t * TILE, TILE), TILE)

    def keys(n):
        if n is None:
            return 0, 0, 1
        return (0, 0, 1 + sub) if n == 0 else (TILE + n * MLA_CHUNK, 1 + n * sub, sub)

    def scores_h(qi, n, hh):
        start, _, n_tiles = keys(n)
        size = n_tiles * TILE
        cols = slice(hh * LANES, (hh + 1) * LANES)
        s = _dot_t(k_ref[start:start + size, cols], q_ref[tile_rows(qi), cols])
        buf(n)[hh, :size, :] = s
        return jnp.max(s, axis=0, keepdims=True)

    def update_h(st, cm, n, hh):
        _, tile0, n_tiles = keys(n)
        size = n_tiles * TILE
        s_ref = buf(n)
        m_prev, acc = st
        m_new = jnp.maximum(m_prev, cm)
        alpha = jnp.exp2(m_prev - m_new)
        for c in range(size // blk):
            p = jnp.exp2(s_ref[hh, c * blk:(c + 1) * blk, :] - m_new)
            p_ref[hh, c * blk:(c + 1) * blk, :] = p.astype(jnp.bfloat16)
        vt = jnp.concatenate([vt_ref[tile0 + c, hh * B_V:(hh + 1) * B_V, :] for c in range(n_tiles)], axis=1)
        vt = jnp.concatenate([vt, jnp.ones((ones_rows, size), jnp.bfloat16)], axis=0)
        return m_new, acc * alpha + _dot(vt, p_ref[hh, :size, :])

    def scores(qi, n):
        return tuple(scores_h(qi, n, hh) for hh in range(2))

    def update(state, cm, n):
        return tuple(update_h(state[hh], cm[hh], n, hh) for hh in range(2))

    def finish(state, qi):
        o_t = jnp.concatenate([acc[:B_V] * (1.0 / acc[B_V:B_V + 1]) for _, acc in state], axis=0)
        o_ref[tile_rows(qi), :] = o_t.T.astype(jnp.bfloat16)

    init = tuple((jnp.full((1, TILE), NEG, jnp.float32), jnp.zeros((B_V + ones_rows, TILE), jnp.float32))
                 for _ in range(2))
    finish(update(init, scores(0, None), None), 0)

    def body(qi, cm):
        state = list(init)
        cm = list(cm)
        nxt = jnp.minimum(qi + 1, nt - 1)
        for n in range(n_items):
            for hh in range(2):
                cm_next = scores_h(qi, n + 1, hh) if n + 1 < n_items else scores_h(nxt, 0, hh)
                state[hh] = update_h(state[hh], cm[hh], n, hh)
                cm[hh] = cm_next
        finish(state, qi)
        return tuple(cm)

    lax.fori_loop(1, nt, body, scores(1, 0))


def _mla_attn(qb, kb, vbt, ctx_len):
    n_batch, t_len, _ = qb.shape
    nt = t_len // TILE
    seq = t_len - ctx_len
    assert ctx_len == TILE and seq % MLA_CHUNK == 0 and MLA_CHUNK % TILE == 0
    n_chunks = seq // MLA_CHUNK
    assert n_chunks % 2 == 0, "the work items of a query tile alternate between two score buffers"
    body = functools.partial(_mla_body, nt=nt, n_chunks=n_chunks)
    item_keys = ctx_len + MLA_CHUNK
    s_buf = pltpu.VMEM((2, item_keys, TILE), jnp.float32)
    return pl.pallas_call(
        body,
        grid=(n_batch, B_HEADS // 2),
        in_specs=[pl.BlockSpec((None, t_len, 2 * LANES), lambda b, p: (b, 0, p)),
                  pl.BlockSpec((None, t_len, 2 * LANES), lambda b, p: (b, 0, p)),
                  pl.BlockSpec((None, nt, 2 * B_V, TILE), lambda b, p: (b, 0, p, 0))],
        out_specs=pl.BlockSpec((None, t_len, LANES), lambda b, p: (b, 0, p)),
        out_shape=jax.ShapeDtypeStruct((n_batch, t_len, B_HEADS * B_V), jnp.bfloat16),
        scratch_shapes=[s_buf, s_buf, pltpu.VMEM((2, item_keys, TILE), jnp.bfloat16)],
        compiler_params=_cparams(("parallel", "parallel")),
        name="mla_attn",
    )(qb, kb, vbt)


def _nbr_row_valid(case, jr, t):
    rpt = TILE // GRID_W
    lo = (rpt, jr + rpt - C_KH // 2, 2 * rpt - C_KH)[case]
    return lo <= t < lo + C_KH


def _nbr_fill_bias(tz_ref, bias_ref):
    rpt = TILE // GRID_W
    lane_lo = lax.broadcasted_iota(jnp.int32, (GRID_W, LANES), 1) < GRID_W
    neg = jnp.full((GRID_W, LANES), NEG, jnp.float32)
    for hh in range(2):
        bias_ref[3, hh] = jnp.full((3 * TILE, TILE), NEG, jnp.float32)
        for case in range(3):
            for t in range(3 * rpt):
                for jp in range(rpt // 2):
                    ok0 = _nbr_row_valid(case, 2 * jp, t)
                    ok1 = _nbr_row_valid(case, 2 * jp + 1, t)
                    if ok0 or ok1:
                        blk = tz_ref[hh, t - rpt - 2 * jp + C_KH - 2]
                        if not ok1:
                            blk = jnp.where(lane_lo, blk, NEG)
                        elif not ok0:
                            blk = jnp.where(lane_lo, NEG, blk)
                    else:
                        blk = neg
                    bias_ref[case, hh, t * GRID_W:(t + 1) * GRID_W, jp * LANES:(jp + 1) * LANES] = blk


def _nbr_body(q_ref, k_ref, vt_ref, tz_ref, o_ref, bias_ref, sa_ref, sb_ref, p_ref, *, nt):
    _nbr_fill_bias(tz_ref, bias_ref)
    n_keys = 4 * TILE
    ones = jnp.ones((16, n_keys), jnp.bfloat16)
    zero = jnp.zeros((), jnp.bfloat16)
    blk = LANES
    masks = _lane_half_masks(TILE)

    def key_tiles(i):
        return 0, jnp.clip(i - 1, 1, nt - 1), i, jnp.clip(i + 1, 1, nt - 1)

    def rows(t):
        return pl.ds(pl.multiple_of(t * TILE, TILE), TILE)

    def scores(i, s_ref, heads):
        case = jnp.where(i == 0, 3, jnp.where(i == 1, 0, jnp.where(i == nt - 1, 2, 1)))
        tc, t0, t1, t2 = key_tiles(i)
        k_ctx = k_ref[rows(tc), :]
        k_loc = jnp.concatenate([k_ref[rows(t0), :], k_ref[rows(t1), :], k_ref[rows(t2), :]], axis=0)
        q2 = q_ref[rows(i), :]
        m = []
        for hh in heads:
            qm = jnp.where(masks[hh], q2, zero)
            s_ctx = _dot_t(k_ctx, qm)
            s_ref[hh, :TILE, :] = s_ctx
            s_loc = _dot_t(k_loc, qm) + bias_ref[case, hh]
            s_ref[hh, TILE:, :] = s_loc
            m.append(jnp.maximum(jnp.max(s_ctx, axis=0, keepdims=True), jnp.max(s_loc, axis=0, keepdims=True)))
        return m

    def update(i, s_ref, m, heads):
        vt = jnp.concatenate([vt_ref[t] for t in key_tiles(i)], axis=1)
        outs = []
        for hh in heads:
            for c in range(n_keys // blk):
                p = jnp.exp2(s_ref[hh, c * blk:(c + 1) * blk, :] - m[hh])
                p_ref[hh, c * blk:(c + 1) * blk, :] = p.astype(jnp.bfloat16)
            vt_h = jnp.concatenate([vt[hh * HEAD_DIM:(hh + 1) * HEAD_DIM], ones], axis=0)
            acc = _dot(vt_h, p_ref[hh])
            outs.append(acc[:HEAD_DIM] * (1.0 / acc[HEAD_DIM:HEAD_DIM + 1]))
        return outs

    def store(i, outs):
        o_ref[rows(i), :] = jnp.concatenate(outs, axis=0).T.astype(jnp.bfloat16)

    def step(i_next, s_next, i_cur, s_cur, m_cur):
        m_next, outs = [], []
        for hh in range(2):
            m_next += scores(i_next, s_next, (hh,))
            outs += update(i_cur, s_cur, m_cur, (hh,))
        store(i_cur, outs)
        return tuple(m_next)

    def body(t, m_a):
        for j in range(0, TILES_PER_TRIP, 2):
            i = TILES_PER_TRIP * t + j
            m_b = step(i + 1, sb_ref, i, sa_ref, m_a)
            m_a = step(i + 2, sa_ref, i + 1, sb_ref, m_b)
        return m_a

    m_a = lax.fori_loop(0, (nt - 1) // TILES_PER_TRIP, body, tuple(scores(0, sa_ref, (0, 1))))
    store(nt - 1, update(nt - 1, sa_ref, m_a, (0, 1)))


def _nbr_attn(q, k, vt, tz, ctx_len):
    n_batch, t_len, d = q.shape
    nt = t_len // TILE
    assert ctx_len == TILE and TILE % GRID_W == 0 and C_KH == 2 * (TILE // GRID_W)
    assert nt >= 4, "needs distinct first / interior / last latent tiles"
    assert (nt - 1) % TILES_PER_TRIP == 0, "whole loop trips plus a last tile"
    seq = pl.BlockSpec((None, t_len, LANES), lambda b, p: (b, 0, p))
    return pl.pallas_call(
        functools.partial(_nbr_body, nt=nt),
        grid=(n_batch, C_HEADS // 2),
        in_specs=[seq, seq, pl.BlockSpec((None, nt, LANES, TILE), lambda b, p: (b, 0, p, 0)),
                  pl.BlockSpec((2,) + tz.shape[1:], lambda b, p: (p, 0, 0, 0))],
        out_specs=seq,
        out_shape=jax.ShapeDtypeStruct((n_batch, t_len, d), jnp.bfloat16),
        scratch_shapes=[pltpu.VMEM((4, 2, 3 * TILE, TILE), jnp.float32),
                        pltpu.VMEM((2, 4 * TILE, TILE), jnp.float32),
                        pltpu.VMEM((2, 4 * TILE, TILE), jnp.float32),
                        pltpu.VMEM((2, 4 * TILE, TILE), jnp.bfloat16)],
        compiler_params=_cparams(("parallel", "parallel")),
        name="nbr_attn",
    )(q, k, vt, tz)


def _nbr_toeplitz(rpb):
    n_off = 2 * C_KW - 1
    pad = GRID_W - C_KW
    ext = jnp.pad(rpb, ((0, 0), (0, 0), (pad, pad)), mode="edge")[:, :, ::-1]
    assert ext.shape[-1] == n_off + 2 * pad == 2 * GRID_W - 1
    n_h, n_r, length = ext.shape
    flat = jnp.tile(jnp.pad(ext, ((0, 0), (0, 0), (0, 1))), (1, 1, GRID_W))
    tz = flat[:, :, GRID_W - 1:GRID_W - 1 + GRID_W * length].reshape(n_h, n_r, GRID_W, length)[..., :GRID_W]
    qc = np.arange(GRID_W)[None, :]
    kc = np.arange(GRID_W)[:, None]
    cs = np.clip(qc - C_KW // 2, 0, GRID_W - C_KW)
    col_ok = jnp.asarray((kc >= cs) & (kc < cs + C_KW))
    tz = jnp.where(col_ok, tz * LOG2E, NEG)
    return jnp.concatenate([tz[:, 1:], tz[:, :-1]], axis=-1)


def _residual(h_ref, mod_ref, o1_ref, o2_ref, gate_ref, w_ref):
    half = D_MODEL // 2
    g = gate_ref[...].astype(jnp.float32)
    og1 = (o1_ref[...].astype(jnp.float32) * g[:, :half]).astype(jnp.bfloat16)
    og2 = (o2_ref[...].astype(jnp.float32) * g[:, half:]).astype(jnp.bfloat16)
    y = _dot(og1, w_ref[:half, :]) + _dot(og2, w_ref[half:, :])
    return h_ref[...] + mod_ref[:, 2 * D_MODEL:] * y


def _residual_then_proj_body(h_ref, mod_ref, o1_ref, o2_ref, gate_ref, w_ref, *rest, core):
    *proj_refs, h_out_ref = rest
    h_new = _residual(h_ref, mod_ref, o1_ref, o2_ref, gate_ref, w_ref)
    h_out_ref[...] = h_new
    core(h_new, *proj_refs)


def _prev_operands(prev, n_batch):
    mods_p, o1, o2, gate, w, o2_block = prev
    d = D_MODEL
    row = lambda wd, blk=0: pl.BlockSpec((None, TILE, wd), lambda b, i: (b, i, blk))
    specs = [_mod_spec(n_batch), row(d // 2), row(d // 2, o2_block), row(d), pl.BlockSpec(w.shape, lambda b, i: (0, 0))]
    return specs, [mods_p, o1, o2, gate, w]


def _out_final_body(h_ref, mod_ref, o1_ref, o2_ref, gate_ref, w_ref, fg_ref, out_ref):
    out_ref[...] = _rms(_residual(h_ref, mod_ref, o1_ref, o2_ref, gate_ref, w_ref), fg_ref[...])


def _out_proj_final(h, mods_l, o1, o2, gate, w, o2_block, final_g):
    n_batch, t_len, d = h.shape
    nt = t_len // TILE
    row = lambda wd, blk=0: pl.BlockSpec((None, TILE, wd), lambda b, i: (b, i + 1, blk))
    mod = pl.BlockSpec((None, 1, 3 * D_MODEL), lambda b, i: (b, 0, 0))
    full = lambda a: pl.BlockSpec(a.shape, lambda b, i: (0,) * a.ndim)
    return pl.pallas_call(
        _out_final_body,
        grid=(n_batch, nt - 1),
        in_specs=[row(d), mod, row(d // 2), row(d // 2, o2_block), row(d), full(w), full(final_g)],
        out_specs=pl.BlockSpec((None, TILE, d), lambda b, i: (b, i, 0)),
        out_shape=jax.ShapeDtypeStruct((n_batch, t_len - TILE, d), jnp.float32),
        compiler_params=_cparams(("parallel", "arbitrary")),
        name="out_proj_final",
    )(h, mods_l, o1, o2, gate, w, final_g)


def _deinterleave(n):
    return np.concatenate([np.arange(0, n, 2), np.arange(1, n, 2)])


def _take_cols(w, cols):
    cols = np.asarray(cols)
    picked = jnp.take(w, jnp.asarray(np.maximum(cols, 0)), axis=1)
    return jnp.where(jnp.asarray(cols >= 0)[None, :], picked, 0.0)


def _even_layout():
    perm = _deinterleave(HEAD_DIM)
    qa0, ka0 = 0, A_HEADS * HEAD_DIM
    va0 = ka0 + A_KV_HEADS * HEAD_DIM
    za0 = va0 + A_KV_HEADS * HEAD_DIM
    cq0 = za0 + A_HEADS * HEAD_DIM
    ckv0 = cq0 + B_Q_LORA
    kr0 = ckv0 + B_KV_LORA
    zb0 = kr0 + B_ROPE
    cols = []
    cols += [qa0 + hd * HEAD_DIM + perm for hd in _A_SLOTS]
    cols += [ka0 + kv * HEAD_DIM + perm for kv in range(A_KV_HEADS)]
    va_cols = va0 + np.arange(A_KV_HEADS * HEAD_DIM)
    cols += [za0 + hd * HEAD_DIM + np.arange(HEAD_DIM) for hd in _A_SLOTS]
    cols += [zb0 + np.arange(B_HEADS * B_V)]
    cols += [cq0 + np.arange(B_Q_LORA), ckv0 + np.arange(B_KV_LORA)]
    cols += [np.full(B_NOPE, -1), kr0 + _deinterleave(B_ROPE), np.full(LANES - B_NOPE - B_ROPE, -1)]
    in_cols = np.concatenate(cols)
    assert in_cols.shape[0] == _AB_COLS

    uq = []
    for hd in range(B_HEADS):
        base = hd * (B_NOPE + B_ROPE)
        uq += [base + np.arange(B_NOPE), base + B_NOPE + _deinterleave(B_ROPE),
               np.full(LANES - B_NOPE - B_ROPE, -1)]
    ukv_k, ukv_v = [], []
    pad = np.full(LANES - B_NOPE, -1)
    for hd in range(B_HEADS):
        base = hd * (B_NOPE + B_V)
        ukv_k += [base + np.arange(B_NOPE), pad]
        ukv_v += [base + B_NOPE + np.arange(B_V)]
    out_rows = np.concatenate([hd * HEAD_DIM + np.arange(HEAD_DIM) for hd in _A_SLOTS]
                              + [A_HEADS * HEAD_DIM + np.arange(B_HEADS * B_V)])
    return in_cols, va_cols, np.concatenate(uq), np.concatenate(ukv_k), np.concatenate(ukv_v), out_rows


def _rope_tables(seq_len, ctx_len):
    t = jnp.arange(seq_len, dtype=jnp.int32)
    row = (t // GRID_W).astype(jnp.float32)
    col = (t % GRID_W).astype(jnp.float32)

    def angles(rot_dim):
        ppa = rot_dim // 4
        inv = ROPE_THETA ** (-jnp.arange(ppa, dtype=jnp.float32) / ppa)
        ang = jnp.concatenate([row[:, None] * inv, col[:, None] * inv], axis=-1)
        return jnp.cos(ang), jnp.sin(ang)

    def with_ctx(c, s1, s2):
        pad = lambda a, v: jnp.concatenate([jnp.full((ctx_len, LANES), v, jnp.float32), a], axis=0)
        return pad(c, 1.0), pad(s1, 0.0), pad(s2, 0.0)

    cos_a, sin_a = angles(HEAD_DIM)
    z32 = jnp.zeros_like(sin_a)
    tabs_a = with_ctx(jnp.tile(cos_a, (1, 4)), jnp.tile(jnp.concatenate([-sin_a, z32], 1), (1, 2)),
                      jnp.tile(jnp.concatenate([z32, sin_a], 1), (1, 2)))
    cos_b, sin_b = angles(B_ROPE)
    one64 = jnp.ones((seq_len, B_NOPE), jnp.float32)
    z64 = jnp.zeros((seq_len, B_NOPE), jnp.float32)
    z16 = jnp.zeros_like(sin_b)
    one32 = jnp.ones((seq_len, LANES - B_NOPE - B_ROPE), jnp.float32)
    z32b = jnp.zeros((seq_len, LANES - B_NOPE - B_ROPE), jnp.float32)
    tabs_b = with_ctx(jnp.concatenate([one64, cos_b, cos_b, one32], 1),
                      jnp.concatenate([z64, -sin_b, z16, z32b], 1),
                      jnp.concatenate([z64, z16, sin_b, z32b], 1))
    return tabs_a, tabs_b


def kernel(x, c, ctx, c_ctx, ada_w, ada_b, norm_g, ab_in_w, ab_out_w, a_sink, b_q_norm_g, b_w_uq,
           b_kv_norm_g, b_w_ukv, c_in_w, c_out_w, c_rpb, final_g):
    n_batch, seq_len, d = x.shape
    ctx_len = ctx.shape[1]
    depth = ada_w.shape[0]
    assert d == D_MODEL and ctx_len == TILE and seq_len % TILE == 0
    bf = jnp.bfloat16

    mod_rows = -(-(n_batch + 1) // 8) * 8
    cc = jnp.concatenate([c, c_ctx[None, :], jnp.zeros((mod_rows - n_batch - 1, d), jnp.float32)], axis=0)
    mods = _mods(cc, ada_w, ada_b).reshape(depth, mod_rows, 1, 3 * d)

    in_cols, va_cols, uq_cols, uk_cols, uv_cols, out_rows = _even_layout()
    tabs_a, tabs_b = _rope_tables(seq_len, ctx_len)
    h = (ctx, x)
    row2 = lambda v: v.reshape(1, -1)

    prev = None
    for layer in range(depth):
        i = layer // 2
        g = row2(norm_g[layer])
        if layer % 2 == 0:
            in_w, ukv_w = ab_in_w[i].astype(bf), b_w_ukv[i].astype(bf)
            w_in = _take_cols(in_w, in_cols)
            wuq = _take_cols(b_w_uq[i].astype(bf), uq_cols)
            wuk = _take_cols(ukv_w, uk_cols)
            wvt = _take_cols(ukv_w, uv_cols).T
            w_out = jnp.take(ab_out_w[i].astype(bf), jnp.asarray(out_rows), axis=0)
            sink = jnp.take(a_sink[i], jnp.asarray(_A_SLOTS))
            wvat = _take_cols(in_w, va_cols).T
            outs = _proj_even(h, mods[layer], g, w_in, wvat, row2(b_q_norm_g[i]), wuq,
                              row2(b_kv_norm_g[i]), wuk, wvt, tabs_a, tabs_b, prev=prev)
            if layer == 0 or prev:
                h = outs[-1]
            qa, ka, vat, gate, qb, kb, vbt = outs[:7]
            o1 = _win_attn(qa, ka, vat, sink, ctx_len)
            o2 = _mla_attn(qb, kb, vbt, ctx_len)
            o2_block = 0
        else:
            w = c_in_w[i].astype(bf)
            w_qkz = jnp.concatenate([w[:, :2 * d], w[:, 3 * d:]], axis=1)
            outs = _proj_odd(h, mods[layer], g, w_qkz, w[:, 2 * d:3 * d].T, prev=prev)
            if prev:
                h = outs[-1]
            q, k, vt, gate = outs[:4]
            o1 = o2 = _nbr_attn(q, k, vt, _nbr_toeplitz(c_rpb[i]), ctx_len)
            o2_block = 1
            w_out = c_out_w[i].astype(bf)
        prev = (mods[layer], o1, o2, gate, w_out, o2_block)
    return _out_proj_final(h, *prev, row2(final_g))
```

```python
import functools
import math

import numpy as np
import jax
import jax.numpy as jnp
from jax import lax
from jax.experimental import pallas as pl
from jax.experimental.pallas import tpu as pltpu

D_MODEL = 1024
HEAD_DIM = 64
GRID_W = 64
ROPE_THETA = 10000.0
NORM_EPS = 1e-6
A_HEADS = 8
A_KV_HEADS = 2
A_WINDOW = 128
B_HEADS = 8
B_Q_LORA = 384
B_KV_LORA = 256
B_NOPE = 64
B_ROPE = 32
B_V = 64
C_HEADS = 16
C_KH = 8
C_KW = 16

LANES = 128
TILE = 256
MLA_CHUNK = 1024
TILES_PER_TRIP = 8
LOG2E = math.log2(math.e)
NEG = -1e30
VMEM_LIMIT = 48 * 1024 * 1024

_QA, _KA, _ZA, _ZB, _CQ, _CKV, _KR, _AB_COLS = 0, 512, 640, 1152, 1664, 2048, 2304, 2432
_A_SLOTS = (0, 4, 1, 5, 2, 6, 3, 7)


def _cparams(sem):
    return pltpu.CompilerParams(dimension_semantics=sem, vmem_limit_bytes=VMEM_LIMIT)


def _silu(x):
    return x * (1.0 / (1.0 + jnp.exp(-x)))


def _dot(a, b):
    return jnp.dot(a, b, preferred_element_type=jnp.float32)


def _dot_t(a, b):
    return lax.dot_general(a, b, (((1,), (1,)), ((), ())), preferred_element_type=jnp.float32)


def _mods_body(c_ref, w_ref, b_ref, o_ref):
    sc = _silu(c_ref[...])
    o_ref[...] = jnp.dot(sc, w_ref[...], preferred_element_type=jnp.float32,
                         precision=lax.Precision.HIGHEST) + b_ref[...]


def _mods(cc, ada_w, ada_b):
    depth, d, d3 = ada_w.shape
    rows = cc.shape[0]
    tn = 512
    return pl.pallas_call(
        _mods_body,
        grid=(depth, d3 // tn),
        in_specs=[
            pl.BlockSpec((rows, d), lambda l, j: (0, 0)),
            pl.BlockSpec((None, d, tn), lambda l, j: (l, 0, j)),
            pl.BlockSpec((None, 1, tn), lambda l, j: (l, 0, j)),
        ],
        out_specs=pl.BlockSpec((None, rows, tn), lambda l, j: (l, 0, j)),
        out_shape=jax.ShapeDtypeStruct((depth, rows, d3), jnp.float32),
        compiler_params=_cparams(("arbitrary", "arbitrary")),
        name="mods",
    )(cc, ada_w, ada_b.reshape(depth, 1, d3))


def _mod_spec(n_batch):
    return pl.BlockSpec((None, 1, 3 * D_MODEL), lambda b, i: (jnp.where(i == 0, n_batch, b), 0, 0))


def _norm_mod(h, g, mod):
    y = h * lax.rsqrt(jnp.mean(h * h, axis=-1, keepdims=True) + NORM_EPS) * g
    return y * (1.0 + mod[:, D_MODEL:2 * D_MODEL]) + mod[:, :D_MODEL]


def _rms(x, g):
    return x * lax.rsqrt(jnp.mean(x * x, axis=-1, keepdims=True) + NORM_EPS) * g


def _rope(x, c, s1, s2, half):
    return x * c + pltpu.roll(x, LANES - half, 1) * s1 + pltpu.roll(x, half, 1) * s2


def _proj_even_body(h_ref, *rest):
    _proj_even_core(h_ref[...], *rest)


def _proj_first_body(ctx_ref, x_ref, *rest):
    *rest, h_out_ref = rest
    h = jnp.where(pl.program_id(1) == 0, ctx_ref[...], x_ref[...])
    h_out_ref[...] = h
    _proj_even_core(h, *rest)


def _proj_even_core(h, mod_ref, g_ref, w_ref, wvat_ref, qg_ref, wuq_ref, kvg_ref, wukv_ref, wvt_ref,
                    ca_ref, sa1_ref, sa2_ref, cb_ref, sb1_ref, sb2_ref,
                    qa_ref, ka_ref, vat_ref, gate_ref, qb_ref, kb_ref, vbt_ref):
    u = _norm_mod(h, g_ref[...], mod_ref[...]).astype(jnp.bfloat16)
    ca, sa1, sa2 = ca_ref[...], sa1_ref[...], sa2_ref[...]
    cb, sb1, sb2 = cb_ref[...], sb1_ref[...], sb2_ref[...]

    qk = _dot(u, w_ref[:, _QA:_ZA])
    a_scale = HEAD_DIM ** -0.5 * LOG2E
    for j in range(4):
        blk = _rope(qk[:, j * LANES:(j + 1) * LANES], ca, sa1, sa2, HEAD_DIM // 2)
        qa_ref[:, j * LANES:(j + 1) * LANES] = (blk * a_scale).astype(jnp.bfloat16)
    ka_ref[...] = _rope(qk[:, _KA:_ZA], ca, sa1, sa2, HEAD_DIM // 2).astype(jnp.bfloat16)
    vat = _dot_t(wvat_ref[...], u).astype(jnp.bfloat16)
    for j in range(TILE // LANES):
        vat_ref[j] = vat[:, j * LANES:(j + 1) * LANES]
    gate_ref[...] = _silu(_dot(u, w_ref[:, _ZA:_CQ])).astype(jnp.bfloat16)

    lat = _dot(u, w_ref[:, _CQ:_AB_COLS])
    cq = _rms(lat[:, :B_Q_LORA], qg_ref[...]).astype(jnp.bfloat16)
    ckv = _rms(lat[:, B_Q_LORA:B_Q_LORA + B_KV_LORA], kvg_ref[...]).astype(jnp.bfloat16)
    kr = _rope(lat[:, B_Q_LORA + B_KV_LORA:], cb, sb1, sb2, B_ROPE // 2)

    qb = _dot(cq, wuq_ref[...])
    b_scale = (B_NOPE + B_ROPE) ** -0.5 * LOG2E
    kb = _dot(ckv, wukv_ref[...])
    for hd in range(B_HEADS):
        sl = slice(hd * LANES, (hd + 1) * LANES)
        qb_ref[:, sl] = (_rope(qb[:, sl], cb, sb1, sb2, B_ROPE // 2) * b_scale).astype(jnp.bfloat16)
        kb_ref[:, sl] = (kb[:, sl] + kr).astype(jnp.bfloat16)
    vbt_ref[...] = _dot_t(wvt_ref[...], ckv).astype(jnp.bfloat16)


def _proj_even(h, mods_l, g, w_in, wvat, qg, wuq, kvg, wukv, wvt, tabs_a, tabs_b, prev=None):
    first = isinstance(h, tuple)
    if first:
        ctx, x = h
        n_batch, seq_len, d = x.shape
        t_len = ctx.shape[1] + seq_len
    else:
        n_batch, t_len, d = h.shape
    nt = t_len // TILE
    row = lambda w: pl.BlockSpec((None, TILE, w), lambda b, i: (b, i, 0))
    full = lambda a: pl.BlockSpec(a.shape, lambda b, i: (0,) * a.ndim)
    tab = pl.BlockSpec((TILE, LANES), lambda b, i: (i, 0))
    bf = jnp.bfloat16
    sds = lambda *shape: jax.ShapeDtypeStruct((n_batch,) + shape, bf)
    n_vt = B_HEADS * B_V
    per_tile = TILE // LANES
    if first:
        h_specs = [pl.BlockSpec((None, TILE, d), lambda b, i: (b, 0, 0)),
                   pl.BlockSpec((None, TILE, d), lambda b, i: (b, jnp.maximum(i - 1, 0), 0))]
        h_args = [ctx, x]
    else:
        prev_specs, prev_args = _prev_operands(prev, n_batch) if prev else ([], [])
        h_specs, h_args = [row(d)] + prev_specs, [h] + prev_args
    out_specs = [row(512), row(LANES),
                 pl.BlockSpec((None, per_tile, LANES, LANES), lambda b, i: (b, i, 0, 0)),
                 row(d), row(d), row(d),
                 pl.BlockSpec((None, None, n_vt, TILE), lambda b, i: (b, i, 0, 0))]
    out_shape = [sds(t_len, 512), sds(t_len, LANES), sds(t_len // LANES, LANES, LANES),
                 sds(t_len, d), sds(t_len, d), sds(t_len, d), sds(nt, n_vt, TILE)]
    if first or prev:
        out_specs.append(row(d))
        out_shape.append(jax.ShapeDtypeStruct((n_batch, t_len, d), jnp.float32))
    return pl.pallas_call(
        _proj_first_body if first else
        functools.partial(_residual_then_proj_body, core=_proj_even_core) if prev else _proj_even_body,
        grid=(n_batch, nt),
        in_specs=h_specs + [_mod_spec(n_batch), full(g), full(w_in), full(wvat), full(qg), full(wuq), full(kvg),
                            full(wukv), full(wvt)] + [tab] * 6,
        out_specs=out_specs,
        out_shape=out_shape,
        input_output_aliases={0: len(out_specs) - 1} if prev else {},
        compiler_params=_cparams(("parallel", "arbitrary")),
        name="proj_even",
    )(*h_args, mods_l, g, w_in, wvat, qg, wuq, kvg, wukv, wvt, *tabs_a, *tabs_b)


def _proj_odd_body(h_ref, *rest):
    _proj_odd_core(h_ref[...], *rest)


def _proj_odd_core(h, mod_ref, g_ref, w_ref, wvt_ref, q_ref, k_ref, vt_ref, gate_ref):
    u = _norm_mod(h, g_ref[...], mod_ref[...]).astype(jnp.bfloat16)
    d = D_MODEL
    q_ref[...] = (_dot(u, w_ref[:, :d]) * (HEAD_DIM ** -0.5 * LOG2E)).astype(jnp.bfloat16)
    k_ref[...] = _dot(u, w_ref[:, d:2 * d]).astype(jnp.bfloat16)
    vt_ref[...] = _dot_t(wvt_ref[...], u).astype(jnp.bfloat16)
    gate_ref[...] = _silu(_dot(u, w_ref[:, 2 * d:])).astype(jnp.bfloat16)


def _proj_odd(h, mods_l, g, w_qkz, wvt, prev=None):
    n_batch, t_len, d = h.shape
    nt = t_len // TILE
    row = pl.BlockSpec((None, TILE, d), lambda b, i: (b, i, 0))
    row_t = pl.BlockSpec((None, None, d, TILE), lambda b, i: (b, i, 0, 0))
    full = lambda a: pl.BlockSpec(a.shape, lambda b, i: (0,) * a.ndim)
    out = jax.ShapeDtypeStruct((n_batch, t_len, d), jnp.bfloat16)
    prev_specs, prev_args = _prev_operands(prev, n_batch) if prev else ([], [])
    out_specs = [row, row, row_t, row]
    out_shape = [out, out, jax.ShapeDtypeStruct((n_batch, nt, d, TILE), jnp.bfloat16), out]
    if prev:
        out_specs.append(row)
        out_shape.append(jax.ShapeDtypeStruct(h.shape, jnp.float32))
    return pl.pallas_call(
        functools.partial(_residual_then_proj_body, core=_proj_odd_core) if prev else _proj_odd_body,
        grid=(n_batch, nt),
        in_specs=[row] + prev_specs + [_mod_spec(n_batch), full(g), full(w_qkz), full(wvt)],
        out_specs=out_specs,
        out_shape=out_shape,
        input_output_aliases={0: len(out_specs) - 1} if prev else {},
        compiler_params=_cparams(("parallel", "arbitrary")),
        name="proj_odd",
    )(h, *prev_args, mods_l, g, w_qkz, wvt)


def _lane_half_masks(rows):
    lane = lax.broadcasted_iota(jnp.int32, (rows, LANES), 1)
    lo = lane < HEAD_DIM
    return lo, jnp.logical_not(lo)


def _win_fill_mask(mask_ref):
    r = lax.broadcasted_iota(jnp.int32, (2 * TILE, TILE), 0)
    c = lax.broadcasted_iota(jnp.int32, (2 * TILE, TILE), 1)
    for case, (back, first) in enumerate(((A_WINDOW, A_WINDOW), (A_WINDOW, 0), (2 * A_WINDOW, 0))):
        ok = (jnp.abs(r - back - c) <= A_WINDOW) & (r >= first)
        mask_ref[case] = jnp.where(ok, 0.0, NEG)
    mask_ref[3] = jnp.full((2 * TILE, TILE), NEG, jnp.float32)


def _win_body(sink_ref, q_ref, k_ref, vt_ref, o_ref, mask_ref, sa_ref, sb_ref, p_ref, *, nt):
    assert TILE == 2 * A_WINDOW
    pair = pl.program_id(1)
    _win_fill_mask(mask_ref)
    n_keys = 3 * TILE
    n128 = nt * (TILE // LANES)
    ones = jnp.ones((16, n_keys), jnp.bfloat16)
    zero = jnp.zeros((), jnp.bfloat16)
    blk = LANES
    masks = _lane_half_masks(TILE)
    sinks = tuple(sink_ref[2 * pair + hh] * LOG2E for hh in range(2))

    def window(i):
        return jnp.clip(2 * i - 1, 0, n128 - 4)

    def scores(i, s_ref, heads):
        case = jnp.where(i == 0, 3, jnp.where(i == 1, 0, jnp.where(i == nt - 1, 2, 1)))
        k_ctx = k_ref[:TILE, :]
        k_loc = k_ref[pl.ds(pl.multiple_of(window(i) * LANES, LANES), 2 * TILE), :]
        q2 = q_ref[pl.ds(pl.multiple_of(i * TILE, TILE), TILE), :]
        m = []
        for hh in heads:
            qm = jnp.where(masks[hh], q2, zero)
            s_ctx = _dot_t(k_ctx, qm)
            s_ref[hh, :TILE, :] = s_ctx
            s_loc = _dot_t(k_loc, qm) + mask_ref[case]
            s_ref[hh, TILE:, :] = s_loc
            m.append(jnp.maximum(jnp.maximum(jnp.max(s_ctx, axis=0, keepdims=True),
                                             jnp.max(s_loc, axis=0, keepdims=True)), sinks[hh]))
        return m

    def update(i, s_ref, m, heads):
        w0 = window(i)
        vt = jnp.concatenate([vt_ref[0], vt_ref[1]] + [vt_ref[w0 + j] for j in range(4)], axis=1)
        outs = []
        for hh in heads:
            for c in range(n_keys // blk):
                p = jnp.exp2(s_ref[hh, c * blk:(c + 1) * blk, :] - m[hh])
                p_ref[hh, c * blk:(c + 1) * blk, :] = p.astype(jnp.bfloat16)
            vt_h = jnp.concatenate([vt[hh * HEAD_DIM:(hh + 1) * HEAD_DIM], ones], axis=0)
            acc = _dot(vt_h, p_ref[hh])
            denom = acc[HEAD_DIM:HEAD_DIM + 1] + jnp.exp2(sinks[hh] - m[hh])
            outs.append(acc[:HEAD_DIM] * (1.0 / denom))
        return outs

    def store(i, outs):
        o_ref[pl.ds(pl.multiple_of(i * TILE, TILE), TILE), :] = jnp.concatenate(outs, axis=0).T.astype(jnp.bfloat16)

    def step(i_next, s_next, i_cur, s_cur, m_cur):
        m_next, outs = [], []
        for hh in range(2):
            m_next += scores(i_next, s_next, (hh,))
            outs += update(i_cur, s_cur, m_cur, (hh,))
        store(i_cur, outs)
        return tuple(m_next)

    def body(t, m_a):
        for j in range(0, TILES_PER_TRIP, 2):
            i = TILES_PER_TRIP * t + j
            m_b = step(i + 1, sb_ref, i, sa_ref, m_a)
            m_a = step(i + 2, sa_ref, i + 1, sb_ref, m_b)
        return m_a

    m_a = lax.fori_loop(0, (nt - 1) // TILES_PER_TRIP, body, tuple(scores(0, sa_ref, (0, 1))))
    store(nt - 1, update(nt - 1, sa_ref, m_a, (0, 1)))


def _win_attn(qa, ka, vat, sink, ctx_len):
    n_batch, t_len, _ = qa.shape
    nt = t_len // TILE
    assert ctx_len == TILE and nt >= 4 and (nt - 1) % TILES_PER_TRIP == 0, "whole loop trips plus a last tile"
    return pl.pallas_call(
        functools.partial(_win_body, nt=nt),
        grid=(n_batch, A_HEADS // 2),
        in_specs=[pl.BlockSpec(memory_space=pltpu.SMEM),
                  pl.BlockSpec((None, t_len, LANES), lambda b, p: (b, 0, p)),
                  pl.BlockSpec((None, t_len, LANES), lambda b, p: (b, 0, 0)),
                  pl.BlockSpec((None,) + vat.shape[1:], lambda b, p: (b, 0, 0, 0))],
        out_specs=pl.BlockSpec((None, t_len, LANES), lambda b, p: (b, 0, p)),
        out_shape=jax.ShapeDtypeStruct((n_batch, t_len, A_HEADS * HEAD_DIM), jnp.bfloat16),
        scratch_shapes=[pltpu.VMEM((4, 2 * TILE, TILE), jnp.float32),
                        pltpu.VMEM((2, 3 * TILE, TILE), jnp.float32),
                        pltpu.VMEM((2, 3 * TILE, TILE), jnp.float32),
                        pltpu.VMEM((2, 3 * TILE, TILE), jnp.bfloat16)],
        compiler_params=_cparams(("parallel", "parallel")),
        name="win_attn",
    )(sink, qa, ka, vat)


def _mla_body(q_ref, k_ref, vt_ref, o_ref, sa_ref, sb_ref, p_ref, *, nt, n_chunks):
    sub = MLA_CHUNK // TILE
    blk = LANES
    ones_rows = 16
    n_items = n_chunks
    bufs = (sa_ref, sb_ref)
    assert n_items % len(bufs) == 0

    def buf(n):
        return bufs[0 if n is None else n % len(bufs)]

    def tile_rows(t):
        return pl.ds(pl.multiple_of(t * TILE, TILE), TILE)

    def keys(n):
        if n is None:
            return 0, 0, 1
        return (0, 0, 1 + sub) if n == 0 else (TILE + n * MLA_CHUNK, 1 + n * sub, sub)

    def scores_h(qi, n, hh):
        start, _, n_tiles = keys(n)
        size = n_tiles * TILE
        cols = slice(hh * LANES, (hh + 1) * LANES)
        s = _dot_t(k_ref[start:start + size, cols], q_ref[tile_rows(qi), cols])
        buf(n)[hh, :size, :] = s
        return jnp.max(s, axis=0, keepdims=True)

    def update_h(st, cm, n, hh):
        _, tile0, n_tiles = keys(n)
        size = n_tiles * TILE
        s_ref = buf(n)
        m_prev, acc = st
        m_new = jnp.maximum(m_prev, cm)
        alpha = jnp.exp2(m_prev - m_new)
        for c in range(size // blk):
            p = jnp.exp2(s_ref[hh, c * blk:(c + 1) * blk, :] - m_new)
            p_ref[hh, c * blk:(c + 1) * blk, :] = p.astype(jnp.bfloat16)
        vt = jnp.concatenate([vt_ref[tile0 + c, hh * B_V:(hh + 1) * B_V, :] for c in range(n_tiles)], axis=1)
        vt = jnp.concatenate([vt, jnp.ones((ones_rows, size), jnp.bfloat16)], axis=0)
        return m_new, acc * alpha + _dot(vt, p_ref[hh, :size, :])

    def scores(qi, n):
        return tuple(scores_h(qi, n, hh) for hh in range(2))

    def update(state, cm, n):
        return tuple(update_h(state[hh], cm[hh], n, hh) for hh in range(2))

    def finish(state, qi):
        o_t = jnp.concatenate([acc[:B_V] * (1.0 / acc[B_V:B_V + 1]) for _, acc in state], axis=0)
        o_ref[tile_rows(qi), :] = o_t.T.astype(jnp.bfloat16)

    init = tuple((jnp.full((1, TILE), NEG, jnp.float32), jnp.zeros((B_V + ones_rows, TILE), jnp.float32))
                 for _ in range(2))
    finish(update(init, scores(0, None), None), 0)

    def body(qi, cm):
        state = list(init)
        cm = list(cm)
        nxt = jnp.minimum(qi + 1, nt - 1)
        for n in range(n_items):
            for hh in range(2):
                cm_next = scores_h(qi, n + 1, hh) if n + 1 < n_items else scores_h(nxt, 0, hh)
                state[hh] = update_h(state[hh], cm[hh], n, hh)
                cm[hh] = cm_next
        finish(state, qi)
        return tuple(cm)

    lax.fori_loop(1, nt, body, scores(1, 0))


def _mla_attn(qb, kb, vbt, ctx_len):
    n_batch, t_len, _ = qb.shape
    nt = t_len // TILE
    seq = t_len - ctx_len
    assert ctx_len == TILE and seq % MLA_CHUNK == 0 and MLA_CHUNK % TILE == 0
    n_chunks = seq // MLA_CHUNK
    assert n_chunks % 2 == 0, "the work items of a query tile alternate between two score buffers"
    body = functools.partial(_mla_body, nt=nt, n_chunks=n_chunks)
    item_keys = ctx_len + MLA_CHUNK
    s_buf = pltpu.VMEM((2, item_keys, TILE), jnp.float32)
    return pl.pallas_call(
        body,
        grid=(n_batch, B_HEADS // 2),
        in_specs=[pl.BlockSpec((None, t_len, 2 * LANES), lambda b, p: (b, 0, p)),
                  pl.BlockSpec((None, t_len, 2 * LANES), lambda b, p: (b, 0, p)),
                  pl.BlockSpec((None, nt, 2 * B_V, TILE), lambda b, p: (b, 0, p, 0))],
        out_specs=pl.BlockSpec((None, t_len, LANES), lambda b, p: (b, 0, p)),
        out_shape=jax.ShapeDtypeStruct((n_batch, t_len, B_HEADS * B_V), jnp.bfloat16),
        scratch_shapes=[s_buf, s_buf, pltpu.VMEM((2, item_keys, TILE), jnp.bfloat16)],
        compiler_params=_cparams(("parallel", "parallel")),
        name="mla_attn",
    )(qb, kb, vbt)


def _nbr_row_valid(case, jr, t):
    rpt = TILE // GRID_W
    lo = (rpt, jr + rpt - C_KH // 2, 2 * rpt - C_KH)[case]
    return lo <= t < lo + C_KH


def _nbr_fill_bias(tz_ref, bias_ref):
    rpt = TILE // GRID_W
    lane_lo = lax.broadcasted_iota(jnp.int32, (GRID_W, LANES), 1) < GRID_W
    neg = jnp.full((GRID_W, LANES), NEG, jnp.float32)
    for hh in range(2):
        bias_ref[3, hh] = jnp.full((3 * TILE, TILE), NEG, jnp.float32)
        for case in range(3):
            for t in range(3 * rpt):
                for jp in range(rpt // 2):
                    ok0 = _nbr_row_valid(case, 2 * jp, t)
                    ok1 = _nbr_row_valid(case, 2 * jp + 1, t)
                    if ok0 or ok1:
                        blk = tz_ref[hh, t - rpt - 2 * jp + C_KH - 2]
                        if not ok1:
                            blk = jnp.where(lane_lo, blk, NEG)
                        elif not ok0:
                            blk = jnp.where(lane_lo, NEG, blk)
                    else:
                        blk = neg
                    bias_ref[case, hh, t * GRID_W:(t + 1) * GRID_W, jp * LANES:(jp + 1) * LANES] = blk


def _nbr_body(q_ref, k_ref, vt_ref, tz_ref, o_ref, bias_ref, sa_ref, sb_ref, p_ref, *, nt):
    _nbr_fill_bias(tz_ref, bias_ref)
    n_keys = 4 * TILE
    ones = jnp.ones((16, n_keys), jnp.bfloat16)
    zero = jnp.zeros((), jnp.bfloat16)
    blk = LANES
    masks = _lane_half_masks(TILE)

    def key_tiles(i):
        return 0, jnp.clip(i - 1, 1, nt - 1), i, jnp.clip(i + 1, 1, nt - 1)

    def rows(t):
        return pl.ds(pl.multiple_of(t * TILE, TILE), TILE)

    def scores(i, s_ref, heads):
        case = jnp.where(i == 0, 3, jnp.where(i == 1, 0, jnp.where(i == nt - 1, 2, 1)))
        tc, t0, t1, t2 = key_tiles(i)
        k_ctx = k_ref[rows(tc), :]
        k_loc = jnp.concatenate([k_ref[rows(t0), :], k_ref[rows(t1), :], k_ref[rows(t2), :]], axis=0)
        q2 = q_ref[rows(i), :]
        m = []
        for hh in heads:
            qm = jnp.where(masks[hh], q2, zero)
            s_ctx = _dot_t(k_ctx, qm)
            s_ref[hh, :TILE, :] = s_ctx
            s_loc = _dot_t(k_loc, qm) + bias_ref[case, hh]
            s_ref[hh, TILE:, :] = s_loc
            m.append(jnp.maximum(jnp.max(s_ctx, axis=0, keepdims=True), jnp.max(s_loc, axis=0, keepdims=True)))
        return m

    def update(i, s_ref, m, heads):
        vt = jnp.concatenate([vt_ref[t] for t in key_tiles(i)], axis=1)
        outs = []
        for hh in heads:
            for c in range(n_keys // blk):
                p = jnp.exp2(s_ref[hh, c * blk:(c + 1) * blk, :] - m[hh])
                p_ref[hh, c * blk:(c + 1) * blk, :] = p.astype(jnp.bfloat16)
            vt_h = jnp.concatenate([vt[hh * HEAD_DIM:(hh + 1) * HEAD_DIM], ones], axis=0)
            acc = _dot(vt_h, p_ref[hh])
            outs.append(acc[:HEAD_DIM] * (1.0 / acc[HEAD_DIM:HEAD_DIM + 1]))
        return outs

    def store(i, outs):
        o_ref[rows(i), :] = jnp.concatenate(outs, axis=0).T.astype(jnp.bfloat16)

    def step(i_next, s_next, i_cur, s_cur, m_cur):
        m_next, outs = [], []
        for hh in range(2):
            m_next += scores(i_next, s_next, (hh,))
            outs += update(i_cur, s_cur, m_cur, (hh,))
        store(i_cur, outs)
        return tuple(m_next)

    def body(t, m_a):
        for j in range(0, TILES_PER_TRIP, 2):
            i = TILES_PER_TRIP * t + j
            m_b = step(i + 1, sb_ref, i, sa_ref, m_a)
            m_a = step(i + 2, sa_ref, i + 1, sb_ref, m_b)
        return m_a

    m_a = lax.fori_loop(0, (nt - 1) // TILES_PER_TRIP, body, tuple(scores(0, sa_ref, (0, 1))))
    store(nt - 1, update(nt - 1, sa_ref, m_a, (0, 1)))


def _nbr_attn(q, k, vt, tz, ctx_len):
    n_batch, t_len, d = q.shape
    nt = t_len // TILE
    assert ctx_len == TILE and TILE % GRID_W == 0 and C_KH == 2 * (TILE // GRID_W)
    assert nt >= 4, "needs distinct first / interior / last latent tiles"
    assert (nt - 1) % TILES_PER_TRIP == 0, "whole loop trips plus a last tile"
    seq = pl.BlockSpec((None, t_len, LANES), lambda b, p: (b, 0, p))
    return pl.pallas_call(
        functools.partial(_nbr_body, nt=nt),
        grid=(n_batch, C_HEADS // 2),
        in_specs=[seq, seq, pl.BlockSpec((None, nt, LANES, TILE), lambda b, p: (b, 0, p, 0)),
                  pl.BlockSpec((2,) + tz.shape[1:], lambda b, p: (p, 0, 0, 0))],
        out_specs=seq,
        out_shape=jax.ShapeDtypeStruct((n_batch, t_len, d), jnp.bfloat16),
        scratch_shapes=[pltpu.VMEM((4, 2, 3 * TILE, TILE), jnp.float32),
                        pltpu.VMEM((2, 4 * TILE, TILE), jnp.float32),
                        pltpu.VMEM((2, 4 * TILE, TILE), jnp.float32),
                        pltpu.VMEM((2, 4 * TILE, TILE), jnp.bfloat16)],
        compiler_params=_cparams(("parallel", "parallel")),
        name="nbr_attn",
    )(q, k, vt, tz)


def _nbr_toeplitz(rpb):
    n_off = 2 * C_KW - 1
    pad = GRID_W - C_KW
    ext = jnp.pad(rpb, ((0, 0), (0, 0), (pad, pad)), mode="edge")[:, :, ::-1]
    assert ext.shape[-1] == n_off + 2 * pad == 2 * GRID_W - 1
    n_h, n_r, length = ext.shape
    flat = jnp.tile(jnp.pad(ext, ((0, 0), (0, 0), (0, 1))), (1, 1, GRID_W))
    tz = flat[:, :, GRID_W - 1:GRID_W - 1 + GRID_W * length].reshape(n_h, n_r, GRID_W, length)[..., :GRID_W]
    qc = np.arange(GRID_W)[None, :]
    kc = np.arange(GRID_W)[:, None]
    cs = np.clip(qc - C_KW // 2, 0, GRID_W - C_KW)
    col_ok = jnp.asarray((kc >= cs) & (kc < cs + C_KW))
    tz = jnp.where(col_ok, tz * LOG2E, NEG)
    return jnp.concatenate([tz[:, 1:], tz[:, :-1]], axis=-1)


def _residual(h_ref, mod_ref, o1_ref, o2_ref, gate_ref, w_ref):
    half = D_MODEL // 2
    g = gate_ref[...].astype(jnp.float32)
    og1 = (o1_ref[...].astype(jnp.float32) * g[:, :half]).astype(jnp.bfloat16)
    og2 = (o2_ref[...].astype(jnp.float32) * g[:, half:]).astype(jnp.bfloat16)
    y = _dot(og1, w_ref[:half, :]) + _dot(og2, w_ref[half:, :])
    return h_ref[...] + mod_ref[:, 2 * D_MODEL:] * y


def _residual_then_proj_body(h_ref, mod_ref, o1_ref, o2_ref, gate_ref, w_ref, *rest, core):
    *proj_refs, h_out_ref = rest
    h_new = _residual(h_ref, mod_ref, o1_ref, o2_ref, gate_ref, w_ref)
    h_out_ref[...] = h_new
    core(h_new, *proj_refs)


def _prev_operands(prev, n_batch):
    mods_p, o1, o2, gate, w, o2_block = prev
    d = D_MODEL
    row = lambda wd, blk=0: pl.BlockSpec((None, TILE, wd), lambda b, i: (b, i, blk))
    specs = [_mod_spec(n_batch), row(d // 2), row(d // 2, o2_block), row(d), pl.BlockSpec(w.shape, lambda b, i: (0, 0))]
    return specs, [mods_p, o1, o2, gate, w]


def _out_final_body(h_ref, mod_ref, o1_ref, o2_ref, gate_ref, w_ref, fg_ref, out_ref):
    out_ref[...] = _rms(_residual(h_ref, mod_ref, o1_ref, o2_ref, gate_ref, w_ref), fg_ref[...])


def _out_proj_final(h, mods_l, o1, o2, gate, w, o2_block, final_g):
    n_batch, t_len, d = h.shape
    nt = t_len // TILE
    row = lambda wd, blk=0: pl.BlockSpec((None, TILE, wd), lambda b, i: (b, i + 1, blk))
    mod = pl.BlockSpec((None, 1, 3 * D_MODEL), lambda b, i: (b, 0, 0))
    full = lambda a: pl.BlockSpec(a.shape, lambda b, i: (0,) * a.ndim)
    return pl.pallas_call(
        _out_final_body,
        grid=(n_batch, nt - 1),
        in_specs=[row(d), mod, row(d // 2), row(d // 2, o2_block), row(d), full(w), full(final_g)],
        out_specs=pl.BlockSpec((None, TILE, d), lambda b, i: (b, i, 0)),
        out_shape=jax.ShapeDtypeStruct((n_batch, t_len - TILE, d), jnp.float32),
        compiler_params=_cparams(("parallel", "arbitrary")),
        name="out_proj_final",
    )(h, mods_l, o1, o2, gate, w, final_g)


def _deinterleave(n):
    return np.concatenate([np.arange(0, n, 2), np.arange(1, n, 2)])


def _take_cols(w, cols):
    cols = np.asarray(cols)
    picked = jnp.take(w, jnp.asarray(np.maximum(cols, 0)), axis=1)
    return jnp.where(jnp.asarray(cols >= 0)[None, :], picked, 0.0)


def _even_layout():
    perm = _deinterleave(HEAD_DIM)
    qa0, ka0 = 0, A_HEADS * HEAD_DIM
    va0 = ka0 + A_KV_HEADS * HEAD_DIM
    za0 = va0 + A_KV_HEADS * HEAD_DIM
    cq0 = za0 + A_HEADS * HEAD_DIM
    ckv0 = cq0 + B_Q_LORA
    kr0 = ckv0 + B_KV_LORA
    zb0 = kr0 + B_ROPE
    cols = []
    cols += [qa0 + hd * HEAD_DIM + perm for hd in _A_SLOTS]
    cols += [ka0 + kv * HEAD_DIM + perm for kv in range(A_KV_HEADS)]
    va_cols = va0 + np.arange(A_KV_HEADS * HEAD_DIM)
    cols += [za0 + hd * HEAD_DIM + np.arange(HEAD_DIM) for hd in _A_SLOTS]
    cols += [zb0 + np.arange(B_HEADS * B_V)]
    cols += [cq0 + np.arange(B_Q_LORA), ckv0 + np.arange(B_KV_LORA)]
    cols += [np.full(B_NOPE, -1), kr0 + _deinterleave(B_ROPE), np.full(LANES - B_NOPE - B_ROPE, -1)]
    in_cols = np.concatenate(cols)
    assert in_cols.shape[0] == _AB_COLS

    uq = []
    for hd in range(B_HEADS):
        base = hd * (B_NOPE + B_ROPE)
        uq += [base + np.arange(B_NOPE), base + B_NOPE + _deinterleave(B_ROPE),
               np.full(LANES - B_NOPE - B_ROPE, -1)]
    ukv_k, ukv_v = [], []
    pad = np.full(LANES - B_NOPE, -1)
    for hd in range(B_HEADS):
        base = hd * (B_NOPE + B_V)
        ukv_k += [base + np.arange(B_NOPE), pad]
        ukv_v += [base + B_NOPE + np.arange(B_V)]
    out_rows = np.concatenate([hd * HEAD_DIM + np.arange(HEAD_DIM) for hd in _A_SLOTS]
                              + [A_HEADS * HEAD_DIM + np.arange(B_HEADS * B_V)])
    return in_cols, va_cols, np.concatenate(uq), np.concatenate(ukv_k), np.concatenate(ukv_v), out_rows


def _rope_tables(seq_len, ctx_len):
    t = jnp.arange(seq_len, dtype=jnp.int32)
    row = (t // GRID_W).astype(jnp.float32)
    col = (t % GRID_W).astype(jnp.float32)

    def angles(rot_dim):
        ppa = rot_dim // 4
        inv = ROPE_THETA ** (-jnp.arange(ppa, dtype=jnp.float32) / ppa)
        ang = jnp.concatenate([row[:, None] * inv, col[:, None] * inv], axis=-1)
        return jnp.cos(ang), jnp.sin(ang)

    def with_ctx(c, s1, s2):
        pad = lambda a, v: jnp.concatenate([jnp.full((ctx_len, LANES), v, jnp.float32), a], axis=0)
        return pad(c, 1.0), pad(s1, 0.0), pad(s2, 0.0)

    cos_a, sin_a = angles(HEAD_DIM)
    z32 = jnp.zeros_like(sin_a)
    tabs_a = with_ctx(jnp.tile(cos_a, (1, 4)), jnp.tile(jnp.concatenate([-sin_a, z32], 1), (1, 2)),
                      jnp.tile(jnp.concatenate([z32, sin_a], 1), (1, 2)))
    cos_b, sin_b = angles(B_ROPE)
    one64 = jnp.ones((seq_len, B_NOPE), jnp.float32)
    z64 = jnp.zeros((seq_len, B_NOPE), jnp.float32)
    z16 = jnp.zeros_like(sin_b)
    one32 = jnp.ones((seq_len, LANES - B_NOPE - B_ROPE), jnp.float32)
    z32b = jnp.zeros((seq_len, LANES - B_NOPE - B_ROPE), jnp.float32)
    tabs_b = with_ctx(jnp.concatenate([one64, cos_b, cos_b, one32], 1),
                      jnp.concatenate([z64, -sin_b, z16, z32b], 1),
                      jnp.concatenate([z64, z16, sin_b, z32b], 1))
    return tabs_a, tabs_b


def kernel(x, c, ctx, c_ctx, ada_w, ada_b, norm_g, ab_in_w, ab_out_w, a_sink, b_q_norm_g, b_w_uq,
           b_kv_norm_g, b_w_ukv, c_in_w, c_out_w, c_rpb, final_g):
    n_batch, seq_len, d = x.shape
    ctx_len = ctx.shape[1]
    depth = ada_w.shape[0]
    assert d == D_MODEL and ctx_len == TILE and seq_len % TILE == 0
    bf = jnp.bfloat16

    mod_rows = -(-(n_batch + 1) // 8) * 8
    cc = jnp.concatenate([c, c_ctx[None, :], jnp.zeros((mod_rows - n_batch - 1, d), jnp.float32)], axis=0)
    mods = _mods(cc, ada_w, ada_b).reshape(depth, mod_rows, 1, 3 * d)

    in_cols, va_cols, uq_cols, uk_cols, uv_cols, out_rows = _even_layout()
    tabs_a, tabs_b = _rope_tables(seq_len, ctx_len)
    h = (ctx, x)
    row2 = lambda v: v.reshape(1, -1)

    prev = None
    for layer in range(depth):
        i = layer // 2
        g = row2(norm_g[layer])
        if layer % 2 == 0:
            in_w, ukv_w = ab_in_w[i].astype(bf), b_w_ukv[i].astype(bf)
            w_in = _take_cols(in_w, in_cols)
            wuq = _take_cols(b_w_uq[i].astype(bf), uq_cols)
            wuk = _take_cols(ukv_w, uk_cols)
            wvt = _take_cols(ukv_w, uv_cols).T
            w_out = jnp.take(ab_out_w[i].astype(bf), jnp.asarray(out_rows), axis=0)
            sink = jnp.take(a_sink[i], jnp.asarray(_A_SLOTS))
            wvat = _take_cols(in_w, va_cols).T
            outs = _proj_even(h, mods[layer], g, w_in, wvat, row2(b_q_norm_g[i]), wuq,
                              row2(b_kv_norm_g[i]), wuk, wvt, tabs_a, tabs_b, prev=prev)
            if layer == 0 or prev:
                h = outs[-1]
            qa, ka, vat, gate, qb, kb, vbt = outs[:7]
            o1 = _win_attn(qa, ka, vat, sink, ctx_len)
            o2 = _mla_attn(qb, kb, vbt, ctx_len)
            o2_block = 0
        else:
            w = c_in_w[i].astype(bf)
            w_qkz = jnp.concatenate([w[:, :2 * d], w[:, 3 * d:]], axis=1)
            outs = _proj_odd(h, mods[layer], g, w_qkz, w[:, 2 * d:3 * d].T, prev=prev)
            if prev:
                h = outs[-1]
            q, k, vt, gate = outs[:4]
            o1 = o2 = _nbr_attn(q, k, vt, _nbr_toeplitz(c_rpb[i]), ctx_len)
            o2_block = 1
            w_out = c_out_w[i].astype(bf)
        prev = (mods[layer], o1, o2, gate, w_out, o2_block)
    return _out_proj_final(h, *prev, row2(final_g))
```
